```python
import math
import jax
import jax.numpy as jnp
from jax import lax
import numpy as np

D_MODEL = 2048
BATCH = 4
SEQ = 2048
DEPTH = 1
DEC_BATCH = 128
DEC_SEQ = 8
PAST_LEN = 16384
PAGE_SIZE = 128

MIX_WIDTH = D_MODEL
LRU_WIDTH = MIX_WIDTH // 2
LRU_HEADS = 8
LRU_BLOCK = LRU_WIDTH // LRU_HEADS
LRU_C = 8.0
CONV_W = 4
DN_WIDTH = MIX_WIDTH - LRU_WIDTH
DN_HEADS = 8
DN_DK = 128
DN_DV = DN_WIDTH // DN_HEADS
DN_KEY = DN_HEADS * DN_DK
DN_CONV_CH = 2 * DN_KEY + DN_WIDTH
DN_CHUNK = 64
IN_COLS = 2 * LRU_WIDTH + DN_CONV_CH + DN_WIDTH + 2 * DN_HEADS
PEER_HEADS = 8
PEER_NKEYS = 128
PEER_N_EXPERTS = PEER_NKEYS * PEER_NKEYS
PEER_QDIM = 256
PEER_HALF = PEER_QDIM // 2
PEER_TOPK = 16
PEER_BLOCK = 128
NORM_EPS = 1e-6

kernel_name = 'hymba_rglru_gdn_peer_adaln_step'


def _rmsnorm(x, w):
    xf = x.astype(jnp.float32)
    y = xf * lax.rsqrt(jnp.mean(xf * xf, axis=-1, keepdims=True) + NORM_EPS)
    return (y * w.astype(jnp.float32)).astype(x.dtype)


def _l2norm(x):
    return x * lax.rsqrt(jnp.sum(x * x, axis=-1, keepdims=True) + NORM_EPS)


def _causal_conv(x, buf, w, b=None):
    T = x.shape[1]
    xp = jnp.concatenate([buf.astype(x.dtype), x], axis=1)
    y = sum(xp[:, k:k + T] * w[k] for k in range(CONV_W))
    if b is not None:
        y = y + b
    return y, xp[:, -(CONV_W - 1):]


def _rg_lru(x, h0, wa, ba, wx, bx, lam, reset_first):
    Bn, T, W = x.shape
    xb = x.reshape(Bn, T, LRU_HEADS, LRU_BLOCK)
    r = jax.nn.sigmoid((jnp.einsum('bthi,hij->bthj', xb, wa).reshape(Bn, T, W) + ba).astype(jnp.float32))
    i = jax.nn.sigmoid((jnp.einsum('bthi,hij->bthj', xb, wx).reshape(Bn, T, W) + bx).astype(jnp.float32))
    log_a = -LRU_C * r * jax.nn.softplus(-lam.astype(jnp.float32))
    a = jnp.exp(log_a)
    mult = jnp.sqrt(jnp.maximum(-jnp.expm1(2.0 * log_a), 0.0))
    if reset_first:
        mult = mult.at[:, 0].set(1.0)
    u = mult * i * x.astype(jnp.float32)

    def step(h, inp):
        a_t, u_t = inp
        h = a_t * h + u_t
        return h, h

    hT, hs = lax.scan(step, h0.astype(jnp.float32), (jnp.swapaxes(a, 0, 1), jnp.swapaxes(u, 0, 1)))
    return jnp.swapaxes(hs, 0, 1), hT


def _gated_delta(q, k, v, g, beta, S0):
    Bn, T, H, _ = q.shape
    C = min(DN_CHUNK, T)
    n = -(-T // C)
    pad = n * C - T
    f32 = jnp.float32
    q = _l2norm(q.astype(f32)) * (DN_DK ** -0.5)
    k = _l2norm(k.astype(f32))
    v = v.astype(f32)

    def blocks(t):
        t = jnp.pad(t, [(0, 0), (0, pad)] + [(0, 0)] * (t.ndim - 2))
        t = jnp.swapaxes(t, 1, 2)
        return t.reshape(t.shape[:2] + (n, C) + t.shape[3:])

    q, k, v, g, beta = blocks(q), blocks(k), blocks(v), blocks(g), blocks(beta)
    g = jnp.cumsum(g, axis=-1)
    tri = jnp.tril(jnp.ones((C, C), bool))
    strict = jnp.tril(jnp.ones((C, C), bool), -1)
    diff = g[..., :, None] - g[..., None, :]
    decay = jnp.where(tri, jnp.exp(jnp.where(tri, diff, 0.0)), 0.0)
    kb = k * beta[..., None]
    vb = v * beta[..., None]
    L = jnp.where(strict, jnp.einsum('bhncd,bhnsd->bhncs', kb, k) * decay, 0.0)
    A = L + jnp.eye(C, dtype=f32)
    rhs = jnp.concatenate([vb, kb * jnp.exp(g)[..., None]], axis=-1)
    sol = lax.linalg.triangular_solve(A, rhs, left_side=True, lower=True, unit_diagonal=True)
    u, w = sol[..., :DN_DV], sol[..., DN_DV:]
    attn = jnp.where(tri, jnp.einsum('bhncd,bhnsd->bhncs', q, k) * decay, 0.0)

    def step(S, inp):
        q_i, k_i, u_i, w_i, g_i, a_i = inp
        v_new = u_i - jnp.einsum('bhcd,bhde->bhce', w_i, S)
        o = jnp.einsum('bhcd,bhde->bhce', q_i * jnp.exp(g_i)[..., None], S) + jnp.einsum('bhcs,bhse->bhce', a_i, v_new)
        g_last = g_i[..., -1]
        S = S * jnp.exp(g_last)[..., None, None] + jnp.einsum(
            'bhcd,bhce->bhde', k_i * jnp.exp(g_last[..., None] - g_i)[..., None], v_new)
        return S, o

    xs = tuple(jnp.moveaxis(t, 2, 0) for t in (q, k, u, w, g, attn))
    S_T, o = lax.scan(step, S0.astype(f32), xs)
    o = jnp.moveaxis(o, 0, 2).reshape(Bn, H, n * C, DN_DV)
    return jnp.swapaxes(o, 1, 2)[:, :T], S_T


def _mixer(h, lru_conv, lru_h, dn_conv, dn_S, reset_first, p):
    Bn, T, _ = h.shape
    f32 = jnp.float32
    proj = h @ p['w_in']
    o1 = LRU_WIDTH
    o2 = 2 * LRU_WIDTH
    o3 = o2 + DN_CONV_CH
    o4 = o3 + DN_WIDTH
    o5 = o4 + DN_HEADS
    x_l, y_l, qkv, z, b_dn, a_dn = jnp.split(proj, [o1, o2, o3, o4, o5], axis=-1)
    xc, new_lru_conv = _causal_conv(x_l, lru_conv, p['lru_conv_w'], p['lru_conv_b'])
    hs, new_lru_h = _rg_lru(xc, lru_h, p['lru_wa'], p['lru_ba'], p['lru_wx'], p['lru_bx'], p['lru_lambda'], reset_first)
    out_lru = (hs * jax.nn.gelu(y_l.astype(f32))).astype(h.dtype)
    qkv_c, new_dn_conv = _causal_conv(qkv, dn_conv, p['dn_conv_w'])
    qkv_c = jax.nn.silu(qkv_c)
    q, k, v = jnp.split(qkv_c, [DN_KEY, 2 * DN_KEY], axis=-1)
    q = q.reshape(Bn, T, DN_HEADS, DN_DK)
    k = k.reshape(Bn, T, DN_HEADS, DN_DK)
    v = v.reshape(Bn, T, DN_HEADS, DN_DV)
    beta = jax.nn.sigmoid(b_dn.astype(f32))
    g = -jnp.exp(p['dn_A_log'].astype(f32)) * jax.nn.softplus(a_dn.astype(f32) + p['dn_dt_bias'].astype(f32))
    o, new_dn_S = _gated_delta(q, k, v, g, beta, dn_S)
    o = _rmsnorm(o, p['dn_norm_w']) * jax.nn.silu(z.reshape(Bn, T, DN_HEADS, DN_DV).astype(f32))
    out_dn = o.reshape(Bn, T, DN_WIDTH).astype(h.dtype)
    mix = jnp.concatenate([out_lru, out_dn], axis=-1) @ p['w_out']
    return mix, (new_lru_conv, new_lru_h, new_dn_conv, new_dn_S)


def _peer(h, p):
    Bn, T, D = h.shape
    N = Bn * T
    nb = -(-N // PEER_BLOCK)
    xt = jnp.pad(h.reshape(N, D), ((0, nb * PEER_BLOCK - N), (0, 0))).reshape(nb, PEER_BLOCK, D)

    def block_fn(xb):
        qry = (xb @ p['peer_wq']).reshape(PEER_BLOCK, PEER_HEADS, 2, PEER_HALF)
        s = jnp.einsum('phsd,hsnd->phsn', qry, p['peer_subkeys']).astype(jnp.float32)
        s1, i1 = lax.top_k(s[:, :, 0], PEER_TOPK)
        s2, i2 = lax.top_k(s[:, :, 1], PEER_TOPK)
        cand = (s1[..., :, None] + s2[..., None, :]).reshape(PEER_BLOCK, PEER_HEADS, PEER_TOPK * PEER_TOPK)
        cidx = (i1[..., :, None] * PEER_NKEYS + i2[..., None, :]).reshape(PEER_BLOCK, PEER_HEADS, PEER_TOPK * PEER_TOPK)
        top_s, pos = lax.top_k(cand, PEER_TOPK)
        idx = jnp.take_along_axis(cidx, pos, axis=-1)
        gates = jax.nn.softmax(top_s, axis=-1)
        u_rows = p['peer_u'][idx]
        act = jax.nn.gelu(jnp.einsum('phkd,pd->phk', u_rows, xb).astype(jnp.float32))
        coef = (gates * act).astype(xb.dtype)
        return jnp.einsum('phk,phkd->pd', coef, p['peer_v'][idx])

    out = lax.map(block_fn, xt).reshape(nb * PEER_BLOCK, D)[:N]
    return out.reshape(Bn, T, D)


def _layer(x, c, lru_conv, lru_h, dn_conv, dn_S, reset_first, p):
    mod = jax.nn.silu(c) @ p['w_ada'] + p['b_ada']
    sh1, sc1, g1, sh2, sc2, g2 = jnp.split(mod, 6, axis=-1)
    h = _rmsnorm(x, p['norm_mix_w']) * (1.0 + sc1[:, None]) + sh1[:, None]
    mix, new_states = _mixer(h, lru_conv, lru_h, dn_conv, dn_S, reset_first, p)
    x = x + g1[:, None] * mix
    h = _rmsnorm(x, p['norm_ffn_w']) * (1.0 + sc2[:, None]) + sh2[:, None]
    x = x + g2[:, None] * _peer(h, p)
    return x, new_states


def setup_inputs(seed: int = 0) -> dict:
    key = jax.random.key(seed)
    ks = jax.random.split(key, 32)
    nrm = jax.random.normal
    f32 = jnp.float32
    L, D = DEPTH, D_MODEL
    lam_a = jax.random.uniform(ks[10], (L, LRU_WIDTH), f32, 0.9, 0.999) ** (1.0 / LRU_C)
    dt = jnp.exp(jax.random.uniform(ks[14], (L, DN_HEADS), f32, math.log(1e-3), math.log(1e-1)))
    return {
        'x_prompt': nrm(ks[0], (BATCH, SEQ, D), f32),
        'x_sample': nrm(ks[1], (DEC_BATCH, DEC_SEQ, D), f32),
        'state_lru_conv': nrm(ks[2], (L, DEC_BATCH, CONV_W - 1, LRU_WIDTH), f32),
        'state_lru_h': 0.5 * nrm(ks[3], (L, DEC_BATCH, LRU_WIDTH), f32),
        'state_dn_conv': nrm(ks[4], (L, DEC_BATCH, CONV_W - 1, DN_CONV_CH), f32),
        'state_dn_S': 0.1 * nrm(ks[5], (L, DEC_BATCH, DN_HEADS, DN_DK, DN_DV), f32),
        'c_prompt': nrm(ks[6], (BATCH, D), f32),
        'c_sample': nrm(ks[7], (DEC_BATCH, D), f32),
        'w_ada': 0.5 * D ** -0.5 * nrm(ks[8], (L, D, 6 * D), f32),
        'b_ada': 0.01 * nrm(ks[9], (L, 6 * D), f32),
        'norm_mix_w': 1.0 + 0.02 * nrm(ks[11], (L, D), f32),
        'norm_ffn_w': 1.0 + 0.02 * nrm(ks[12], (L, D), f32),
        'w_in': D ** -0.5 * nrm(ks[13], (L, D, IN_COLS), f32),
        'lru_conv_w': CONV_W ** -0.5 * nrm(ks[15], (L, CONV_W, LRU_WIDTH), f32),
        'lru_conv_b': 0.01 * nrm(ks[16], (L, LRU_WIDTH), f32),
        'lru_wa': LRU_BLOCK ** -0.5 * nrm(ks[17], (L, LRU_HEADS, LRU_BLOCK, LRU_BLOCK), f32),
        'lru_ba': 0.1 * nrm(ks[18], (L, LRU_WIDTH), f32),
        'lru_wx': LRU_BLOCK ** -0.5 * nrm(ks[19], (L, LRU_HEADS, LRU_BLOCK, LRU_BLOCK), f32),
        'lru_bx': 0.1 * nrm(ks[20], (L, LRU_WIDTH), f32),
        'lru_lambda': jnp.log(lam_a) - jnp.log1p(-lam_a),
        'dn_conv_w': CONV_W ** -0.5 * nrm(ks[21], (L, CONV_W, DN_CONV_CH), f32),
        'dn_A_log': jnp.log(jax.random.uniform(ks[22], (L, DN_HEADS), f32, 1.0, 16.0)),
        'dn_dt_bias': dt + jnp.log(-jnp.expm1(-dt)),
        'dn_norm_w': 1.0 + 0.02 * nrm(ks[23], (L, DN_DV), f32),
        'w_out': MIX_WIDTH ** -0.5 * nrm(ks[24], (L, MIX_WIDTH, D), f32),
        'peer_wq': D ** -0.5 * nrm(ks[25], (L, D, PEER_HEADS * PEER_QDIM), f32),
        'peer_subkeys': PEER_HALF ** -0.5 * nrm(ks[26], (L, PEER_HEADS, 2, PEER_NKEYS, PEER_HALF), f32),
        'peer_u': D ** -0.5 * nrm(ks[27], (L, PEER_N_EXPERTS, D), f32),
        'peer_v': PEER_HEADS ** -0.5 * nrm(ks[28], (L, PEER_N_EXPERTS, D), f32),
        'final_norm_w': 1.0 + 0.02 * nrm(ks[29], (D,), f32),
    }


def reference(x_prompt, x_sample, state_lru_conv, state_lru_h, state_dn_conv, state_dn_S, c_prompt, c_sample,
              w_ada, b_ada, norm_mix_w, norm_ffn_w, w_in, lru_conv_w, lru_conv_b, lru_wa, lru_ba, lru_wx, lru_bx,
              lru_lambda, dn_conv_w, dn_A_log, dn_dt_bias, dn_norm_w, w_out, peer_wq, peer_subkeys, peer_u, peer_v,
              final_norm_w):
    Bp = x_prompt.shape[0]
    xp, xs = x_prompt, x_sample
    p_lc, p_lh, p_dc, p_dS = [], [], [], []
    s_lc, s_lh, s_dc, s_dS = [], [], [], []
    for l in range(DEPTH):
        p = {'w_ada': w_ada[l], 'b_ada': b_ada[l], 'norm_mix_w': norm_mix_w[l], 'norm_ffn_w': norm_ffn_w[l],
             'w_in': w_in[l], 'lru_conv_w': lru_conv_w[l], 'lru_conv_b': lru_conv_b[l], 'lru_wa': lru_wa[l],
             'lru_ba': lru_ba[l], 'lru_wx': lru_wx[l], 'lru_bx': lru_bx[l], 'lru_lambda': lru_lambda[l],
             'dn_conv_w': dn_conv_w[l], 'dn_A_log': dn_A_log[l], 'dn_dt_bias': dn_dt_bias[l],
             'dn_norm_w': dn_norm_w[l], 'w_out': w_out[l], 'peer_wq': peer_wq[l],
             'peer_subkeys': peer_subkeys[l], 'peer_u': peer_u[l], 'peer_v': peer_v[l]}
        xp, (lc, lh, dc, dS) = _layer(
            xp, c_prompt,
            jnp.zeros((Bp, CONV_W - 1, LRU_WIDTH), xp.dtype), jnp.zeros((Bp, LRU_WIDTH), jnp.float32),
            jnp.zeros((Bp, CONV_W - 1, DN_CONV_CH), xp.dtype), jnp.zeros((Bp, DN_HEADS, DN_DK, DN_DV), jnp.float32),
            True, p)
        p_lc.append(lc); p_lh.append(lh); p_dc.append(dc); p_dS.append(dS)
        xs, (lc, lh, dc, dS) = _layer(xs, c_sample, state_lru_conv[l], state_lru_h[l], state_dn_conv[l],
                                      state_dn_S[l], False, p)
        s_lc.append(lc); s_lh.append(lh); s_dc.append(dc); s_dS.append(dS)
    y_prompt = _rmsnorm(xp, final_norm_w)
    y_sample = _rmsnorm(xs, final_norm_w)
    return (y_prompt, y_sample,
            jnp.stack(p_lc), jnp.stack(p_lh), jnp.stack(p_dc), jnp.stack(p_dS),
            jnp.stack(s_lc), jnp.stack(s_lh), jnp.stack(s_dc), jnp.stack(s_dS))
```

```python
import functools
import math

import jax
import jax.numpy as jnp
from jax import lax
from jax.experimental import pallas as pl
from jax.experimental.pallas import tpu as pltpu

F32 = jnp.float32
BF16 = jnp.bfloat16
NORM_EPS = 1e-6
LRU_C = 8.0
CONV_W = 4
CONV_PAD = 8
PEER_TOPK = 16
DN_CHUNK = 64
LANES = 128
VMEM_LIMIT_BYTES = 56 * 1024 * 1024
NT_DIMS = (((1,), (1,)), ((), ()))
TN_DIMS = (((0,), (0,)), ((), ()))


def _params(*semantics):
    return pltpu.CompilerParams(dimension_semantics=semantics, vmem_limit_bytes=VMEM_LIMIT_BYTES)


def _tile(n, target, multiple):
    if n <= target:
        return n
    t = (target // multiple) * multiple
    while t >= multiple:
        if n % t == 0:
            return t
        t -= multiple
    raise ValueError(f"no tile for {n} <= {target} in multiples of {multiple}")


def _softplus(x):
    return jnp.maximum(x, 0.0) + jnp.log1p(jnp.exp(-jnp.abs(x)))


def _row_groups(batch, seq, tm):
    r = min(seq, tm)
    assert seq % r == 0 and tm % r == 0 and (batch * seq) % tm == 0
    return tm // r, r, seq // r


def _ada_kernel(c_ref, w_ref, b_ref, o_ref):
    c = c_ref[...]
    a = (c * jax.nn.sigmoid(c)).astype(BF16)
    o_ref[...] = jnp.dot(a, w_ref[...].astype(BF16), preferred_element_type=F32) + b_ref[...]


def _ada(c, w, b):
    bc, d = c.shape
    n = w.shape[1]
    tn = _tile(n, 1024, LANES)
    return pl.pallas_call(
        _ada_kernel,
        out_shape=jax.ShapeDtypeStruct((bc, n), F32),
        grid=(n // tn,),
        in_specs=[pl.BlockSpec((bc, d), lambda j: (0, 0)),
                  pl.BlockSpec((d, tn), lambda j: (0, j)),
                  pl.BlockSpec((1, tn), lambda j: (0, j))],
        out_specs=pl.BlockSpec((bc, tn), lambda j: (0, j)),
        compiler_params=_params("arbitrary"),
        name="ada",
    )(c, w, b.reshape(1, n))


def _inproj_kernel(x_ref, sc_ref, sh_ref, nw_ref, wm_ref, ws_ref, om_ref, os_ref, h_ref):
    @pl.when(pl.program_id(1) == 0)
    def _():
        x = x_ref[...]
        y = x * lax.rsqrt(jnp.mean(x * x, axis=-1, keepdims=True) + NORM_EPS) * nw_ref[...]
        h = y * (1.0 + sc_ref[...]) + sh_ref[...]
        hb = h.reshape(h_ref.shape).astype(BF16)
        h_ref[...] = hb
        os_ref[...] = jnp.dot(hb, ws_ref[...], preferred_element_type=F32)

    om_ref[...] = jnp.dot(h_ref[...], wm_ref[...], preferred_element_type=F32)


def _inproj(x, sc, sh, nw, wm, ws):
    b, t, d = x.shape
    n = b * t
    m = wm.shape[1]
    tm = _tile(n, 512, 8)
    tn = _tile(m, 1536, LANES)
    gb, r, nt = _row_groups(b, t, tm)
    return pl.pallas_call(
        _inproj_kernel,
        out_shape=(jax.ShapeDtypeStruct((n, m), F32), jax.ShapeDtypeStruct((n, LANES), F32)),
        grid=(n // tm, m // tn),
        in_specs=[pl.BlockSpec((gb, r, d), lambda i, j: (i // nt, i % nt, 0)),
                  pl.BlockSpec((gb, 1, d), lambda i, j: (i // nt, 0, 0)),
                  pl.BlockSpec((gb, 1, d), lambda i, j: (i // nt, 0, 0)),
                  pl.BlockSpec((1, 1, d), lambda i, j: (0, 0, 0)),
                  pl.BlockSpec((d, tn), lambda i, j: (0, j)),
                  pl.BlockSpec((d, LANES), lambda i, j: (0, 0))],
        out_specs=(pl.BlockSpec((tm, tn), lambda i, j: (i, j)),
                   pl.BlockSpec((tm, LANES), lambda i, j: (i, 0))),
        scratch_shapes=[pltpu.VMEM((tm, d), BF16)],
        compiler_params=_params("arbitrary", "arbitrary"),
        name="inproj",
    )(x, sc, sh, nw, wm, ws)


def _causal_conv(x_ref, xs_ref, cw_ref):
    t = x_ref.shape[1]
    xs_ref[:, CONV_PAD:CONV_PAD + t, :] = x_ref[...]
    cw = cw_ref[...]
    y = None
    for k in range(CONV_W):
        lo = CONV_PAD - (CONV_W - 1) + k
        term = xs_ref[:, lo:lo + t, :] * cw[k:k + 1, :][None]
        y = term if y is None else y + term
    xs_ref[:, CONV_PAD - (CONV_W - 1):CONV_PAD, :] = xs_ref[:, t + CONV_PAD - (CONV_W - 1):t + CONV_PAD, :]
    return y


def _lru_kernel(xl_ref, yl_ref, cs_ref, h0_ref, cw_ref, cb_ref, wg_ref, ba_ref, bx_ref, lam_ref,
                o_ref, hn_ref, xs_ref, a_ref, u_ref, h_ref, *, reset_first):
    ti = pl.program_id(1)
    bb, tc, lw = xl_ref.shape
    nheads = lw // LANES

    @pl.when(ti == 0)
    def _():
        xs_ref[:, CONV_PAD - (CONV_W - 1):CONV_PAD, :] = cs_ref[...]
        h_ref[...] = h0_ref[...]

    xc = (_causal_conv(xl_ref, xs_ref, cw_ref) + cb_ref[...][None]).reshape(bb * tc, lw)
    sp = _softplus(-lam_ref[...])
    t_glob = lax.broadcasted_iota(jnp.int32, (bb, tc, LANES), 1) + ti * tc
    for hd in range(nheads):
        sl = slice(hd * LANES, (hd + 1) * LANES)
        xh = xc[:, sl]
        gates = jnp.dot(xh.astype(BF16), wg_ref[hd], preferred_element_type=F32)
        r = jax.nn.sigmoid(gates[:, :LANES] + ba_ref[:, sl])
        i = jax.nn.sigmoid(gates[:, LANES:] + bx_ref[:, sl])
        log_a = -LRU_C * r * sp[:, sl]
        a = jnp.exp(log_a)
        mult = jnp.sqrt(jnp.maximum(-jnp.tanh(log_a) * (a * a + 1.0), 0.0))
        u = i * xh
        a3 = a.reshape(bb, tc, LANES)
        m3 = mult.reshape(bb, tc, LANES)
        if reset_first:
            m3 = jnp.where(t_glob == 0, 1.0, m3)
        a_ref[:, :, sl] = a3
        u_ref[:, :, sl] = m3 * u.reshape(bb, tc, LANES)

    def step(t, h):
        h = a_ref[:, pl.ds(t, 1), :] * h + u_ref[:, pl.ds(t, 1), :]
        u_ref[:, pl.ds(t, 1), :] = h
        return h

    h = lax.fori_loop(0, tc, step, h_ref[...], unroll=8)
    h_ref[...] = h
    hn_ref[...] = h
    o_ref[...] = u_ref[...] * jax.nn.gelu(yl_ref[...])


def _lru(pm3, conv_state, h0, cw, cb, wg, ba, bx, lam, *, reset_first, bb, tc):
    b, t, _ = pm3.shape
    lw = lam.shape[-1]
    assert b % bb == 0 and t % tc == 0 and tc % 8 == 0
    row = lambda v: v.reshape(1, lw)
    kern = functools.partial(_lru_kernel, reset_first=reset_first)
    out, hn = pl.pallas_call(
        kern,
        out_shape=(jax.ShapeDtypeStruct((b, t, lw), F32), jax.ShapeDtypeStruct((b, 1, lw), F32)),
        grid=(b // bb, t // tc),
        in_specs=[pl.BlockSpec((bb, tc, lw), lambda i, j: (i, j, 0)),
                  pl.BlockSpec((bb, tc, lw), lambda i, j: (i, j, 1)),
                  pl.BlockSpec((bb, CONV_W - 1, lw), lambda i, j: (i, 0, 0)),
                  pl.BlockSpec((bb, 1, lw), lambda i, j: (i, 0, 0)),
                  pl.BlockSpec((CONV_W, lw), lambda i, j: (0, 0)),
                  pl.BlockSpec((1, lw), lambda i, j: (0, 0)),
                  pl.BlockSpec(wg.shape, lambda i, j: (0, 0, 0)),
                  pl.BlockSpec((1, lw), lambda i, j: (0, 0)),
                  pl.BlockSpec((1, lw), lambda i, j: (0, 0)),
                  pl.BlockSpec((1, lw), lambda i, j: (0, 0))],
        out_specs=(pl.BlockSpec((bb, tc, lw), lambda i, j: (i, j, 0)),
                   pl.BlockSpec((bb, 1, lw), lambda i, j: (i, 0, 0))),
        scratch_shapes=[pltpu.VMEM((bb, tc + CONV_PAD, lw), F32),
                        pltpu.VMEM((bb, tc, lw), F32),
                        pltpu.VMEM((bb, tc, lw), F32),
                        pltpu.VMEM((bb, 1, lw), F32)],
        compiler_params=_params("arbitrary", "arbitrary"),
        name="lru",
    )(pm3, pm3, conv_state, h0.reshape(b, 1, lw), cw, row(cb), wg, row(ba), row(bx), row(lam))
    return out, hn.reshape(b, lw)


def _delta_kernel(q_ref, k_ref, v_ref, z_ref, ps_ref, csq_ref, csk_ref, csv_ref, s0_ref,
                  cwq_ref, cwk_ref, cwv_ref, ea_ref, dtb_ref, nw_ref,
                  o_ref, sn_ref,
                  xq_ref, xk_ref, xv_ref, qn_ref, kn_ref, vn_ref, beta_ref, g_ref, *, chunk, nheads):
    ti = pl.program_id(1)
    bb, tb, w = q_ref.shape
    n_chunks = tb // chunk
    rows = bb * chunk
    assert bb == 1 or n_chunks == 1
    lo = CONV_PAD - (CONV_W - 1)

    @pl.when(ti == 0)
    def _():
        xq_ref[:, lo:CONV_PAD, :] = csq_ref[...]
        xk_ref[:, lo:CONV_PAD, :] = csk_ref[...]
        xv_ref[:, lo:CONV_PAD, :] = csv_ref[...]
        sn_ref[...] = s0_ref[...]

    def conv_silu(x_ref, xs_ref, cw_ref):
        y = _causal_conv(x_ref, xs_ref, cw_ref)
        return (y * jax.nn.sigmoid(y)).reshape(bb * tb, w)

    qc = conv_silu(q_ref, xq_ref, cwq_ref)
    kc = conv_silu(k_ref, xk_ref, cwk_ref)
    vn_ref[...] = conv_silu(v_ref, xv_ref, cwv_ref)
    for hd in range(nheads):
        sl = slice(hd * LANES, (hd + 1) * LANES)
        qh = qc[:, sl]
        kh = kc[:, sl]
        qn_ref[:, sl] = qh * (lax.rsqrt(jnp.sum(qh * qh, axis=-1, keepdims=True) + NORM_EPS) * (LANES ** -0.5))
        kn_ref[:, sl] = kh * lax.rsqrt(jnp.sum(kh * kh, axis=-1, keepdims=True) + NORM_EPS)
    ps = ps_ref[...].reshape(bb * tb, LANES)
    beta_ref[...] = jax.nn.sigmoid(ps)
    g_ref[...] = -ea_ref[...] * _softplus(ps + dtb_ref[...])

    ii = lax.broadcasted_iota(jnp.int32, (rows, rows), 0)
    jj = lax.broadcasted_iota(jnp.int32, (rows, rows), 1)
    if bb == 1:
        le = jj <= ii
        lt = jj < ii
    else:
        same = (ii // chunk) == (jj // chunk)
        le = jnp.logical_and(same, jj <= ii)
        lt = jnp.logical_and(same, jj < ii)
    le_b = jnp.where(le, 1.0, 0.0).astype(BF16)
    n_double = int(math.log2(chunk)) - 1
    assert 2 ** (n_double + 1) == chunk

    def do_chunk(c, carry):
        r0 = pl.multiple_of(c * rows, rows)
        rsl = pl.ds(r0, rows)
        g = g_ref[rsl, :]
        beta = beta_ref[rsl, :]
        g_hi = g.astype(BF16)
        g_lo = (g - g_hi.astype(F32)).astype(BF16)
        gc = (jnp.dot(le_b, g_hi, preferred_element_type=F32)
              + jnp.dot(le_b, g_lo, preferred_element_type=F32))
        gcp = gc if rows == LANES else jnp.concatenate([gc, jnp.zeros((LANES - rows, LANES), F32)], axis=0)
        gct = gcp.T
        eg = jnp.exp(gc)
        for hd in range(nheads):
            sl = slice(hd * LANES, (hd + 1) * LANES)
            gl = nheads + hd
            q = qn_ref[rsl, sl]
            k = kn_ref[rsl, sl]
            v = vn_ref[rsl, sl]
            bcol = beta[:, hd:hd + 1]
            gcol = gc[:, gl:gl + 1]
            grow = gct[gl:gl + 1, :rows]
            decay = jnp.where(le, jnp.exp(jnp.where(le, gcol - grow, 0.0)), 0.0)
            kb = k * bcol
            k16 = k.astype(BF16)
            kk = lax.dot_general(kb.astype(BF16), k16, NT_DIMS, preferred_element_type=F32)
            qk = lax.dot_general(q.astype(BF16), k16, NT_DIMS, preferred_element_type=F32)
            lmat = jnp.where(lt, kk * decay, 0.0)
            attn = jnp.where(le, qk * decay, 0.0)
            p = -lmat
            nmat = p
            for _ in range(n_double):
                p = jnp.dot(p, p, preferred_element_type=F32, precision=lax.Precision.HIGHEST)
                nmat = nmat + p + jnp.dot(nmat, p, preferred_element_type=F32, precision=lax.Precision.HIGHEST)
            egc = eg[:, gl:gl + 1]
            rhs = jnp.concatenate([v * bcol, kb * egc], axis=1)
            sol = rhs + jnp.dot(nmat, rhs, preferred_element_type=F32, precision=lax.Precision.HIGHEST)
            u = sol[:, :LANES]
            wmat = sol[:, LANES:]
            qe = q * egc
            ws_parts, qs_parts, s_olds = [], [], []
            for b in range(bb):
                seg = slice(b * chunk, (b + 1) * chunk)
                s_old = sn_ref[b, hd]
                s_olds.append(s_old)
                lhs = jnp.concatenate([wmat[seg], qe[seg]], axis=0).astype(BF16)
                both = jnp.dot(lhs, s_old.astype(BF16), preferred_element_type=F32)
                ws_parts.append(both[:chunk])
                qs_parts.append(both[chunk:])
            w_s = ws_parts[0] if bb == 1 else jnp.concatenate(ws_parts, axis=0)
            q_s = qs_parts[0] if bb == 1 else jnp.concatenate(qs_parts, axis=0)
            v_new = u - w_s
            v16 = v_new.astype(BF16)
            o = q_s + jnp.dot(attn.astype(BF16), v16, preferred_element_type=F32)
            for b in range(bb):
                seg = slice(b * chunk, (b + 1) * chunk)
                g_last = gc[(b + 1) * chunk - 1:(b + 1) * chunk, gl:gl + 1]
                kd = (k[seg] * jnp.exp(g_last - gcol[seg])).astype(BF16)
                sn_ref[b, hd] = s_olds[b] * jnp.exp(g_last) + lax.dot_general(
                    kd, v16[seg], TN_DIMS, preferred_element_type=F32)
            on = o * lax.rsqrt(jnp.mean(o * o, axis=-1, keepdims=True) + NORM_EPS) * nw_ref[...]
            if bb == 1:
                zz = z_ref[0, rsl, sl]
                o_ref[0, rsl, sl] = on * (zz * jax.nn.sigmoid(zz))
            else:
                zz = z_ref[:, :, sl].reshape(rows, LANES)
                o_ref[:, :, sl] = (on * (zz * jax.nn.sigmoid(zz))).reshape(bb, chunk, LANES)
        return carry

    if n_chunks == 1:
        do_chunk(0, 0)
    else:
        lax.fori_loop(0, n_chunks, do_chunk, 0)


def _delta(pm3, ps3, conv_state, s0, cw, a_log, dt_bias, norm_w, *, bb, tb, chunk):
    b, t, _ = pm3.shape
    nheads = a_log.shape[-1]
    w = nheads * LANES
    assert s0.shape == (b, nheads, LANES, LANES) and nheads <= 8
    assert b % bb == 0 and t % tb == 0 and tb % chunk == 0 and (bb * chunk) % 8 == 0 and bb * chunk <= LANES
    pad = lambda v: jnp.zeros((1, LANES), F32).at[0, nheads:2 * nheads].set(v)
    kern = functools.partial(_delta_kernel, chunk=chunk, nheads=nheads)
    col = lambda c: pl.BlockSpec((bb, tb, w), lambda i, j, c=c: (i, j, c))
    cst = lambda c: pl.BlockSpec((bb, CONV_W - 1, w), lambda i, j, c=c: (i, 0, c))
    cwt = lambda c: pl.BlockSpec((CONV_W, w), lambda i, j, c=c: (0, c))
    vec = pl.BlockSpec((1, LANES), lambda i, j: (0, 0))
    sblk = pl.BlockSpec((bb, nheads, LANES, LANES), lambda i, j: (i, 0, 0, 0))
    out, sn = pl.pallas_call(
        kern,
        out_shape=(jax.ShapeDtypeStruct((b, t, w), F32), jax.ShapeDtypeStruct(s0.shape, F32)),
        grid=(b // bb, t // tb),
        in_specs=[col(2), col(3), col(4), col(5),
                  pl.BlockSpec((bb, tb, LANES), lambda i, j: (i, j, 0)),
                  cst(0), cst(1), cst(2), sblk, cwt(0), cwt(1), cwt(2), vec, vec, vec],
        out_specs=(pl.BlockSpec((bb, tb, w), lambda i, j: (i, j, 0)), sblk),
        scratch_shapes=[pltpu.VMEM((bb, tb + CONV_PAD, w), F32)] * 3
                       + [pltpu.VMEM((bb * tb, w), F32)] * 3
                       + [pltpu.VMEM((bb * tb, LANES), F32)] * 2,
        compiler_params=_params("arbitrary", "arbitrary"),
        name="delta",
    )(pm3, pm3, pm3, pm3, ps3, conv_state, conv_state, conv_state, s0, cw, cw, cw,
      pad(jnp.exp(a_log.astype(F32))), pad(dt_bias.astype(F32)), norm_w.reshape(1, LANES))
    return out, sn


def _outproj_kernel(ol_ref, od_ref, x_ref, g1_ref, sc_ref, sh_ref, nw_ref, wt_ref, wb_ref, x1_ref, h2_ref):
    tm = h2_ref.shape[0]
    a = ol_ref[...].reshape(tm, ol_ref.shape[-1]).astype(BF16)
    b = od_ref[...].reshape(tm, od_ref.shape[-1]).astype(BF16)
    mix = (jnp.dot(a, wt_ref[...], preferred_element_type=F32)
           + jnp.dot(b, wb_ref[...], preferred_element_type=F32))
    x1 = x_ref[...] + g1_ref[...] * mix.reshape(x_ref.shape)
    x1_ref[...] = x1
    y = x1 * lax.rsqrt(jnp.mean(x1 * x1, axis=-1, keepdims=True) + NORM_EPS) * nw_ref[...]
    h2_ref[...] = (y * (1.0 + sc_ref[...]) + sh_ref[...]).reshape(h2_ref.shape).astype(BF16)


def _outproj(ol, od, x, g1, sc, sh, nw, wt, wb):
    b, t, d = x.shape
    n = b * t
    lw, dw = ol.shape[-1], od.shape[-1]
    tm = _tile(n, 256, 16)
    gb, r, nt = _row_groups(b, t, tm)
    tok = lambda c: pl.BlockSpec((gb, r, c), lambda i: (i // nt, i % nt, 0))
    per = pl.BlockSpec((gb, 1, d), lambda i: (i // nt, 0, 0))
    return pl.pallas_call(
        _outproj_kernel,
        out_shape=(jax.ShapeDtypeStruct((b, t, d), F32), jax.ShapeDtypeStruct((n, d), BF16)),
        grid=(n // tm,),
        in_specs=[tok(lw), tok(dw), tok(d), per, per, per,
                  pl.BlockSpec((1, 1, d), lambda i: (0, 0, 0)),
                  pl.BlockSpec((lw, d), lambda i: (0, 0)),
                  pl.BlockSpec((dw, d), lambda i: (0, 0))],
        out_specs=(tok(d), pl.BlockSpec((tm, d), lambda i: (i, 0))),
        compiler_params=_params("arbitrary"),
        name="outproj",
    )(ol, od, x, g1, sc, sh, nw, wt, wb)


def _top_values(s, count):
    vals = []
    work = s
    for r in range(count):
        m = jnp.max(work, axis=0, keepdims=True)
        vals.append(m)
        if r + 1 < count:
            work = jnp.where(work == m, -jnp.inf, work)
    return vals


def _router_kernel(h_ref, wq_ref, sk_ref, s2_ref, e2_ref, thr_ref, e1_ref, q_ref, *, nheads):
    nk = sk_ref.shape[1]
    half = sk_ref.shape[2]
    q_ref[...] = lax.dot_general(wq_ref[...], h_ref[...], NT_DIMS, preferred_element_type=F32)
    p = h_ref.shape[0]
    sub = lax.broadcasted_iota(jnp.int32, (8, p), 0)

    def head(hd, carry):
        base = pl.multiple_of(hd * (2 * half), 2 * half)
        q1 = q_ref[pl.ds(base, half), :].astype(BF16)
        q2 = q_ref[pl.ds(base + half, half), :].astype(BF16)
        s1 = jnp.dot(sk_ref[2 * hd], q1, preferred_element_type=F32)
        s2 = jnp.dot(sk_ref[2 * hd + 1], q2, preferred_element_type=F32)
        nv = PEER_TOPK + 1
        t1 = _top_values(s1, nv)
        t2 = _top_values(s2, nv)
        t2_lo = jnp.concatenate(t2[:8], axis=0)
        t2_hi = jnp.concatenate(t2[8:16], axis=0)
        pieces = [t1[0] + t2_lo, t1[0] + t2_hi, jnp.where(sub < 1, t1[0] + t2[16], -jnp.inf)]
        for a in range(1, nv):
            nb = nv // (a + 1)
            c = t1[a] + t2_lo
            pieces.append(c if nb >= 8 else jnp.where(sub < nb, c, -jnp.inf))
        cand = jnp.concatenate(pieces, axis=0)
        best = _top_values(cand, nv)
        tau = best[PEER_TOPK - 1]
        mid = 0.5 * (tau + best[PEER_TOPK])
        top = t1[0] + t2[0]
        z = jnp.sum(jnp.where(cand >= tau, jnp.exp(cand - top), 0.0), axis=0, keepdims=True)
        s2_ref[hd] = s2
        e2_ref[hd] = jnp.exp(s2 - t2[0]) / z
        thr_ref[hd] = mid - s1
        e1_ref[hd] = jnp.exp(s1 - t1[0])
        return carry

    lax.fori_loop(0, nheads, head, 0)


def _router(h2, wq_t, sk, *, tp):
    n, d = h2.shape
    nheads = sk.shape[0] // 2
    nk = sk.shape[1]
    hq = wq_t.shape[0]
    out = jax.ShapeDtypeStruct((nheads, nk, n), F32)
    oblk = pl.BlockSpec((nheads, nk, tp), lambda i: (0, 0, i))
    return pl.pallas_call(
        functools.partial(_router_kernel, nheads=nheads),
        out_shape=(out, out, out, out),
        grid=(n // tp,),
        in_specs=[pl.BlockSpec((tp, d), lambda i: (i, 0)),
                  pl.BlockSpec((hq, d), lambda i: (0, 0)),
                  pl.BlockSpec(sk.shape, lambda i: (0, 0, 0))],
        out_specs=(oblk, oblk, oblk, oblk),
        scratch_shapes=[pltpu.VMEM((hq, tp), F32)],
        compiler_params=_params("arbitrary"),
        name="router",
    )(h2, wq_t, sk)


def _dense_kernel(h_ref, u_ref, vt_ref, s2_ref, e2_ref, thr_ref, e1_ref, x1_ref, g2_ref, fw_ref,
                  y_ref, acc_ref, *, nheads, rows_per_step):
    e = pl.program_id(1)
    nk = s2_ref.shape[1]

    @pl.when(e == 0)
    def _():
        acc_ref[...] = jnp.zeros_like(acc_ref)

    st = lax.dot_general(u_ref[...], h_ref[...], NT_DIMS, preferred_element_type=F32)
    parts = []
    for ii in range(rows_per_step):
        i1 = e * rows_per_step + ii
        gate = None
        for hd in range(nheads):
            thr = thr_ref[hd, pl.ds(i1, 1), :]
            e1 = e1_ref[hd, pl.ds(i1, 1), :]
            term = jnp.where(s2_ref[hd] >= thr, e2_ref[hd] * e1, 0.0)
            gate = term if gate is None else gate + term
        parts.append((jax.nn.gelu(st[ii * nk:(ii + 1) * nk]) * gate).astype(BF16))
    coef = parts[0] if rows_per_step == 1 else jnp.concatenate(parts, axis=0)
    acc_ref[...] += jnp.dot(vt_ref[...], coef, preferred_element_type=F32)

    @pl.when(e == pl.num_programs(1) - 1)
    def _():
        peer = acc_ref[...].T.reshape(x1_ref.shape)
        x2 = x1_ref[...] + g2_ref[...] * peer
        y_ref[...] = x2 * lax.rsqrt(jnp.mean(x2 * x2, axis=-1, keepdims=True) + NORM_EPS) * fw_ref[...]


def _dense(h2, u16, vt16, s2t, e2t, thr, e1, x1, g2, fw, *, tp, rows_per_step):
    b, t, d = x1.shape
    n = b * t
    nheads, nk, _ = s2t.shape
    ne = u16.shape[0]
    ce = rows_per_step * nk
    gb, r, nt = _row_groups(b, t, tp)
    rblk = pl.BlockSpec((nheads, nk, tp), lambda i, e: (0, 0, i))
    tok = pl.BlockSpec((gb, r, d), lambda i, e: (i // nt, i % nt, 0))
    return pl.pallas_call(
        functools.partial(_dense_kernel, nheads=nheads, rows_per_step=rows_per_step),
        out_shape=jax.ShapeDtypeStruct((b, t, d), F32),
        grid=(n // tp, ne // ce),
        in_specs=[pl.BlockSpec((tp, d), lambda i, e: (i, 0)),
                  pl.BlockSpec((ce, d), lambda i, e: (e, 0)),
                  pl.BlockSpec((d, ce), lambda i, e: (0, e)),
                  rblk, rblk, rblk, rblk, tok,
                  pl.BlockSpec((gb, 1, d), lambda i, e: (i // nt, 0, 0)),
                  pl.BlockSpec((1, 1, d), lambda i, e: (0, 0, 0))],
        out_specs=tok,
        scratch_shapes=[pltpu.VMEM((d, tp), F32)],
        compiler_params=_params("arbitrary", "arbitrary"),
        name="dense",
    )(h2, u16, vt16, s2t, e2t, thr, e1, x1, g2, fw)


def _group(x, mod, lru_conv, lru_h, dn_conv, dn_s, reset_first, p, fw, *, lru_tiles, dn_tiles, tp):
    b, t, d = x.shape
    sh1, sc1, g1, sh2, sc2, g2 = [m.reshape(b, 1, d) for m in jnp.split(mod, 6, axis=-1)]
    pm, ps = _inproj(x, sc1, sh1, p['norm_mix_w'], p['w_main'], p['w_small'])
    pm3 = pm.reshape(b, t, -1)
    ps3 = ps.reshape(b, t, LANES)
    lw = p['lru_lambda'].shape[-1]
    out_lru, new_h = _lru(pm3, lru_conv, lru_h, p['lru_conv_w'], p['lru_conv_b'], p['lru_wg'], p['lru_ba'],
                          p['lru_bx'], p['lru_lambda'], reset_first=reset_first, bb=lru_tiles[0], tc=lru_tiles[1])
    out_dn, new_s = _delta(pm3, ps3, dn_conv, dn_s, p['dn_conv_w'], p['dn_A_log'], p['dn_dt_bias'], p['dn_norm_w'],
                           bb=dn_tiles[0], tb=dn_tiles[1], chunk=dn_tiles[2])
    keep = CONV_W - 1
    new_lru_conv = pm3[:, t - keep:, :lw]
    new_dn_conv = pm3[:, t - keep:, 2 * lw:2 * lw + dn_conv.shape[-1]]
    x1, h2 = _outproj(out_lru, out_dn, x, g1, sc2, sh2, p['norm_ffn_w'], p['w_out_top'], p['w_out_bot'])
    s2t, e2t, thr, e1 = _router(h2, p['wq_t'], p['subkeys'], tp=tp)
    y = _dense(h2, p['peer_u'], p['peer_vt'], s2t, e2t, thr, e1, x1, g2, fw, tp=tp, rows_per_step=4)
    return y, (new_lru_conv, new_h, new_dn_conv, new_s)


def kernel(x_prompt, x_sample, state_lru_conv, state_lru_h, state_dn_conv, state_dn_S, c_prompt, c_sample,
           w_ada, b_ada, norm_mix_w, norm_ffn_w, w_in, lru_conv_w, lru_conv_b, lru_wa, lru_ba, lru_wx, lru_bx,
           lru_lambda, dn_conv_w, dn_A_log, dn_dt_bias, dn_norm_w, w_out, peer_wq, peer_subkeys, peer_u, peer_v,
           final_norm_w):
    depth = w_ada.shape[0]
    assert depth == 1, "the final norm is fused into the last layer's expert kernel"
    bp, seq, d = x_prompt.shape
    bs, dseq, _ = x_sample.shape
    lw = lru_lambda.shape[-1]
    dn_heads = dn_A_log.shape[-1]
    dn_w = dn_heads * LANES
    conv_ch = dn_conv_w.shape[-1]
    assert lru_wa.shape[-1] == LANES and lw == dn_w and conv_ch == 3 * dn_w
    assert peer_subkeys.shape[-1] == LANES and peer_subkeys.shape[-2] == LANES
    main = 2 * lw + conv_ch + dn_w
    l = 0
    w_small = jnp.zeros((d, LANES), F32).at[:, :2 * dn_heads].set(w_in[l][:, main:]).astype(BF16)
    nheads = peer_subkeys.shape[1]
    p = {
        'norm_mix_w': norm_mix_w[l].reshape(1, 1, d), 'norm_ffn_w': norm_ffn_w[l].reshape(1, 1, d),
        'w_main': w_in[l][:, :main].astype(BF16), 'w_small': w_small,
        'lru_conv_w': lru_conv_w[l], 'lru_conv_b': lru_conv_b[l],
        'lru_wg': jnp.concatenate([lru_wa[l], lru_wx[l]], axis=-1).astype(BF16),
        'lru_ba': lru_ba[l], 'lru_bx': lru_bx[l], 'lru_lambda': lru_lambda[l],
        'dn_conv_w': dn_conv_w[l], 'dn_A_log': dn_A_log[l], 'dn_dt_bias': dn_dt_bias[l], 'dn_norm_w': dn_norm_w[l],
        'w_out_top': w_out[l][:lw].astype(BF16), 'w_out_bot': w_out[l][lw:].astype(BF16),
        'wq_t': peer_wq[l].T.astype(BF16),
        'subkeys': peer_subkeys[l].reshape(nheads * 2, LANES, LANES).astype(BF16),
        'peer_u': peer_u[l].astype(BF16), 'peer_vt': peer_v[l].T.astype(BF16),
    }
    fw = final_norm_w.reshape(1, 1, d)
    mod = _ada(jnp.concatenate([c_prompt, c_sample], axis=0), w_ada[l], b_ada[l])
    zeros = lambda *s: jnp.zeros(s, F32)
    yp, sp = _group(x_prompt, mod[:bp], zeros(bp, CONV_W - 1, lw), zeros(bp, lw), zeros(bp, CONV_W - 1, conv_ch),
                    zeros(bp, dn_heads, LANES, LANES), True, p, fw,
                    lru_tiles=(bp, _tile(seq, 128, 8)), dn_tiles=(1, _tile(seq, 256, DN_CHUNK), min(DN_CHUNK, seq)),
                    tp=_tile(bp * seq, 512, LANES))
    ys, ss = _group(x_sample, mod[bp:], state_lru_conv[l], state_lru_h[l], state_dn_conv[l], state_dn_S[l],
                    False, p, fw,
                    lru_tiles=(_tile(bs, 32, 1), dseq), dn_tiles=(_tile(bs, DN_CHUNK // dseq, 1), dseq, dseq),
                    tp=_tile(bs * dseq, 512, LANES))
    stack = lambda v: v[None]
    return (yp, ys, stack(sp[0]), stack(sp[1]), stack(sp[2]), stack(sp[3]),
            stack(ss[0]), stack(ss[1]), stack(ss[2]), stack(ss[3]))
```

```python
import functools
import math

import jax
import jax.numpy as jnp
from jax import lax
from jax.experimental import pallas as pl
from jax.experimental.pallas import tpu as pltpu

F32 = jnp.float32
BF16 = jnp.bfloat16
NORM_EPS = 1e-6
LRU_C = 8.0
CONV_W = 4
CONV_PAD = 8
PEER_TOPK = 16
DN_CHUNK = 64
LANES = 128
VMEM_LIMIT_BYTES = 60 * 1024 * 1024
NT_DIMS = (((1,), (1,)), ((), ()))
TN_DIMS = (((0,), (0,)), ((), ()))


def _params(*semantics):
    return pltpu.CompilerParams(dimension_semantics=semantics, vmem_limit_bytes=VMEM_LIMIT_BYTES)


def _tile(n, target, multiple):
    if n <= target:
        return n
    t = (target // multiple) * multiple
    while t >= multiple:
        if n % t == 0:
            return t
        t -= multiple
    raise ValueError(f"no tile for {n} <= {target} in multiples of {multiple}")


def _softplus(x):
    return jnp.maximum(x, 0.0) + jnp.log1p(jnp.exp(-jnp.abs(x)))


def _row_groups(batch, seq, tm):
    r = min(seq, tm)
    assert seq % r == 0 and tm % r == 0 and (batch * seq) % tm == 0
    return tm // r, r, seq // r


def _ada_kernel(c_ref, w_ref, b_ref, o_ref):
    c = c_ref[...]
    a = (c * jax.nn.sigmoid(c)).astype(BF16)
    o_ref[...] = jnp.dot(a, w_ref[...].astype(BF16), preferred_element_type=F32) + b_ref[...]


def _ada(c, w, b):
    bc, d = c.shape
    n = w.shape[1]
    tn = _tile(n, 1024, LANES)
    return pl.pallas_call(
        _ada_kernel,
        out_shape=jax.ShapeDtypeStruct((bc, n), F32),
        grid=(n // tn,),
        in_specs=[pl.BlockSpec((bc, d), lambda j: (0, 0)),
                  pl.BlockSpec((d, tn), lambda j: (0, j)),
                  pl.BlockSpec((1, tn), lambda j: (0, j))],
        out_specs=pl.BlockSpec((bc, tn), lambda j: (0, j)),
        compiler_params=_params("arbitrary"),
        name="ada",
    )(c, w, b.reshape(1, n))


def _inproj_kernel(x_ref, sc_ref, sh_ref, nw_ref, wm_ref, ws_ref, om_ref, os_ref, h_ref):
    @pl.when(pl.program_id(1) == 0)
    def _():
        x = x_ref[...]
        y = x * lax.rsqrt(jnp.mean(x * x, axis=-1, keepdims=True) + NORM_EPS) * nw_ref[...]
        h = y * (1.0 + sc_ref[...]) + sh_ref[...]
        hb = h.reshape(h_ref.shape).astype(BF16)
        h_ref[...] = hb
        os_ref[...] = jnp.dot(hb, ws_ref[...], preferred_element_type=F32)

    om_ref[...] = jnp.dot(h_ref[...], wm_ref[...], preferred_element_type=F32)


def _inproj(x, sc, sh, nw, wm, ws):
    b, t, d = x.shape
    n = b * t
    m = wm.shape[1]
    tm = _tile(n, 512, 8)
    tn = _tile(m, 1536, LANES)
    gb, r, nt = _row_groups(b, t, tm)
    return pl.pallas_call(
        _inproj_kernel,
        out_shape=(jax.ShapeDtypeStruct((n, m), F32), jax.ShapeDtypeStruct((n, LANES), F32)),
        grid=(n // tm, m // tn),
        in_specs=[pl.BlockSpec((gb, r, d), lambda i, j: (i // nt, i % nt, 0)),
                  pl.BlockSpec((gb, 1, d), lambda i, j: (i // nt, 0, 0)),
                  pl.BlockSpec((gb, 1, d), lambda i, j: (i // nt, 0, 0)),
                  pl.BlockSpec((1, 1, d), lambda i, j: (0, 0, 0)),
                  pl.BlockSpec((d, tn), lambda i, j: (0, j)),
                  pl.BlockSpec((d, LANES), lambda i, j: (0, 0))],
        out_specs=(pl.BlockSpec((tm, tn), lambda i, j: (i, j)),
                   pl.BlockSpec((tm, LANES), lambda i, j: (i, 0))),
        scratch_shapes=[pltpu.VMEM((tm, d), BF16)],
        compiler_params=_params("arbitrary", "arbitrary"),
        name="inproj",
    )(x, sc, sh, nw, wm, ws)


def _causal_conv(x_ref, xs_ref, cw_ref):
    t = x_ref.shape[1]
    xs_ref[:, CONV_PAD:CONV_PAD + t, :] = x_ref[...]
    cw = cw_ref[...]
    y = None
    for k in range(CONV_W):
        lo = CONV_PAD - (CONV_W - 1) + k
        term = xs_ref[:, lo:lo + t, :] * cw[k:k + 1, :][None]
        y = term if y is None else y + term
    xs_ref[:, CONV_PAD - (CONV_W - 1):CONV_PAD, :] = xs_ref[:, t + CONV_PAD - (CONV_W - 1):t + CONV_PAD, :]
    return y


def _lru_kernel(xl_ref, yl_ref, cs_ref, h0_ref, cw_ref, cb_ref, wg_ref, ba_ref, bx_ref, lam_ref,
                o_ref, hn_ref, xs_ref, a_ref, u_ref, h_ref, *, reset_first):
    ti = pl.program_id(1)
    bb, tc, lw = xl_ref.shape
    nheads = lw // LANES

    @pl.when(ti == 0)
    def _():
        xs_ref[:, CONV_PAD - (CONV_W - 1):CONV_PAD, :] = cs_ref[...]
        h_ref[...] = h0_ref[...]

    xc = (_causal_conv(xl_ref, xs_ref, cw_ref) + cb_ref[...][None]).reshape(bb * tc, lw)
    sp = _softplus(-lam_ref[...])
    t_glob = lax.broadcasted_iota(jnp.int32, (bb, tc, LANES), 1) + ti * tc
    for hd in range(nheads):
        sl = slice(hd * LANES, (hd + 1) * LANES)
        xh = xc[:, sl]
        gates = jnp.dot(xh.astype(BF16), wg_ref[hd], preferred_element_type=F32)
        r = jax.nn.sigmoid(gates[:, :LANES] + ba_ref[:, sl])
        i = jax.nn.sigmoid(gates[:, LANES:] + bx_ref[:, sl])
        log_a = -LRU_C * r * sp[:, sl]
        a = jnp.exp(log_a)
        mult = jnp.sqrt(jnp.maximum(-jnp.tanh(log_a) * (a * a + 1.0), 0.0))
        u = i * xh
        a3 = a.reshape(bb, tc, LANES)
        m3 = mult.reshape(bb, tc, LANES)
        if reset_first:
            m3 = jnp.where(t_glob == 0, 1.0, m3)
        a_ref[:, :, sl] = a3
        u_ref[:, :, sl] = m3 * u.reshape(bb, tc, LANES)

    def step(t, h):
        h = a_ref[:, pl.ds(t, 1), :] * h + u_ref[:, pl.ds(t, 1), :]
        u_ref[:, pl.ds(t, 1), :] = h
        return h

    h = lax.fori_loop(0, tc, step, h_ref[...], unroll=8)
    h_ref[...] = h
    hn_ref[...] = h
    o_ref[...] = u_ref[...] * jax.nn.gelu(yl_ref[...])


def _lru(pm3, conv_state, h0, cw, cb, wg, ba, bx, lam, *, reset_first, bb, tc):
    b, t, _ = pm3.shape
    lw = lam.shape[-1]
    assert b % bb == 0 and t % tc == 0 and tc % 8 == 0
    row = lambda v: v.reshape(1, lw)
    kern = functools.partial(_lru_kernel, reset_first=reset_first)
    out, hn = pl.pallas_call(
        kern,
        out_shape=(jax.ShapeDtypeStruct((b, t, lw), F32), jax.ShapeDtypeStruct((b, 1, lw), F32)),
        grid=(b // bb, t // tc),
        in_specs=[pl.BlockSpec((bb, tc, lw), lambda i, j: (i, j, 0)),
                  pl.BlockSpec((bb, tc, lw), lambda i, j: (i, j, 1)),
                  pl.BlockSpec((bb, CONV_W - 1, lw), lambda i, j: (i, 0, 0)),
                  pl.BlockSpec((bb, 1, lw), lambda i, j: (i, 0, 0)),
                  pl.BlockSpec((CONV_W, lw), lambda i, j: (0, 0)),
                  pl.BlockSpec((1, lw), lambda i, j: (0, 0)),
                  pl.BlockSpec(wg.shape, lambda i, j: (0, 0, 0)),
                  pl.BlockSpec((1, lw), lambda i, j: (0, 0)),
                  pl.BlockSpec((1, lw), lambda i, j: (0, 0)),
                  pl.BlockSpec((1, lw), lambda i, j: (0, 0))],
        out_specs=(pl.BlockSpec((bb, tc, lw), lambda i, j: (i, j, 0)),
                   pl.BlockSpec((bb, 1, lw), lambda i, j: (i, 0, 0))),
        scratch_shapes=[pltpu.VMEM((bb, tc + CONV_PAD, lw), F32),
                        pltpu.VMEM((bb, tc, lw), F32),
                        pltpu.VMEM((bb, tc, lw), F32),
                        pltpu.VMEM((bb, 1, lw), F32)],
        compiler_params=_params("arbitrary", "arbitrary"),
        name="lru",
    )(pm3, pm3, conv_state, h0.reshape(b, 1, lw), cw, row(cb), wg, row(ba), row(bx), row(lam))
    return out, hn.reshape(b, lw)


def _delta_kernel(q_ref, k_ref, v_ref, z_ref, ps_ref, csq_ref, csk_ref, csv_ref, s0_ref,
                  cwq_ref, cwk_ref, cwv_ref, ea_ref, dtb_ref, nw_ref,
                  o_ref, sn_ref,
                  xq_ref, xk_ref, xv_ref, qn_ref, kn_ref, vn_ref, beta_ref, g_ref, *, chunk, nheads):
    ti = pl.program_id(1)
    bb, tb, w = q_ref.shape
    n_chunks = tb // chunk
    rows = bb * chunk
    assert bb == 1 or n_chunks == 1
    lo = CONV_PAD - (CONV_W - 1)

    @pl.when(ti == 0)
    def _():
        xq_ref[:, lo:CONV_PAD, :] = csq_ref[...]
        xk_ref[:, lo:CONV_PAD, :] = csk_ref[...]
        xv_ref[:, lo:CONV_PAD, :] = csv_ref[...]
        sn_ref[...] = s0_ref[...]

    def conv_silu(x_ref, xs_ref, cw_ref):
        y = _causal_conv(x_ref, xs_ref, cw_ref)
        return (y * jax.nn.sigmoid(y)).reshape(bb * tb, w)

    qc = conv_silu(q_ref, xq_ref, cwq_ref)
    kc = conv_silu(k_ref, xk_ref, cwk_ref)
    vn_ref[...] = conv_silu(v_ref, xv_ref, cwv_ref)
    for hd in range(nheads):
        sl = slice(hd * LANES, (hd + 1) * LANES)
        qh = qc[:, sl]
        kh = kc[:, sl]
        qn_ref[:, sl] = qh * (lax.rsqrt(jnp.sum(qh * qh, axis=-1, keepdims=True) + NORM_EPS) * (LANES ** -0.5))
        kn_ref[:, sl] = kh * lax.rsqrt(jnp.sum(kh * kh, axis=-1, keepdims=True) + NORM_EPS)
    ps = ps_ref[...].reshape(bb * tb, LANES)
    beta_ref[...] = jax.nn.sigmoid(ps)
    g_ref[...] = -ea_ref[...] * _softplus(ps + dtb_ref[...])

    ii = lax.broadcasted_iota(jnp.int32, (rows, rows), 0)
    jj = lax.broadcasted_iota(jnp.int32, (rows, rows), 1)
    if bb == 1:
        le = jj <= ii
        lt = jj < ii
    else:
        same = (ii // chunk) == (jj // chunk)
        le = jnp.logical_and(same, jj <= ii)
        lt = jnp.logical_and(same, jj < ii)
    le_b = jnp.where(le, 1.0, 0.0).astype(BF16)
    assert chunk & (chunk - 1) == 0 and chunk >= 2
    levels = []
    b = 1
    while b < chunk:
        sh = b.bit_length()
        join = jnp.logical_and((ii >> sh) == (jj >> sh), (ii >> (sh - 1)) != (jj >> (sh - 1)))
        levels.append(jnp.logical_and(join, lt))
        b *= 2
    heads = range(nheads)
    lanes = [slice(hd * LANES, (hd + 1) * LANES) for hd in heads]

    def mm(a, b):
        return jnp.dot(a.astype(BF16), b.astype(BF16), preferred_element_type=F32)

    def do_chunk(c, carry):
        r0 = pl.multiple_of(c * rows, rows)
        rsl = pl.ds(r0, rows)
        g = g_ref[rsl, :]
        beta = beta_ref[rsl, :]
        g_hi = g.astype(BF16)
        g_lo = (g - g_hi.astype(F32)).astype(BF16)
        gc = (jnp.dot(le_b, g_hi, preferred_element_type=F32)
              + jnp.dot(le_b, g_lo, preferred_element_type=F32))
        gcp = gc if rows == LANES else jnp.concatenate([gc, jnp.zeros((LANES - rows, LANES), F32)], axis=0)
        gct = gcp.T
        eg = jnp.exp(gc)
        q = [qn_ref[rsl, sl] for sl in lanes]
        k = [kn_ref[rsl, sl] for sl in lanes]
        v = [vn_ref[rsl, sl] for sl in lanes]
        bcol = [beta[:, hd:hd + 1] for hd in heads]
        gcol = [gc[:, nheads + hd:nheads + hd + 1] for hd in heads]
        egc = [eg[:, nheads + hd:nheads + hd + 1] for hd in heads]
        decay = [jnp.where(le, jnp.exp(jnp.where(le, gcol[hd] - gct[nheads + hd:nheads + hd + 1, :rows], 0.0)), 0.0)
                 for hd in heads]
        kb = [k[hd] * bcol[hd] for hd in heads]
        k16 = [k[hd].astype(BF16) for hd in heads]
        kk = [lax.dot_general(kb[hd].astype(BF16), k16[hd], NT_DIMS, preferred_element_type=F32) for hd in heads]
        qk = [lax.dot_general(q[hd].astype(BF16), k16[hd], NT_DIMS, preferred_element_type=F32) for hd in heads]
        lmat = [jnp.where(lt, kk[hd] * decay[hd], 0.0) for hd in heads]
        attn = [jnp.where(le, qk[hd] * decay[hd], 0.0).astype(BF16) for hd in heads]
        nmat = [jnp.where(levels[0], -lmat[hd], 0.0) for hd in heads]
        for lvl in levels[1:]:
            off = [jnp.where(lvl, lmat[hd], 0.0) for hd in heads]
            m = [off[hd] + mm(nmat[hd], off[hd]) for hd in heads]
            nmat = [nmat[hd] - m[hd] - mm(m[hd], nmat[hd]) for hd in heads]
        rhs = [jnp.concatenate([v[hd] * bcol[hd], kb[hd] * egc[hd]], axis=1) for hd in heads]
        sol = [rhs[hd] + mm(nmat[hd], rhs[hd]) for hd in heads]
        qe = [q[hd] * egc[hd] for hd in heads]
        s_old = [[sn_ref[b, hd] for b in range(bb)] for hd in heads]
        both = [[jnp.dot(jnp.concatenate([sol[hd][b * chunk:(b + 1) * chunk, LANES:],
                                          qe[hd][b * chunk:(b + 1) * chunk]], axis=0).astype(BF16),
                         s_old[hd][b].astype(BF16), preferred_element_type=F32) for b in range(bb)] for hd in heads]
        cat = lambda parts: parts[0] if bb == 1 else jnp.concatenate(parts, axis=0)
        v16 = [(sol[hd][:, :LANES] - cat([both[hd][b][:chunk] for b in range(bb)])).astype(BF16) for hd in heads]
        o = [cat([both[hd][b][chunk:] for b in range(bb)]) + jnp.dot(attn[hd], v16[hd], preferred_element_type=F32)
             for hd in heads]
        for hd in heads:
            for b in range(bb):
                seg = slice(b * chunk, (b + 1) * chunk)
                g_last = gcol[hd][(b + 1) * chunk - 1:(b + 1) * chunk]
                kd = (k[hd][seg] * jnp.exp(g_last - gcol[hd][seg])).astype(BF16)
                sn_ref[b, hd] = s_old[hd][b] * jnp.exp(g_last) + lax.dot_general(
                    kd, v16[hd][seg], TN_DIMS, preferred_element_type=F32)
        for hd in heads:
            sl = lanes[hd]
            on = o[hd] * lax.rsqrt(jnp.mean(o[hd] * o[hd], axis=-1, keepdims=True) + NORM_EPS) * nw_ref[...]
            if bb == 1:
                zz = z_ref[0, rsl, sl]
                o_ref[0, rsl, sl] = on * (zz * jax.nn.sigmoid(zz))
            else:
                zz = z_ref[:, :, sl].reshape(rows, LANES)
                o_ref[:, :, sl] = (on * (zz * jax.nn.sigmoid(zz))).reshape(bb, chunk, LANES)
        return carry

    if n_chunks == 1:
        do_chunk(0, 0)
    else:
        lax.fori_loop(0, n_chunks, do_chunk, 0)


def _delta(pm3, ps3, conv_state, s0, cw, a_log, dt_bias, norm_w, *, bb, tb, chunk):
    b, t, _ = pm3.shape
    nheads = a_log.shape[-1]
    w = nheads * LANES
    assert s0.shape == (b, nheads, LANES, LANES) and nheads <= 8
    assert b % bb == 0 and t % tb == 0 and tb % chunk == 0 and (bb * chunk) % 8 == 0 and bb * chunk <= LANES
    pad = lambda v: jnp.zeros((1, LANES), F32).at[0, nheads:2 * nheads].set(v)
    kern = functools.partial(_delta_kernel, chunk=chunk, nheads=nheads)
    col = lambda c: pl.BlockSpec((bb, tb, w), lambda i, j, c=c: (i, j, c))
    cst = lambda c: pl.BlockSpec((bb, CONV_W - 1, w), lambda i, j, c=c: (i, 0, c))
    cwt = lambda c: pl.BlockSpec((CONV_W, w), lambda i, j, c=c: (0, c))
    vec = pl.BlockSpec((1, LANES), lambda i, j: (0, 0))
    sblk = pl.BlockSpec((bb, nheads, LANES, LANES), lambda i, j: (i, 0, 0, 0))
    out, sn = pl.pallas_call(
        kern,
        out_shape=(jax.ShapeDtypeStruct((b, t, w), F32), jax.ShapeDtypeStruct(s0.shape, F32)),
        grid=(b // bb, t // tb),
        in_specs=[col(2), col(3), col(4), col(5),
                  pl.BlockSpec((bb, tb, LANES), lambda i, j: (i, j, 0)),
                  cst(0), cst(1), cst(2), sblk, cwt(0), cwt(1), cwt(2), vec, vec, vec],
        out_specs=(pl.BlockSpec((bb, tb, w), lambda i, j: (i, j, 0)), sblk),
        scratch_shapes=[pltpu.VMEM((bb, tb + CONV_PAD, w), F32)] * 3
                       + [pltpu.VMEM((bb * tb, w), F32)] * 3
                       + [pltpu.VMEM((bb * tb, LANES), F32)] * 2,
        compiler_params=_params("arbitrary", "arbitrary"),
        name="delta",
    )(pm3, pm3, pm3, pm3, ps3, conv_state, conv_state, conv_state, s0, cw, cw, cw,
      pad(jnp.exp(a_log.astype(F32))), pad(dt_bias.astype(F32)), norm_w.reshape(1, LANES))
    return out, sn


def _outproj_kernel(ol_ref, od_ref, x_ref, g1_ref, sc_ref, sh_ref, nw_ref, wt_ref, wb_ref, x1_ref, h2_ref):
    tm = h2_ref.shape[0]
    a = ol_ref[...].reshape(tm, ol_ref.shape[-1]).astype(BF16)
    b = od_ref[...].reshape(tm, od_ref.shape[-1]).astype(BF16)
    mix = (jnp.dot(a, wt_ref[...], preferred_element_type=F32)
           + jnp.dot(b, wb_ref[...], preferred_element_type=F32))
    x1 = x_ref[...] + g1_ref[...] * mix.reshape(x_ref.shape)
    x1_ref[...] = x1
    y = x1 * lax.rsqrt(jnp.mean(x1 * x1, axis=-1, keepdims=True) + NORM_EPS) * nw_ref[...]
    h2_ref[...] = (y * (1.0 + sc_ref[...]) + sh_ref[...]).reshape(h2_ref.shape).astype(BF16)


def _outproj(ol, od, x, g1, sc, sh, nw, wt, wb):
    b, t, d = x.shape
    n = b * t
    lw, dw = ol.shape[-1], od.shape[-1]
    tm = _tile(n, 256, 16)
    gb, r, nt = _row_groups(b, t, tm)
    tok = lambda c: pl.BlockSpec((gb, r, c), lambda i: (i // nt, i % nt, 0))
    per = pl.BlockSpec((gb, 1, d), lambda i: (i // nt, 0, 0))
    return pl.pallas_call(
        _outproj_kernel,
        out_shape=(jax.ShapeDtypeStruct((b, t, d), F32), jax.ShapeDtypeStruct((n, d), BF16)),
        grid=(n // tm,),
        in_specs=[tok(lw), tok(dw), tok(d), per, per, per,
                  pl.BlockSpec((1, 1, d), lambda i: (0, 0, 0)),
                  pl.BlockSpec((lw, d), lambda i: (0, 0)),
                  pl.BlockSpec((dw, d), lambda i: (0, 0))],
        out_specs=(tok(d), pl.BlockSpec((tm, d), lambda i: (i, 0))),
        compiler_params=_params("arbitrary"),
        name="outproj",
    )(ol, od, x, g1, sc, sh, nw, wt, wb)


def _top_values(s, count):
    vals = []
    work = s
    for r in range(count):
        m = jnp.max(work, axis=0, keepdims=True)
        vals.append(m)
        if r + 1 < count:
            work = jnp.where(work == m, -jnp.inf, work)
    return vals


def _router_kernel(h_ref, wq_ref, sk_ref, s2_ref, e2_ref, thr_ref, e1_ref, q_ref, *, nheads):
    nk = sk_ref.shape[1]
    half = sk_ref.shape[2]
    q_ref[...] = lax.dot_general(wq_ref[...], h_ref[...], NT_DIMS, preferred_element_type=F32)
    p = h_ref.shape[0]
    sub = lax.broadcasted_iota(jnp.int32, (8, p), 0)

    def head(hd, carry):
        base = pl.multiple_of(hd * (2 * half), 2 * half)
        q1 = q_ref[pl.ds(base, half), :].astype(BF16)
        q2 = q_ref[pl.ds(base + half, half), :].astype(BF16)
        s1 = jnp.dot(sk_ref[2 * hd], q1, preferred_element_type=F32)
        s2 = jnp.dot(sk_ref[2 * hd + 1], q2, preferred_element_type=F32)
        nv = PEER_TOPK + 1
        t1 = _top_values(s1, nv)
        t2 = _top_values(s2, nv)
        t2_lo = jnp.concatenate(t2[:8], axis=0)
        t2_hi = jnp.concatenate(t2[8:16], axis=0)
        pieces = [t1[0] + t2_lo, t1[0] + t2_hi, jnp.where(sub < 1, t1[0] + t2[16], -jnp.inf)]
        for a in range(1, nv):
            nb = nv // (a + 1)
            c = t1[a] + t2_lo
            pieces.append(c if nb >= 8 else jnp.where(sub < nb, c, -jnp.inf))
        cand = jnp.concatenate(pieces, axis=0)
        best = _top_values(cand, nv)
        tau = best[PEER_TOPK - 1]
        mid = 0.5 * (tau + best[PEER_TOPK])
        top = t1[0] + t2[0]
        z = jnp.sum(jnp.where(cand >= tau, jnp.exp(cand - top), 0.0), axis=0, keepdims=True)
        s2_ref[hd] = s2
        e2_ref[hd] = jnp.exp(s2 - t2[0]) / z
        thr_ref[hd] = mid - s1
        e1_ref[hd] = jnp.exp(s1 - t1[0])
        return carry

    lax.fori_loop(0, nheads, head, 0)


def _router(h2, wq_t, sk, *, tp):
    n, d = h2.shape
    nheads = sk.shape[0] // 2
    nk = sk.shape[1]
    hq = wq_t.shape[0]
    out = jax.ShapeDtypeStruct((nheads, nk, n), F32)
    oblk = pl.BlockSpec((nheads, nk, tp), lambda i: (0, 0, i))
    return pl.pallas_call(
        functools.partial(_router_kernel, nheads=nheads),
        out_shape=(out, out, out, out),
        grid=(n // tp,),
        in_specs=[pl.BlockSpec((tp, d), lambda i: (i, 0)),
                  pl.BlockSpec((hq, d), lambda i: (0, 0)),
                  pl.BlockSpec(sk.shape, lambda i: (0, 0, 0))],
        out_specs=(oblk, oblk, oblk, oblk),
        scratch_shapes=[pltpu.VMEM((hq, tp), F32)],
        compiler_params=_params("arbitrary"),
        name="router",
    )(h2, wq_t, sk)


def _dense_kernel(h_ref, u_ref, vt_ref, s2_ref, e2_ref, thr_ref, e1_ref, x1_ref, g2_ref, fw_ref,
                  y_ref, acc_ref, st_ref, coef_ref, *, nheads, rows_per_step, n_chunks):
    s = pl.program_id(1)
    nk = s2_ref.shape[1]
    tp = h_ref.shape[0]

    @pl.when(s == 0)
    def _():
        acc_ref[...] = jnp.zeros_like(acc_ref)
        st_ref[...] = jnp.zeros_like(st_ref)
        coef_ref[...] = jnp.zeros_like(coef_ref)

    cur = s % 2
    prev = 1 - cur
    acc_ref[...] += jnp.dot(vt_ref[...], coef_ref[cur], preferred_element_type=F32)
    chunk = jnp.clip(s - 1, 0, n_chunks - 1)
    for ii in range(rows_per_step):
        i1 = chunk * rows_per_step + ii
        rows = slice(ii * nk, (ii + 1) * nk)
        thr_rows = [thr_ref[hd, pl.ds(i1, 1), :] for hd in range(nheads)]
        e1_rows = [e1_ref[hd, pl.ds(i1, 1), :] for hd in range(nheads)]
        for pb in range(tp // LANES):
            cols = slice(pb * LANES, (pb + 1) * LANES)
            gate = None
            for hd in range(nheads):
                term = jnp.where(s2_ref[hd, :, cols] >= thr_rows[hd][:, cols],
                                 e2_ref[hd, :, cols] * e1_rows[hd][:, cols], 0.0)
                gate = term if gate is None else gate + term
            coef_ref[prev, rows, cols] = (jax.nn.gelu(st_ref[prev, rows, cols]) * gate).astype(BF16)
    st_ref[cur] = lax.dot_general(u_ref[...], h_ref[...], NT_DIMS, preferred_element_type=F32)

    @pl.when(s == pl.num_programs(1) - 1)
    def _():
        peer = acc_ref[...].T.reshape(x1_ref.shape)
        x2 = x1_ref[...] + g2_ref[...] * peer
        y_ref[...] = x2 * lax.rsqrt(jnp.mean(x2 * x2, axis=-1, keepdims=True) + NORM_EPS) * fw_ref[...]


def _dense(h2, u16, vt16, s2t, e2t, thr, e1, x1, g2, fw, *, tp, rows_per_step):
    b, t, d = x1.shape
    n = b * t
    nheads, nk, _ = s2t.shape
    ne = u16.shape[0]
    ce = rows_per_step * nk
    n_chunks = ne // ce
    gb, r, nt = _row_groups(b, t, tp)
    rblk = pl.BlockSpec((nheads, nk, tp), lambda i, s: (0, 0, i))
    tok = pl.BlockSpec((gb, r, d), lambda i, s: (i // nt, i % nt, 0))
    return pl.pallas_call(
        functools.partial(_dense_kernel, nheads=nheads, rows_per_step=rows_per_step, n_chunks=n_chunks),
        out_shape=jax.ShapeDtypeStruct((b, t, d), F32),
        grid=(n // tp, n_chunks + 2),
        in_specs=[pl.BlockSpec((tp, d), lambda i, s: (i, 0)),
                  pl.BlockSpec((ce, d), lambda i, s: (jnp.minimum(s, n_chunks - 1), 0)),
                  pl.BlockSpec((d, ce), lambda i, s: (0, jnp.clip(s - 2, 0, n_chunks - 1))),
                  rblk, rblk, rblk, rblk, tok,
                  pl.BlockSpec((gb, 1, d), lambda i, s: (i // nt, 0, 0)),
                  pl.BlockSpec((1, 1, d), lambda i, s: (0, 0, 0))],
        out_specs=tok,
        scratch_shapes=[pltpu.VMEM((d, tp), F32), pltpu.VMEM((2, ce, tp), F32), pltpu.VMEM((2, ce, tp), BF16)],
        compiler_params=_params("arbitrary", "arbitrary"),
        name="dense",
    )(h2, u16, vt16, s2t, e2t, thr, e1, x1, g2, fw)


def _group(x, mod, lru_conv, lru_h, dn_conv, dn_s, reset_first, p, fw, *, lru_tiles, dn_tiles, tp):
    b, t, d = x.shape
    sh1, sc1, g1, sh2, sc2, g2 = [m.reshape(b, 1, d) for m in jnp.split(mod, 6, axis=-1)]
    pm, ps = _inproj(x, sc1, sh1, p['norm_mix_w'], p['w_main'], p['w_small'])
    pm3 = pm.reshape(b, t, -1)
    ps3 = ps.reshape(b, t, LANES)
    lw = p['lru_lambda'].shape[-1]
    out_lru, new_h = _lru(pm3, lru_conv, lru_h, p['lru_conv_w'], p['lru_conv_b'], p['lru_wg'], p['lru_ba'],
                          p['lru_bx'], p['lru_lambda'], reset_first=reset_first, bb=lru_tiles[0], tc=lru_tiles[1])
    out_dn, new_s = _delta(pm3, ps3, dn_conv, dn_s, p['dn_conv_w'], p['dn_A_log'], p['dn_dt_bias'], p['dn_norm_w'],
                           bb=dn_tiles[0], tb=dn_tiles[1], chunk=dn_tiles[2])
    keep = CONV_W - 1
    new_lru_conv = pm3[:, t - keep:, :lw]
    new_dn_conv = pm3[:, t - keep:, 2 * lw:2 * lw + dn_conv.shape[-1]]
    x1, h2 = _outproj(out_lru, out_dn, x, g1, sc2, sh2, p['norm_ffn_w'], p['w_out_top'], p['w_out_bot'])
    s2t, e2t, thr, e1 = _router(h2, p['wq_t'], p['subkeys'], tp=tp)
    y = _dense(h2, p['peer_u'], p['peer_vt'], s2t, e2t, thr, e1, x1, g2, fw, tp=tp, rows_per_step=4)
    return y, (new_lru_conv, new_h, new_dn_conv, new_s)


def kernel(x_prompt, x_sample, state_lru_conv, state_lru_h, state_dn_conv, state_dn_S, c_prompt, c_sample,
           w_ada, b_ada, norm_mix_w, norm_ffn_w, w_in, lru_conv_w, lru_conv_b, lru_wa, lru_ba, lru_wx, lru_bx,
           lru_lambda, dn_conv_w, dn_A_log, dn_dt_bias, dn_norm_w, w_out, peer_wq, peer_subkeys, peer_u, peer_v,
           final_norm_w):
    depth = w_ada.shape[0]
    assert depth == 1, "the final norm is fused into the last layer's expert kernel"
    bp, seq, d = x_prompt.shape
    bs, dseq, _ = x_sample.shape
    lw = lru_lambda.shape[-1]
    dn_heads = dn_A_log.shape[-1]
    dn_w = dn_heads * LANES
    conv_ch = dn_conv_w.shape[-1]
    assert lru_wa.shape[-1] == LANES and lw == dn_w and conv_ch == 3 * dn_w
    assert peer_subkeys.shape[-1] == LANES and peer_subkeys.shape[-2] == LANES
    main = 2 * lw + conv_ch + dn_w
    l = 0
    w_small = jnp.zeros((d, LANES), F32).at[:, :2 * dn_heads].set(w_in[l][:, main:]).astype(BF16)
    nheads = peer_subkeys.shape[1]
    p = {
        'norm_mix_w': norm_mix_w[l].reshape(1, 1, d), 'norm_ffn_w': norm_ffn_w[l].reshape(1, 1, d),
        'w_main': w_in[l][:, :main].astype(BF16), 'w_small': w_small,
        'lru_conv_w': lru_conv_w[l], 'lru_conv_b': lru_conv_b[l],
        'lru_wg': jnp.concatenate([lru_wa[l], lru_wx[l]], axis=-1).astype(BF16),
        'lru_ba': lru_ba[l], 'lru_bx': lru_bx[l], 'lru_lambda': lru_lambda[l],
        'dn_conv_w': dn_conv_w[l], 'dn_A_log': dn_A_log[l], 'dn_dt_bias': dn_dt_bias[l], 'dn_norm_w': dn_norm_w[l],
        'w_out_top': w_out[l][:lw].astype(BF16), 'w_out_bot': w_out[l][lw:].astype(BF16),
        'wq_t': peer_wq[l].T.astype(BF16),
        'subkeys': peer_subkeys[l].reshape(nheads * 2, LANES, LANES).astype(BF16),
        'peer_u': peer_u[l].astype(BF16), 'peer_vt': peer_v[l].T.astype(BF16),
    }
    fw = final_norm_w.reshape(1, 1, d)
    mod = _ada(jnp.concatenate([c_prompt, c_sample], axis=0), w_ada[l], b_ada[l])
    zeros = lambda *s: jnp.zeros(s, F32)
    yp, sp = _group(x_prompt, mod[:bp], zeros(bp, CONV_W - 1, lw), zeros(bp, lw), zeros(bp, CONV_W - 1, conv_ch),
                    zeros(bp, dn_heads, LANES, LANES), True, p, fw,
                    lru_tiles=(bp, _tile(seq, 128, 8)), dn_tiles=(1, _tile(seq, 256, DN_CHUNK), min(DN_CHUNK, seq)),
                    tp=_tile(bp * seq, 512, LANES))
    ys, ss = _group(x_sample, mod[bp:], state_lru_conv[l], state_lru_h[l], state_dn_conv[l], state_dn_S[l],
                    False, p, fw,
                    lru_tiles=(_tile(bs, 32, 1), dseq), dn_tiles=(_tile(bs, DN_CHUNK // dseq, 1), dseq, dseq),
                    tp=_tile(bs * dseq, 512, LANES))
    stack = lambda v: v[None]
    return (yp, ys, stack(sp[0]), stack(sp[1]), stack(sp[2]), stack(sp[3]),
            stack(ss[0]), stack(ss[1]), stack(ss[2]), stack(ss[3]))
```

```python
import functools
import math

import jax
import jax.numpy as jnp
from jax import lax
from jax.experimental import pallas as pl
from jax.experimental.pallas import tpu as pltpu

F32 = jnp.float32
BF16 = jnp.bfloat16
NORM_EPS = 1e-6
LRU_C = 8.0
CONV_W = 4
CONV_PAD = 8
PEER_TOPK = 16
DN_CHUNK = 64
LANES = 128
VMEM_LIMIT_BYTES = 60 * 1024 * 1024
NT_DIMS = (((1,), (1,)), ((), ()))
TN_DIMS = (((0,), (0,)), ((), ()))


def _params(*semantics):
    return pltpu.CompilerParams(dimension_semantics=semantics, vmem_limit_bytes=VMEM_LIMIT_BYTES)


def _tile(n, target, multiple):
    if n <= target:
        return n
    t = (target // multiple) * multiple
    while t >= multiple:
        if n % t == 0:
            return t
        t -= multiple
    raise ValueError(f"no tile for {n} <= {target} in multiples of {multiple}")


def _softplus(x):
    return jnp.maximum(x, 0.0) + jnp.log1p(jnp.exp(-jnp.abs(x)))


def _row_groups(batch, seq, tm):
    r = min(seq, tm)
    assert seq % r == 0 and tm % r == 0 and (batch * seq) % tm == 0
    return tm // r, r, seq // r


def _ada_kernel(c_ref, w_ref, b_ref, o_ref):
    c = c_ref[...]
    a = (c * jax.nn.sigmoid(c)).astype(BF16)
    o_ref[...] = jnp.dot(a, w_ref[...].astype(BF16), preferred_element_type=F32) + b_ref[...]


def _ada(c, w, b):
    bc, d = c.shape
    n = w.shape[1]
    tn = _tile(n, 1024, LANES)
    return pl.pallas_call(
        _ada_kernel,
        out_shape=jax.ShapeDtypeStruct((bc, n), F32),
        grid=(n // tn,),
        in_specs=[pl.BlockSpec((bc, d), lambda j: (0, 0)),
                  pl.BlockSpec((d, tn), lambda j: (0, j)),
                  pl.BlockSpec((1, tn), lambda j: (0, j))],
        out_specs=pl.BlockSpec((bc, tn), lambda j: (0, j)),
        compiler_params=_params("arbitrary"),
        name="ada",
    )(c, w, b.reshape(1, n))


def _inproj_kernel(x_ref, sc_ref, sh_ref, nw_ref, wm_ref, ws_ref, om_ref, os_ref, h_ref):
    @pl.when(pl.program_id(1) == 0)
    def _():
        x = x_ref[...]
        y = x * lax.rsqrt(jnp.mean(x * x, axis=-1, keepdims=True) + NORM_EPS) * nw_ref[...]
        h = y * (1.0 + sc_ref[...]) + sh_ref[...]
        hb = h.reshape(h_ref.shape).astype(BF16)
        h_ref[...] = hb
        os_ref[...] = jnp.dot(hb, ws_ref[...], preferred_element_type=F32)

    om_ref[...] = jnp.dot(h_ref[...], wm_ref[...], preferred_element_type=F32)


def _inproj(x, sc, sh, nw, wm, ws):
    b, t, d = x.shape
    n = b * t
    m = wm.shape[1]
    tm = _tile(n, 512, 8)
    tn = _tile(m, 1536, LANES)
    gb, r, nt = _row_groups(b, t, tm)
    return pl.pallas_call(
        _inproj_kernel,
        out_shape=(jax.ShapeDtypeStruct((n, m), F32), jax.ShapeDtypeStruct((n, LANES), F32)),
        grid=(n // tm, m // tn),
        in_specs=[pl.BlockSpec((gb, r, d), lambda i, j: (i // nt, i % nt, 0)),
                  pl.BlockSpec((gb, 1, d), lambda i, j: (i // nt, 0, 0)),
                  pl.BlockSpec((gb, 1, d), lambda i, j: (i // nt, 0, 0)),
                  pl.BlockSpec((1, 1, d), lambda i, j: (0, 0, 0)),
                  pl.BlockSpec((d, tn), lambda i, j: (0, j)),
                  pl.BlockSpec((d, LANES), lambda i, j: (0, 0))],
        out_specs=(pl.BlockSpec((tm, tn), lambda i, j: (i, j)),
                   pl.BlockSpec((tm, LANES), lambda i, j: (i, 0))),
        scratch_shapes=[pltpu.VMEM((tm, d), BF16)],
        compiler_params=_params("arbitrary", "arbitrary"),
        name="inproj",
    )(x, sc, sh, nw, wm, ws)


def _causal_conv(x_ref, xs_ref, cw_ref):
    t = x_ref.shape[1]
    xs_ref[:, CONV_PAD:CONV_PAD + t, :] = x_ref[...]
    cw = cw_ref[...]
    y = None
    for k in range(CONV_W):
        lo = CONV_PAD - (CONV_W - 1) + k
        term = xs_ref[:, lo:lo + t, :] * cw[k:k + 1, :][None]
        y = term if y is None else y + term
    xs_ref[:, CONV_PAD - (CONV_W - 1):CONV_PAD, :] = xs_ref[:, t + CONV_PAD - (CONV_W - 1):t + CONV_PAD, :]
    return y


def _lru_kernel(xl_ref, yl_ref, cs_ref, h0_ref, cw_ref, cb_ref, wg_ref, ba_ref, bx_ref, lam_ref,
                o_ref, hn_ref, xs_ref, a_ref, u_ref, h_ref, *, reset_first):
    ti = pl.program_id(1)
    bb, tc, lw = xl_ref.shape
    nheads = lw // LANES

    @pl.when(ti == 0)
    def _():
        xs_ref[:, CONV_PAD - (CONV_W - 1):CONV_PAD, :] = cs_ref[...]
        h_ref[...] = h0_ref[...]

    xc = (_causal_conv(xl_ref, xs_ref, cw_ref) + cb_ref[...][None]).reshape(bb * tc, lw)
    sp = _softplus(-lam_ref[...])
    t_glob = lax.broadcasted_iota(jnp.int32, (bb, tc, LANES), 1) + ti * tc
    for hd in range(nheads):
        sl = slice(hd * LANES, (hd + 1) * LANES)
        xh = xc[:, sl]
        gates = jnp.dot(xh.astype(BF16), wg_ref[hd], preferred_element_type=F32)
        r = jax.nn.sigmoid(gates[:, :LANES] + ba_ref[:, sl])
        i = jax.nn.sigmoid(gates[:, LANES:] + bx_ref[:, sl])
        log_a = -LRU_C * r * sp[:, sl]
        a = jnp.exp(log_a)
        mult = jnp.sqrt(jnp.maximum(-jnp.tanh(log_a) * (a * a + 1.0), 0.0))
        u = i * xh
        a3 = a.reshape(bb, tc, LANES)
        m3 = mult.reshape(bb, tc, LANES)
        if reset_first:
            m3 = jnp.where(t_glob == 0, 1.0, m3)
        a_ref[:, :, sl] = a3
        u_ref[:, :, sl] = m3 * u.reshape(bb, tc, LANES)

    def step(t, h):
        h = a_ref[:, pl.ds(t, 1), :] * h + u_ref[:, pl.ds(t, 1), :]
        u_ref[:, pl.ds(t, 1), :] = h
        return h

    h = lax.fori_loop(0, tc, step, h_ref[...], unroll=8)
    h_ref[...] = h
    hn_ref[...] = h
    o_ref[...] = u_ref[...] * jax.nn.gelu(yl_ref[...])


def _lru(pm3, conv_state, h0, cw, cb, wg, ba, bx, lam, *, reset_first, bb, tc):
    b, t, _ = pm3.shape
    lw = lam.shape[-1]
    assert b % bb == 0 and t % tc == 0 and tc % 8 == 0
    row = lambda v: v.reshape(1, lw)
    kern = functools.partial(_lru_kernel, reset_first=reset_first)
    out, hn = pl.pallas_call(
        kern,
        out_shape=(jax.ShapeDtypeStruct((b, t, lw), F32), jax.ShapeDtypeStruct((b, 1, lw), F32)),
        grid=(b // bb, t // tc),
        in_specs=[pl.BlockSpec((bb, tc, lw), lambda i, j: (i, j, 0)),
                  pl.BlockSpec((bb, tc, lw), lambda i, j: (i, j, 1)),
                  pl.BlockSpec((bb, CONV_W - 1, lw), lambda i, j: (i, 0, 0)),
                  pl.BlockSpec((bb, 1, lw), lambda i, j: (i, 0, 0)),
                  pl.BlockSpec((CONV_W, lw), lambda i, j: (0, 0)),
                  pl.BlockSpec((1, lw), lambda i, j: (0, 0)),
                  pl.BlockSpec(wg.shape, lambda i, j: (0, 0, 0)),
                  pl.BlockSpec((1, lw), lambda i, j: (0, 0)),
                  pl.BlockSpec((1, lw), lambda i, j: (0, 0)),
                  pl.BlockSpec((1, lw), lambda i, j: (0, 0))],
        out_specs=(pl.BlockSpec((bb, tc, lw), lambda i, j: (i, j, 0)),
                   pl.BlockSpec((bb, 1, lw), lambda i, j: (i, 0, 0))),
        scratch_shapes=[pltpu.VMEM((bb, tc + CONV_PAD, lw), F32),
                        pltpu.VMEM((bb, tc, lw), F32),
                        pltpu.VMEM((bb, tc, lw), F32),
                        pltpu.VMEM((bb, 1, lw), F32)],
        compiler_params=_params("arbitrary", "arbitrary"),
        name="lru",
    )(pm3, pm3, conv_state, h0.reshape(b, 1, lw), cw, row(cb), wg, row(ba), row(bx), row(lam))
    return out, hn.reshape(b, lw)


def _delta_kernel(q_ref, k_ref, v_ref, z_ref, ps_ref, csq_ref, csk_ref, csv_ref, s0_ref,
                  cwq_ref, cwk_ref, cwv_ref, ea_ref, dtb_ref, nw_ref,
                  o_ref, sn_ref,
                  xq_ref, xk_ref, xv_ref, qn_ref, kn_ref, vn_ref, beta_ref, g_ref, *, chunk, nheads):
    ti = pl.program_id(1)
    bb, tb, w = q_ref.shape
    n_chunks = tb // chunk
    rows = bb * chunk
    assert bb == 1 or n_chunks == 1
    lo = CONV_PAD - (CONV_W - 1)

    @pl.when(ti == 0)
    def _():
        xq_ref[:, lo:CONV_PAD, :] = csq_ref[...]
        xk_ref[:, lo:CONV_PAD, :] = csk_ref[...]
        xv_ref[:, lo:CONV_PAD, :] = csv_ref[...]
        sn_ref[...] = s0_ref[...]

    def conv_silu(x_ref, xs_ref, cw_ref):
        y = _causal_conv(x_ref, xs_ref, cw_ref)
        return (y * jax.nn.sigmoid(y)).reshape(bb * tb, w)

    qc = conv_silu(q_ref, xq_ref, cwq_ref)
    kc = conv_silu(k_ref, xk_ref, cwk_ref)
    vn_ref[...] = conv_silu(v_ref, xv_ref, cwv_ref)
    for hd in range(nheads):
        sl = slice(hd * LANES, (hd + 1) * LANES)
        qh = qc[:, sl]
        kh = kc[:, sl]
        qn_ref[:, sl] = qh * (lax.rsqrt(jnp.sum(qh * qh, axis=-1, keepdims=True) + NORM_EPS) * (LANES ** -0.5))
        kn_ref[:, sl] = kh * lax.rsqrt(jnp.sum(kh * kh, axis=-1, keepdims=True) + NORM_EPS)
    ps = ps_ref[...].reshape(bb * tb, LANES)
    beta_ref[...] = jax.nn.sigmoid(ps)
    g_ref[...] = -ea_ref[...] * _softplus(ps + dtb_ref[...])

    ii = lax.broadcasted_iota(jnp.int32, (rows, rows), 0)
    jj = lax.broadcasted_iota(jnp.int32, (rows, rows), 1)
    if bb == 1:
        le = jj <= ii
        lt = jj < ii
    else:
        same = (ii // chunk) == (jj // chunk)
        le = jnp.logical_and(same, jj <= ii)
        lt = jnp.logical_and(same, jj < ii)
    le_b = jnp.where(le, 1.0, 0.0).astype(BF16)
    assert chunk & (chunk - 1) == 0 and chunk >= 2
    levels = []
    b = 1
    while b < chunk:
        sh = b.bit_length()
        join = jnp.logical_and((ii >> sh) == (jj >> sh), (ii >> (sh - 1)) != (jj >> (sh - 1)))
        levels.append(jnp.logical_and(join, lt))
        b *= 2
    heads = range(nheads)
    lanes = [slice(hd * LANES, (hd + 1) * LANES) for hd in heads]

    def mm(a, b):
        return jnp.dot(a.astype(BF16), b.astype(BF16), preferred_element_type=F32)

    def do_chunk(c, carry):
        r0 = pl.multiple_of(c * rows, rows)
        rsl = pl.ds(r0, rows)
        g = g_ref[rsl, :]
        beta = beta_ref[rsl, :]
        g_hi = g.astype(BF16)
        g_lo = (g - g_hi.astype(F32)).astype(BF16)
        gc = (jnp.dot(le_b, g_hi, preferred_element_type=F32)
              + jnp.dot(le_b, g_lo, preferred_element_type=F32))
        gcp = gc if rows == LANES else jnp.concatenate([gc, jnp.zeros((LANES - rows, LANES), F32)], axis=0)
        gct = gcp.T
        eg = jnp.exp(gc)
        q = [qn_ref[rsl, sl] for sl in lanes]
        k = [kn_ref[rsl, sl] for sl in lanes]
        v = [vn_ref[rsl, sl] for sl in lanes]
        bcol = [beta[:, hd:hd + 1] for hd in heads]
        gcol = [gc[:, nheads + hd:nheads + hd + 1] for hd in heads]
        egc = [eg[:, nheads + hd:nheads + hd + 1] for hd in heads]
        decay = [jnp.where(le, jnp.exp(jnp.where(le, gcol[hd] - gct[nheads + hd:nheads + hd + 1, :rows], 0.0)), 0.0)
                 for hd in heads]
        kb = [k[hd] * bcol[hd] for hd in heads]
        k16 = [k[hd].astype(BF16) for hd in heads]
        kk = [lax.dot_general(kb[hd].astype(BF16), k16[hd], NT_DIMS, preferred_element_type=F32) for hd in heads]
        qk = [lax.dot_general(q[hd].astype(BF16), k16[hd], NT_DIMS, preferred_element_type=F32) for hd in heads]
        lmat = [jnp.where(lt, kk[hd] * decay[hd], 0.0) for hd in heads]
        attn = [jnp.where(le, qk[hd] * decay[hd], 0.0).astype(BF16) for hd in heads]
        nmat = [jnp.where(levels[0], -lmat[hd], 0.0) for hd in heads]
        for lvl in levels[1:]:
            off = [jnp.where(lvl, lmat[hd], 0.0) for hd in heads]
            m = [off[hd] + mm(nmat[hd], off[hd]) for hd in heads]
            nmat = [nmat[hd] - m[hd] - mm(m[hd], nmat[hd]) for hd in heads]
        rhs = [jnp.concatenate([v[hd] * bcol[hd], kb[hd] * egc[hd]], axis=1) for hd in heads]
        sol = [rhs[hd] + mm(nmat[hd], rhs[hd]) for hd in heads]
        qe = [q[hd] * egc[hd] for hd in heads]
        s_old = [[sn_ref[b, hd] for b in range(bb)] for hd in heads]
        both = [[jnp.dot(jnp.concatenate([sol[hd][b * chunk:(b + 1) * chunk, LANES:],
                                          qe[hd][b * chunk:(b + 1) * chunk]], axis=0).astype(BF16),
                         s_old[hd][b].astype(BF16), preferred_element_type=F32) for b in range(bb)] for hd in heads]
        cat = lambda parts: parts[0] if bb == 1 else jnp.concatenate(parts, axis=0)
        v16 = [(sol[hd][:, :LANES] - cat([both[hd][b][:chunk] for b in range(bb)])).astype(BF16) for hd in heads]
        o = [cat([both[hd][b][chunk:] for b in range(bb)]) + jnp.dot(attn[hd], v16[hd], preferred_element_type=F32)
             for hd in heads]
        for hd in heads:
            for b in range(bb):
                seg = slice(b * chunk, (b + 1) * chunk)
                g_last = gcol[hd][(b + 1) * chunk - 1:(b + 1) * chunk]
                kd = (k[hd][seg] * jnp.exp(g_last - gcol[hd][seg])).astype(BF16)
                sn_ref[b, hd] = s_old[hd][b] * jnp.exp(g_last) + lax.dot_general(
                    kd, v16[hd][seg], TN_DIMS, preferred_element_type=F32)
        for hd in heads:
            sl = lanes[hd]
            on = o[hd] * lax.rsqrt(jnp.mean(o[hd] * o[hd], axis=-1, keepdims=True) + NORM_EPS) * nw_ref[...]
            if bb == 1:
                zz = z_ref[0, rsl, sl]
                o_ref[0, rsl, sl] = on * (zz * jax.nn.sigmoid(zz))
            else:
                zz = z_ref[:, :, sl].reshape(rows, LANES)
                o_ref[:, :, sl] = (on * (zz * jax.nn.sigmoid(zz))).reshape(bb, chunk, LANES)
        return carry

    if n_chunks == 1:
        do_chunk(0, 0)
    else:
        lax.fori_loop(0, n_chunks, do_chunk, 0)


def _delta(pm3, ps3, conv_state, s0, cw, a_log, dt_bias, norm_w, *, bb, tb, chunk):
    b, t, _ = pm3.shape
    nheads = a_log.shape[-1]
    w = nheads * LANES
    assert s0.shape == (b, nheads, LANES, LANES) and nheads <= 8
    assert b % bb == 0 and t % tb == 0 and tb % chunk == 0 and (bb * chunk) % 8 == 0 and bb * chunk <= LANES
    pad = lambda v: jnp.zeros((1, LANES), F32).at[0, nheads:2 * nheads].set(v)
    kern = functools.partial(_delta_kernel, chunk=chunk, nheads=nheads)
    col = lambda c: pl.BlockSpec((bb, tb, w), lambda i, j, c=c: (i, j, c))
    cst = lambda c: pl.BlockSpec((bb, CONV_W - 1, w), lambda i, j, c=c: (i, 0, c))
    cwt = lambda c: pl.BlockSpec((CONV_W, w), lambda i, j, c=c: (0, c))
    vec = pl.BlockSpec((1, LANES), lambda i, j: (0, 0))
    sblk = pl.BlockSpec((bb, nheads, LANES, LANES), lambda i, j: (i, 0, 0, 0))
    out, sn = pl.pallas_call(
        kern,
        out_shape=(jax.ShapeDtypeStruct((b, t, w), F32), jax.ShapeDtypeStruct(s0.shape, F32)),
        grid=(b // bb, t // tb),
        in_specs=[col(2), col(3), col(4), col(5),
                  pl.BlockSpec((bb, tb, LANES), lambda i, j: (i, j, 0)),
                  cst(0), cst(1), cst(2), sblk, cwt(0), cwt(1), cwt(2), vec, vec, vec],
        out_specs=(pl.BlockSpec((bb, tb, w), lambda i, j: (i, j, 0)), sblk),
        scratch_shapes=[pltpu.VMEM((bb, tb + CONV_PAD, w), F32)] * 3
                       + [pltpu.VMEM((bb * tb, w), F32)] * 3
                       + [pltpu.VMEM((bb * tb, LANES), F32)] * 2,
        compiler_params=_params("arbitrary", "arbitrary"),
        name="delta",
    )(pm3, pm3, pm3, pm3, ps3, conv_state, conv_state, conv_state, s0, cw, cw, cw,
      pad(jnp.exp(a_log.astype(F32))), pad(dt_bias.astype(F32)), norm_w.reshape(1, LANES))
    return out, sn


def _outproj_kernel(ol_ref, od_ref, x_ref, g1_ref, sc_ref, sh_ref, nw_ref, wt_ref, wb_ref, x1_ref, h2_ref):
    tm = h2_ref.shape[0]
    a = ol_ref[...].reshape(tm, ol_ref.shape[-1]).astype(BF16)
    b = od_ref[...].reshape(tm, od_ref.shape[-1]).astype(BF16)
    mix = (jnp.dot(a, wt_ref[...], preferred_element_type=F32)
           + jnp.dot(b, wb_ref[...], preferred_element_type=F32))
    x1 = x_ref[...] + g1_ref[...] * mix.reshape(x_ref.shape)
    x1_ref[...] = x1
    y = x1 * lax.rsqrt(jnp.mean(x1 * x1, axis=-1, keepdims=True) + NORM_EPS) * nw_ref[...]
    h2_ref[...] = (y * (1.0 + sc_ref[...]) + sh_ref[...]).reshape(h2_ref.shape).astype(BF16)


def _outproj(ol, od, x, g1, sc, sh, nw, wt, wb):
    b, t, d = x.shape
    n = b * t
    lw, dw = ol.shape[-1], od.shape[-1]
    tm = _tile(n, 256, 16)
    gb, r, nt = _row_groups(b, t, tm)
    tok = lambda c: pl.BlockSpec((gb, r, c), lambda i: (i // nt, i % nt, 0))
    per = pl.BlockSpec((gb, 1, d), lambda i: (i // nt, 0, 0))
    return pl.pallas_call(
        _outproj_kernel,
        out_shape=(jax.ShapeDtypeStruct((b, t, d), F32), jax.ShapeDtypeStruct((n, d), BF16)),
        grid=(n // tm,),
        in_specs=[tok(lw), tok(dw), tok(d), per, per, per,
                  pl.BlockSpec((1, 1, d), lambda i: (0, 0, 0)),
                  pl.BlockSpec((lw, d), lambda i: (0, 0)),
                  pl.BlockSpec((dw, d), lambda i: (0, 0))],
        out_specs=(tok(d), pl.BlockSpec((tm, d), lambda i: (i, 0))),
        compiler_params=_params("arbitrary"),
        name="outproj",
    )(ol, od, x, g1, sc, sh, nw, wt, wb)


def _top_values(s, count):
    vals = []
    work = s
    for r in range(count):
        m = jnp.max(work, axis=0, keepdims=True)
        vals.append(m)
        if r + 1 < count:
            work = jnp.where(work == m, -jnp.inf, work)
    return vals


def _router_kernel(h_ref, wq_ref, sk_ref, s2_ref, e2_ref, thr_ref, e1_ref, q_ref, *, nheads):
    nk = sk_ref.shape[1]
    half = sk_ref.shape[2]
    q_ref[...] = lax.dot_general(wq_ref[...], h_ref[...], NT_DIMS, preferred_element_type=F32)
    p = h_ref.shape[0]
    sub = lax.broadcasted_iota(jnp.int32, (8, p), 0)

    def head(hd, carry):
        base = pl.multiple_of(hd * (2 * half), 2 * half)
        q1 = q_ref[pl.ds(base, half), :].astype(BF16)
        q2 = q_ref[pl.ds(base + half, half), :].astype(BF16)
        s1 = jnp.dot(sk_ref[2 * hd], q1, preferred_element_type=F32)
        s2 = jnp.dot(sk_ref[2 * hd + 1], q2, preferred_element_type=F32)
        nv = PEER_TOPK + 1
        t1 = _top_values(s1, nv)
        t2 = _top_values(s2, nv)
        t2_lo = jnp.concatenate(t2[:8], axis=0)
        t2_hi = jnp.concatenate(t2[8:16], axis=0)
        pieces = [t1[0] + t2_lo, t1[0] + t2_hi, jnp.where(sub < 1, t1[0] + t2[16], -jnp.inf)]
        for a in range(1, nv):
            nb = nv // (a + 1)
            c = t1[a] + t2_lo
            pieces.append(c if nb >= 8 else jnp.where(sub < nb, c, -jnp.inf))
        cand = jnp.concatenate(pieces, axis=0)
        best = _top_values(cand, nv)
        tau = best[PEER_TOPK - 1]
        mid = 0.5 * (tau + best[PEER_TOPK])
        top = t1[0] + t2[0]
        z = jnp.sum(jnp.where(cand >= tau, jnp.exp(cand - top), 0.0), axis=0, keepdims=True)
        e2 = jnp.exp(s2 - t2[0]) / z
        thr = mid - s1
        e1 = jnp.exp(s1 - t1[0])
        for pb in range(p // LANES):
            cols = slice(pb * LANES, (pb + 1) * LANES)
            s2_ref[hd, pb] = s2[:, cols]
            e2_ref[hd, pb] = e2[:, cols]
            thr_ref[hd, pb] = thr[:, cols]
            e1_ref[hd, pb] = e1[:, cols]
        return carry

    lax.fori_loop(0, nheads, head, 0)


def _router(h2, wq_t, sk, *, tp):
    n, d = h2.shape
    nheads = sk.shape[0] // 2
    nk = sk.shape[1]
    hq = wq_t.shape[0]
    out = jax.ShapeDtypeStruct((nheads, n // LANES, nk, LANES), F32)
    oblk = pl.BlockSpec((nheads, tp // LANES, nk, LANES), lambda i: (0, i, 0, 0))
    return pl.pallas_call(
        functools.partial(_router_kernel, nheads=nheads),
        out_shape=(out, out, out, out),
        grid=(n // tp,),
        in_specs=[pl.BlockSpec((tp, d), lambda i: (i, 0)),
                  pl.BlockSpec((hq, d), lambda i: (0, 0)),
                  pl.BlockSpec(sk.shape, lambda i: (0, 0, 0))],
        out_specs=(oblk, oblk, oblk, oblk),
        scratch_shapes=[pltpu.VMEM((hq, tp), F32)],
        compiler_params=_params("arbitrary"),
        name="router",
    )(h2, wq_t, sk)


def _dense_kernel(h_ref, u_ref, vt_ref, s2_ref, e2_ref, thr_ref, e1_ref, x1_ref, g2_ref, fw_ref,
                  y_ref, acc_ref, st_ref, coef_ref, *, nheads, rows_per_step, n_chunks):
    s = pl.program_id(1)
    nk = s2_ref.shape[2]
    tp = h_ref.shape[0]

    @pl.when(s == 0)
    def _():
        acc_ref[...] = jnp.zeros_like(acc_ref)
        st_ref[...] = jnp.zeros_like(st_ref)
        coef_ref[...] = jnp.zeros_like(coef_ref)

    cur = s % 2
    prev = 1 - cur
    acc_ref[...] += jnp.dot(vt_ref[...], coef_ref[cur], preferred_element_type=F32)
    chunk = jnp.clip(s - 1, 0, n_chunks - 1)
    for ii in range(rows_per_step):
        i1 = chunk * rows_per_step + ii
        rows = slice(ii * nk, (ii + 1) * nk)
        for pb in range(tp // LANES):
            cols = slice(pb * LANES, (pb + 1) * LANES)
            gate = None
            for hd in range(nheads):
                term = jnp.where(s2_ref[hd, pb] >= thr_ref[hd, pb, pl.ds(i1, 1), :],
                                 e2_ref[hd, pb] * e1_ref[hd, pb, pl.ds(i1, 1), :], 0.0)
                gate = term if gate is None else gate + term
            coef_ref[prev, rows, cols] = (jax.nn.gelu(st_ref[prev, rows, cols]) * gate).astype(BF16)
    st_ref[cur] = lax.dot_general(u_ref[...], h_ref[...], NT_DIMS, preferred_element_type=F32)

    @pl.when(s == pl.num_programs(1) - 1)
    def _():
        peer = acc_ref[...].T.reshape(x1_ref.shape)
        x2 = x1_ref[...] + g2_ref[...] * peer
        y_ref[...] = x2 * lax.rsqrt(jnp.mean(x2 * x2, axis=-1, keepdims=True) + NORM_EPS) * fw_ref[...]


def _dense(h2, u16, vt16, s2t, e2t, thr, e1, x1, g2, fw, *, tp, rows_per_step):
    b, t, d = x1.shape
    n = b * t
    nheads, _, nk, _ = s2t.shape
    ne = u16.shape[0]
    ce = rows_per_step * nk
    n_chunks = ne // ce
    gb, r, nt = _row_groups(b, t, tp)
    rblk = pl.BlockSpec((nheads, tp // LANES, nk, LANES), lambda i, s: (0, i, 0, 0))
    tok = pl.BlockSpec((gb, r, d), lambda i, s: (i // nt, i % nt, 0))
    return pl.pallas_call(
        functools.partial(_dense_kernel, nheads=nheads, rows_per_step=rows_per_step, n_chunks=n_chunks),
        out_shape=jax.ShapeDtypeStruct((b, t, d), F32),
        grid=(n // tp, n_chunks + 2),
        in_specs=[pl.BlockSpec((tp, d), lambda i, s: (i, 0)),
                  pl.BlockSpec((ce, d), lambda i, s: (jnp.minimum(s, n_chunks - 1), 0)),
                  pl.BlockSpec((d, ce), lambda i, s: (0, jnp.clip(s - 2, 0, n_chunks - 1))),
                  rblk, rblk, rblk, rblk, tok,
                  pl.BlockSpec((gb, 1, d), lambda i, s: (i // nt, 0, 0)),
                  pl.BlockSpec((1, 1, d), lambda i, s: (0, 0, 0))],
        out_specs=tok,
        scratch_shapes=[pltpu.VMEM((d, tp), F32), pltpu.VMEM((2, ce, tp), F32), pltpu.VMEM((2, ce, tp), BF16)],
        compiler_params=_params("arbitrary", "arbitrary"),
        name="dense",
    )(h2, u16, vt16, s2t, e2t, thr, e1, x1, g2, fw)


def _group(x, mod, lru_conv, lru_h, dn_conv, dn_s, reset_first, p, fw, *, lru_tiles, dn_tiles, tp):
    b, t, d = x.shape
    sh1, sc1, g1, sh2, sc2, g2 = [m.reshape(b, 1, d) for m in jnp.split(mod, 6, axis=-1)]
    pm, ps = _inproj(x, sc1, sh1, p['norm_mix_w'], p['w_main'], p['w_small'])
    pm3 = pm.reshape(b, t, -1)
    ps3 = ps.reshape(b, t, LANES)
    lw = p['lru_lambda'].shape[-1]
    out_lru, new_h = _lru(pm3, lru_conv, lru_h, p['lru_conv_w'], p['lru_conv_b'], p['lru_wg'], p['lru_ba'],
                          p['lru_bx'], p['lru_lambda'], reset_first=reset_first, bb=lru_tiles[0], tc=lru_tiles[1])
    out_dn, new_s = _delta(pm3, ps3, dn_conv, dn_s, p['dn_conv_w'], p['dn_A_log'], p['dn_dt_bias'], p['dn_norm_w'],
                           bb=dn_tiles[0], tb=dn_tiles[1], chunk=dn_tiles[2])
    keep = CONV_W - 1
    new_lru_conv = pm3[:, t - keep:, :lw]
    new_dn_conv = pm3[:, t - keep:, 2 * lw:2 * lw + dn_conv.shape[-1]]
    x1, h2 = _outproj(out_lru, out_dn, x, g1, sc2, sh2, p['norm_ffn_w'], p['w_out_top'], p['w_out_bot'])
    s2t, e2t, thr, e1 = _router(h2, p['wq_t'], p['subkeys'], tp=tp)
    y = _dense(h2, p['peer_u'], p['peer_vt'], s2t, e2t, thr, e1, x1, g2, fw, tp=tp, rows_per_step=4)
    return y, (new_lru_conv, new_h, new_dn_conv, new_s)


def kernel(x_prompt, x_sample, state_lru_conv, state_lru_h, state_dn_conv, state_dn_S, c_prompt, c_sample,
           w_ada, b_ada, norm_mix_w, norm_ffn_w, w_in, lru_conv_w, lru_conv_b, lru_wa, lru_ba, lru_wx, lru_bx,
           lru_lambda, dn_conv_w, dn_A_log, dn_dt_bias, dn_norm_w, w_out, peer_wq, peer_subkeys, peer_u, peer_v,
           final_norm_w):
    depth = w_ada.shape[0]
    assert depth == 1, "the final norm is fused into the last layer's expert kernel"
    bp, seq, d = x_prompt.shape
    bs, dseq, _ = x_sample.shape
    lw = lru_lambda.shape[-1]
    dn_heads = dn_A_log.shape[-1]
    dn_w = dn_heads * LANES
    conv_ch = dn_conv_w.shape[-1]
    assert lru_wa.shape[-1] == LANES and lw == dn_w and conv_ch == 3 * dn_w
    assert peer_subkeys.shape[-1] == LANES and peer_subkeys.shape[-2] == LANES
    main = 2 * lw + conv_ch + dn_w
    l = 0
    w_small = jnp.zeros((d, LANES), F32).at[:, :2 * dn_heads].set(w_in[l][:, main:]).astype(BF16)
    nheads = peer_subkeys.shape[1]
    p = {
        'norm_mix_w': norm_mix_w[l].reshape(1, 1, d), 'norm_ffn_w': norm_ffn_w[l].reshape(1, 1, d),
        'w_main': w_in[l][:, :main].astype(BF16), 'w_small': w_small,
        'lru_conv_w': lru_conv_w[l], 'lru_conv_b': lru_conv_b[l],
        'lru_wg': jnp.concatenate([lru_wa[l], lru_wx[l]], axis=-1).astype(BF16),
        'lru_ba': lru_ba[l], 'lru_bx': lru_bx[l], 'lru_lambda': lru_lambda[l],
        'dn_conv_w': dn_conv_w[l], 'dn_A_log': dn_A_log[l], 'dn_dt_bias': dn_dt_bias[l], 'dn_norm_w': dn_norm_w[l],
        'w_out_top': w_out[l][:lw].astype(BF16), 'w_out_bot': w_out[l][lw:].astype(BF16),
        'wq_t': peer_wq[l].T.astype(BF16),
        'subkeys': peer_subkeys[l].reshape(nheads * 2, LANES, LANES).astype(BF16),
        'peer_u': peer_u[l].astype(BF16), 'peer_vt': peer_v[l].T.astype(BF16),
    }
    fw = final_norm_w.reshape(1, 1, d)
    mod = _ada(jnp.concatenate([c_prompt, c_sample], axis=0), w_ada[l], b_ada[l])
    zeros = lambda *s: jnp.zeros(s, F32)
    yp, sp = _group(x_prompt, mod[:bp], zeros(bp, CONV_W - 1, lw), zeros(bp, lw), zeros(bp, CONV_W - 1, conv_ch),
                    zeros(bp, dn_heads, LANES, LANES), True, p, fw,
                    lru_tiles=(bp, _tile(seq, 128, 8)), dn_tiles=(1, _tile(seq, 256, DN_CHUNK), min(DN_CHUNK, seq)),
                    tp=_tile(bp * seq, 512, LANES))
    ys, ss = _group(x_sample, mod[bp:], state_lru_conv[l], state_lru_h[l], state_dn_conv[l], state_dn_S[l],
                    False, p, fw,
                    lru_tiles=(_tile(bs, 32, 1), dseq), dn_tiles=(_tile(bs, DN_CHUNK // dseq, 1), dseq, dseq),
                    tp=_tile(bs * dseq, 512, LANES))
    stack = lambda v: v[None]
    return (yp, ys, stack(sp[0]), stack(sp[1]), stack(sp[2]), stack(sp[3]),
            stack(ss[0]), stack(ss[1]), stack(ss[2]), stack(ss[3]))
```

```python
import functools
import math

import jax
import jax.numpy as jnp
from jax import lax
from jax.experimental import pallas as pl
from jax.experimental.pallas import tpu as pltpu

F32 = jnp.float32
BF16 = jnp.bfloat16
NORM_EPS = 1e-6
LRU_C = 8.0
CONV_W = 4
CONV_PAD = 8
PEER_TOPK = 16
DN_CHUNK = 64
LANES = 128
VMEM_LIMIT_BYTES = 60 * 1024 * 1024
NT_DIMS = (((1,), (1,)), ((), ()))
TN_DIMS = (((0,), (0,)), ((), ()))


def _params(*semantics):
    return pltpu.CompilerParams(dimension_semantics=semantics, vmem_limit_bytes=VMEM_LIMIT_BYTES)


def _tile(n, target, multiple):
    if n <= target:
        return n
    t = (target // multiple) * multiple
    while t >= multiple:
        if n % t == 0:
            return t
        t -= multiple
    raise ValueError(f"no tile for {n} <= {target} in multiples of {multiple}")


def _softplus(x):
    return jnp.maximum(x, 0.0) + jnp.log1p(jnp.exp(-jnp.abs(x)))


def _row_groups(batch, seq, tm):
    r = min(seq, tm)
    assert seq % r == 0 and tm % r == 0 and (batch * seq) % tm == 0
    return tm // r, r, seq // r


def _ada_kernel(c_ref, w_ref, b_ref, o_ref):
    c = c_ref[...]
    a = (c * jax.nn.sigmoid(c)).astype(BF16)
    o_ref[...] = jnp.dot(a, w_ref[...].astype(BF16), preferred_element_type=F32) + b_ref[...]


def _ada(c, w, b):
    bc, d = c.shape
    n = w.shape[1]
    tn = _tile(n, 1024, LANES)
    return pl.pallas_call(
        _ada_kernel,
        out_shape=jax.ShapeDtypeStruct((bc, n), F32),
        grid=(n // tn,),
        in_specs=[pl.BlockSpec((bc, d), lambda j: (0, 0)),
                  pl.BlockSpec((d, tn), lambda j: (0, j)),
                  pl.BlockSpec((1, tn), lambda j: (0, j))],
        out_specs=pl.BlockSpec((bc, tn), lambda j: (0, j)),
        compiler_params=_params("arbitrary"),
        name="ada",
    )(c, w, b.reshape(1, n))


def _inproj_kernel(x_ref, sc_ref, sh_ref, nw_ref, wm_ref, ws_ref, om_ref, os_ref, h_ref):
    @pl.when(pl.program_id(1) == 0)
    def _():
        x = x_ref[...]
        y = x * lax.rsqrt(jnp.mean(x * x, axis=-1, keepdims=True) + NORM_EPS) * nw_ref[...]
        h = y * (1.0 + sc_ref[...]) + sh_ref[...]
        hb = h.reshape(h_ref.shape).astype(BF16)
        h_ref[...] = hb
        os_ref[...] = jnp.dot(hb, ws_ref[...], preferred_element_type=F32)

    om_ref[...] = jnp.dot(h_ref[...], wm_ref[...], preferred_element_type=F32)


def _inproj(x, sc, sh, nw, wm, ws):
    b, t, d = x.shape
    n = b * t
    m = wm.shape[1]
    tm = _tile(n, 512, 8)
    tn = _tile(m, 3072, LANES)
    gb, r, nt = _row_groups(b, t, tm)
    return pl.pallas_call(
        _inproj_kernel,
        out_shape=(jax.ShapeDtypeStruct((n, m), F32), jax.ShapeDtypeStruct((n, LANES), F32)),
        grid=(n // tm, m // tn),
        in_specs=[pl.BlockSpec((gb, r, d), lambda i, j: (i // nt, i % nt, 0)),
                  pl.BlockSpec((gb, 1, d), lambda i, j: (i // nt, 0, 0)),
                  pl.BlockSpec((gb, 1, d), lambda i, j: (i // nt, 0, 0)),
                  pl.BlockSpec((1, 1, d), lambda i, j: (0, 0, 0)),
                  pl.BlockSpec((d, tn), lambda i, j: (0, j)),
                  pl.BlockSpec((d, LANES), lambda i, j: (0, 0))],
        out_specs=(pl.BlockSpec((tm, tn), lambda i, j: (i, j)),
                   pl.BlockSpec((tm, LANES), lambda i, j: (i, 0))),
        scratch_shapes=[pltpu.VMEM((tm, d), BF16)],
        compiler_params=_params("arbitrary", "arbitrary"),
        name="inproj",
    )(x, sc, sh, nw, wm, ws)


def _causal_conv(x_ref, xs_ref, cw_ref):
    t = x_ref.shape[1]
    xs_ref[:, CONV_PAD:CONV_PAD + t, :] = x_ref[...]
    cw = cw_ref[...]
    y = None
    for k in range(CONV_W):
        lo = CONV_PAD - (CONV_W - 1) + k
        term = xs_ref[:, lo:lo + t, :] * cw[k:k + 1, :][None]
        y = term if y is None else y + term
    xs_ref[:, CONV_PAD - (CONV_W - 1):CONV_PAD, :] = xs_ref[:, t + CONV_PAD - (CONV_W - 1):t + CONV_PAD, :]
    return y


def _lru_kernel(xl_ref, yl_ref, cs_ref, h0_ref, cw_ref, cb_ref, wg_ref, ba_ref, bx_ref, lam_ref,
                o_ref, hn_ref, xs_ref, a_ref, u_ref, h_ref, *, reset_first):
    ti = pl.program_id(1)
    bb, tc, lw = xl_ref.shape
    nheads = lw // LANES

    @pl.when(ti == 0)
    def _():
        xs_ref[:, CONV_PAD - (CONV_W - 1):CONV_PAD, :] = cs_ref[...]
        h_ref[...] = h0_ref[...]

    xc = (_causal_conv(xl_ref, xs_ref, cw_ref) + cb_ref[...][None]).reshape(bb * tc, lw)
    sp = _softplus(-lam_ref[...])
    t_glob = lax.broadcasted_iota(jnp.int32, (bb, tc, LANES), 1) + ti * tc
    for hd in range(nheads):
        sl = slice(hd * LANES, (hd + 1) * LANES)
        xh = xc[:, sl]
        gates = jnp.dot(xh.astype(BF16), wg_ref[hd], preferred_element_type=F32)
        r = jax.nn.sigmoid(gates[:, :LANES] + ba_ref[:, sl])
        i = jax.nn.sigmoid(gates[:, LANES:] + bx_ref[:, sl])
        log_a = -LRU_C * r * sp[:, sl]
        a = jnp.exp(log_a)
        mult = jnp.sqrt(jnp.maximum(-jnp.tanh(log_a) * (a * a + 1.0), 0.0))
        u = i * xh
        a3 = a.reshape(bb, tc, LANES)
        m3 = mult.reshape(bb, tc, LANES)
        if reset_first:
            m3 = jnp.where(t_glob == 0, 1.0, m3)
        a_ref[:, :, sl] = a3
        u_ref[:, :, sl] = m3 * u.reshape(bb, tc, LANES)

    def step(t, h):
        h = a_ref[:, pl.ds(t, 1), :] * h + u_ref[:, pl.ds(t, 1), :]
        u_ref[:, pl.ds(t, 1), :] = h
        return h

    h = lax.fori_loop(0, tc, step, h_ref[...], unroll=8)
    h_ref[...] = h
    hn_ref[...] = h
    o_ref[...] = u_ref[...] * jax.nn.gelu(yl_ref[...])


def _lru(pm3, conv_state, h0, cw, cb, wg, ba, bx, lam, *, reset_first, bb, tc):
    b, t, _ = pm3.shape
    lw = lam.shape[-1]
    assert b % bb == 0 and t % tc == 0 and tc % 8 == 0
    row = lambda v: v.reshape(1, lw)
    kern = functools.partial(_lru_kernel, reset_first=reset_first)
    out, hn = pl.pallas_call(
        kern,
        out_shape=(jax.ShapeDtypeStruct((b, t, lw), F32), jax.ShapeDtypeStruct((b, 1, lw), F32)),
        grid=(b // bb, t // tc),
        in_specs=[pl.BlockSpec((bb, tc, lw), lambda i, j: (i, j, 0)),
                  pl.BlockSpec((bb, tc, lw), lambda i, j: (i, j, 1)),
                  pl.BlockSpec((bb, CONV_W - 1, lw), lambda i, j: (i, 0, 0)),
                  pl.BlockSpec((bb, 1, lw), lambda i, j: (i, 0, 0)),
                  pl.BlockSpec((CONV_W, lw), lambda i, j: (0, 0)),
                  pl.BlockSpec((1, lw), lambda i, j: (0, 0)),
                  pl.BlockSpec(wg.shape, lambda i, j: (0, 0, 0)),
                  pl.BlockSpec((1, lw), lambda i, j: (0, 0)),
                  pl.BlockSpec((1, lw), lambda i, j: (0, 0)),
                  pl.BlockSpec((1, lw), lambda i, j: (0, 0))],
        out_specs=(pl.BlockSpec((bb, tc, lw), lambda i, j: (i, j, 0)),
                   pl.BlockSpec((bb, 1, lw), lambda i, j: (i, 0, 0))),
        scratch_shapes=[pltpu.VMEM((bb, tc + CONV_PAD, lw), F32),
                        pltpu.VMEM((bb, tc, lw), F32),
                        pltpu.VMEM((bb, tc, lw), F32),
                        pltpu.VMEM((bb, 1, lw), F32)],
        compiler_params=_params("arbitrary", "arbitrary"),
        name="lru",
    )(pm3, pm3, conv_state, h0.reshape(b, 1, lw), cw, row(cb), wg, row(ba), row(bx), row(lam))
    return out, hn.reshape(b, lw)


def _delta_kernel(q_ref, k_ref, v_ref, z_ref, ps_ref, csq_ref, csk_ref, csv_ref, s0_ref,
                  cwq_ref, cwk_ref, cwv_ref, ea_ref, dtb_ref, nw_ref,
                  o_ref, sn_ref,
                  xq_ref, xk_ref, xv_ref, qn_ref, kn_ref, vn_ref, beta_ref, g_ref, *, chunk, nheads):
    ti = pl.program_id(1)
    bb, tb, w = q_ref.shape
    n_chunks = tb // chunk
    rows = bb * chunk
    assert bb == 1 or n_chunks == 1
    lo = CONV_PAD - (CONV_W - 1)

    @pl.when(ti == 0)
    def _():
        xq_ref[:, lo:CONV_PAD, :] = csq_ref[...]
        xk_ref[:, lo:CONV_PAD, :] = csk_ref[...]
        xv_ref[:, lo:CONV_PAD, :] = csv_ref[...]
        sn_ref[...] = s0_ref[...]

    def conv_silu(x_ref, xs_ref, cw_ref):
        y = _causal_conv(x_ref, xs_ref, cw_ref)
        return (y * jax.nn.sigmoid(y)).reshape(bb * tb, w)

    qc = conv_silu(q_ref, xq_ref, cwq_ref)
    kc = conv_silu(k_ref, xk_ref, cwk_ref)
    vn_ref[...] = conv_silu(v_ref, xv_ref, cwv_ref)
    for hd in range(nheads):
        sl = slice(hd * LANES, (hd + 1) * LANES)
        qh = qc[:, sl]
        kh = kc[:, sl]
        qn_ref[:, sl] = qh * (lax.rsqrt(jnp.sum(qh * qh, axis=-1, keepdims=True) + NORM_EPS) * (LANES ** -0.5))
        kn_ref[:, sl] = kh * lax.rsqrt(jnp.sum(kh * kh, axis=-1, keepdims=True) + NORM_EPS)
    ps = ps_ref[...].reshape(bb * tb, LANES)
    beta_ref[...] = jax.nn.sigmoid(ps)
    g_ref[...] = -ea_ref[...] * _softplus(ps + dtb_ref[...])

    ii = lax.broadcasted_iota(jnp.int32, (rows, rows), 0)
    jj = lax.broadcasted_iota(jnp.int32, (rows, rows), 1)
    if bb == 1:
        le = jj <= ii
        lt = jj < ii
    else:
        same = (ii // chunk) == (jj // chunk)
        le = jnp.logical_and(same, jj <= ii)
        lt = jnp.logical_and(same, jj < ii)
    le_b = jnp.where(le, 1.0, 0.0).astype(BF16)
    assert chunk & (chunk - 1) == 0 and chunk >= 2
    levels = []
    b = 1
    while b < chunk:
        sh = b.bit_length()
        join = jnp.logical_and((ii >> sh) == (jj >> sh), (ii >> (sh - 1)) != (jj >> (sh - 1)))
        levels.append(jnp.logical_and(join, lt))
        b *= 2
    heads = range(nheads)
    lanes = [slice(hd * LANES, (hd + 1) * LANES) for hd in heads]

    def mm(a, b):
        return jnp.dot(a.astype(BF16), b.astype(BF16), preferred_element_type=F32)

    def do_chunk(c, carry):
        r0 = pl.multiple_of(c * rows, rows)
        rsl = pl.ds(r0, rows)
        g = g_ref[rsl, :]
        beta = beta_ref[rsl, :]
        g_hi = g.astype(BF16)
        g_lo = (g - g_hi.astype(F32)).astype(BF16)
        gc = (jnp.dot(le_b, g_hi, preferred_element_type=F32)
              + jnp.dot(le_b, g_lo, preferred_element_type=F32))
        gcp = gc if rows == LANES else jnp.concatenate([gc, jnp.zeros((LANES - rows, LANES), F32)], axis=0)
        gct = gcp.T
        eg = jnp.exp(gc)
        q = [qn_ref[rsl, sl] for sl in lanes]
        k = [kn_ref[rsl, sl] for sl in lanes]
        v = [vn_ref[rsl, sl] for sl in lanes]
        bcol = [beta[:, hd:hd + 1] for hd in heads]
        gcol = [gc[:, nheads + hd:nheads + hd + 1] for hd in heads]
        egc = [eg[:, nheads + hd:nheads + hd + 1] for hd in heads]
        decay = [jnp.where(le, jnp.exp(jnp.where(le, gcol[hd] - gct[nheads + hd:nheads + hd + 1, :rows], 0.0)), 0.0)
                 for hd in heads]
        kb = [k[hd] * bcol[hd] for hd in heads]
        k16 = [k[hd].astype(BF16) for hd in heads]
        kk = [lax.dot_general(kb[hd].astype(BF16), k16[hd], NT_DIMS, preferred_element_type=F32) for hd in heads]
        qk = [lax.dot_general(q[hd].astype(BF16), k16[hd], NT_DIMS, preferred_element_type=F32) for hd in heads]
        lmat = [jnp.where(lt, kk[hd] * decay[hd], 0.0) for hd in heads]
        attn = [jnp.where(le, qk[hd] * decay[hd], 0.0).astype(BF16) for hd in heads]
        nmat = [jnp.where(levels[0], -lmat[hd], 0.0) for hd in heads]
        for lvl in levels[1:]:
            off = [jnp.where(lvl, lmat[hd], 0.0) for hd in heads]
            m = [off[hd] + mm(nmat[hd], off[hd]) for hd in heads]
            nmat = [nmat[hd] - m[hd] - mm(m[hd], nmat[hd]) for hd in heads]
        rhs = [jnp.concatenate([v[hd] * bcol[hd], kb[hd] * egc[hd]], axis=1) for hd in heads]
        sol = [rhs[hd] + mm(nmat[hd], rhs[hd]) for hd in heads]
        qe = [q[hd] * egc[hd] for hd in heads]
        s_old = [[sn_ref[b, hd] for b in range(bb)] for hd in heads]
        both = [[jnp.dot(jnp.concatenate([sol[hd][b * chunk:(b + 1) * chunk, LANES:],
                                          qe[hd][b * chunk:(b + 1) * chunk]], axis=0).astype(BF16),
                         s_old[hd][b].astype(BF16), preferred_element_type=F32) for b in range(bb)] for hd in heads]
        cat = lambda parts: parts[0] if bb == 1 else jnp.concatenate(parts, axis=0)
        v16 = [(sol[hd][:, :LANES] - cat([both[hd][b][:chunk] for b in range(bb)])).astype(BF16) for hd in heads]
        o = [cat([both[hd][b][chunk:] for b in range(bb)]) + jnp.dot(attn[hd], v16[hd], preferred_element_type=F32)
             for hd in heads]
        for hd in heads:
            for b in range(bb):
                seg = slice(b * chunk, (b + 1) * chunk)
                g_last = gcol[hd][(b + 1) * chunk - 1:(b + 1) * chunk]
                kd = (k[hd][seg] * jnp.exp(g_last - gcol[hd][seg])).astype(BF16)
                sn_ref[b, hd] = s_old[hd][b] * jnp.exp(g_last) + lax.dot_general(
                    kd, v16[hd][seg], TN_DIMS, preferred_element_type=F32)
        for hd in heads:
            sl = lanes[hd]
            on = o[hd] * lax.rsqrt(jnp.mean(o[hd] * o[hd], axis=-1, keepdims=True) + NORM_EPS) * nw_ref[...]
            if bb == 1:
                zz = z_ref[0, rsl, sl]
                o_ref[0, rsl, sl] = on * (zz * jax.nn.sigmoid(zz))
            else:
                zz = z_ref[:, :, sl].reshape(rows, LANES)
                o_ref[:, :, sl] = (on * (zz * jax.nn.sigmoid(zz))).reshape(bb, chunk, LANES)
        return carry

    if n_chunks == 1:
        do_chunk(0, 0)
    else:
        lax.fori_loop(0, n_chunks, do_chunk, 0)


def _delta(pm3, ps3, conv_state, s0, cw, a_log, dt_bias, norm_w, *, bb, tb, chunk):
    b, t, _ = pm3.shape
    nheads = a_log.shape[-1]
    w = nheads * LANES
    assert s0.shape == (b, nheads, LANES, LANES) and nheads <= 8
    assert b % bb == 0 and t % tb == 0 and tb % chunk == 0 and (bb * chunk) % 8 == 0 and bb * chunk <= LANES
    pad = lambda v: jnp.zeros((1, LANES), F32).at[0, nheads:2 * nheads].set(v)
    kern = functools.partial(_delta_kernel, chunk=chunk, nheads=nheads)
    col = lambda c: pl.BlockSpec((bb, tb, w), lambda i, j, c=c: (i, j, c))
    cst = lambda c: pl.BlockSpec((bb, CONV_W - 1, w), lambda i, j, c=c: (i, 0, c))
    cwt = lambda c: pl.BlockSpec((CONV_W, w), lambda i, j, c=c: (0, c))
    vec = pl.BlockSpec((1, LANES), lambda i, j: (0, 0))
    sblk = pl.BlockSpec((bb, nheads, LANES, LANES), lambda i, j: (i, 0, 0, 0))
    out, sn = pl.pallas_call(
        kern,
        out_shape=(jax.ShapeDtypeStruct((b, t, w), F32), jax.ShapeDtypeStruct(s0.shape, F32)),
        grid=(b // bb, t // tb),
        in_specs=[col(2), col(3), col(4), col(5),
                  pl.BlockSpec((bb, tb, LANES), lambda i, j: (i, j, 0)),
                  cst(0), cst(1), cst(2), sblk, cwt(0), cwt(1), cwt(2), vec, vec, vec],
        out_specs=(pl.BlockSpec((bb, tb, w), lambda i, j: (i, j, 0)), sblk),
        scratch_shapes=[pltpu.VMEM((bb, tb + CONV_PAD, w), F32)] * 3
                       + [pltpu.VMEM((bb * tb, w), F32)] * 3
                       + [pltpu.VMEM((bb * tb, LANES), F32)] * 2,
        compiler_params=_params("arbitrary", "arbitrary"),
        name="delta",
    )(pm3, pm3, pm3, pm3, ps3, conv_state, conv_state, conv_state, s0, cw, cw, cw,
      pad(jnp.exp(a_log.astype(F32))), pad(dt_bias.astype(F32)), norm_w.reshape(1, LANES))
    return out, sn


def _outproj_kernel(ol_ref, od_ref, x_ref, g1_ref, sc_ref, sh_ref, nw_ref, wt_ref, wb_ref, x1_ref, h2_ref):
    tm = h2_ref.shape[0]
    a = ol_ref[...].reshape(tm, ol_ref.shape[-1]).astype(BF16)
    b = od_ref[...].reshape(tm, od_ref.shape[-1]).astype(BF16)
    mix = (jnp.dot(a, wt_ref[...], preferred_element_type=F32)
           + jnp.dot(b, wb_ref[...], preferred_element_type=F32))
    x1 = x_ref[...] + g1_ref[...] * mix.reshape(x_ref.shape)
    x1_ref[...] = x1
    y = x1 * lax.rsqrt(jnp.mean(x1 * x1, axis=-1, keepdims=True) + NORM_EPS) * nw_ref[...]
    h2_ref[...] = (y * (1.0 + sc_ref[...]) + sh_ref[...]).reshape(h2_ref.shape).astype(BF16)


def _outproj(ol, od, x, g1, sc, sh, nw, wt, wb):
    b, t, d = x.shape
    n = b * t
    lw, dw = ol.shape[-1], od.shape[-1]
    tm = _tile(n, 512, 16)
    gb, r, nt = _row_groups(b, t, tm)
    tok = lambda c: pl.BlockSpec((gb, r, c), lambda i: (i // nt, i % nt, 0))
    per = pl.BlockSpec((gb, 1, d), lambda i: (i // nt, 0, 0))
    return pl.pallas_call(
        _outproj_kernel,
        out_shape=(jax.ShapeDtypeStruct((b, t, d), F32), jax.ShapeDtypeStruct((n, d), BF16)),
        grid=(n // tm,),
        in_specs=[tok(lw), tok(dw), tok(d), per, per, per,
                  pl.BlockSpec((1, 1, d), lambda i: (0, 0, 0)),
                  pl.BlockSpec((lw, d), lambda i: (0, 0)),
                  pl.BlockSpec((dw, d), lambda i: (0, 0))],
        out_specs=(tok(d), pl.BlockSpec((tm, d), lambda i: (i, 0))),
        compiler_params=_params("arbitrary"),
        name="outproj",
    )(ol, od, x, g1, sc, sh, nw, wt, wb)


def _sorting_network(n):
    pairs = []
    p = 1
    while p < n:
        k = p
        while k >= 1:
            for j in range(k % p, n - k, 2 * k):
                for i in range(min(k, n - j - k)):
                    if (i + j) // (2 * p) == (i + j + k) // (2 * p):
                        pairs.append((i + j, i + j + k))
            k //= 2
        p *= 2
    return pairs


def _merge_top(lists, singles, count):
    lists = list(lists)
    singles = list(singles)
    vals = []
    for r in range(count):
        head = lists[0]
        for x in singles:
            head = jnp.maximum(head, x)
        m = jnp.max(head, axis=0, keepdims=True)
        vals.append(m)
        left = count - r - 1
        if left == 0:
            break
        eq = lists[0] == m
        keep = min(len(lists), left)
        lists = [jnp.where(eq, lists[v + 1] if v + 1 < len(lists) else -jnp.inf, lists[v]) for v in range(keep)]
        singles = [jnp.where(x == m, -jnp.inf, x) for x in singles]
    return vals


def _top_values(s, count):
    tiles = [s[8 * v:8 * (v + 1)] for v in range(s.shape[0] // 8)]
    for i, j in _sorting_network(len(tiles)):
        tiles[i], tiles[j] = jnp.maximum(tiles[i], tiles[j]), jnp.minimum(tiles[i], tiles[j])
    return _merge_top(tiles, [], count)


def _router_kernel(h_ref, wq_ref, sk_ref, s2_ref, e2_ref, thr_ref, e1_ref, q_ref, *, nheads):
    nk = sk_ref.shape[1]
    half = sk_ref.shape[2]
    q_ref[...] = lax.dot_general(wq_ref[...], h_ref[...], NT_DIMS, preferred_element_type=F32)
    p = h_ref.shape[0]
    sub = lax.broadcasted_iota(jnp.int32, (8, p), 0)

    def head(hd, carry):
        base = pl.multiple_of(hd * (2 * half), 2 * half)
        q1 = q_ref[pl.ds(base, half), :].astype(BF16)
        q2 = q_ref[pl.ds(base + half, half), :].astype(BF16)
        s1 = jnp.dot(sk_ref[2 * hd], q1, preferred_element_type=F32)
        s2 = jnp.dot(sk_ref[2 * hd + 1], q2, preferred_element_type=F32)
        nv = PEER_TOPK + 1
        t1 = _top_values(s1, nv)
        t2 = _top_values(s2, nv)
        t2_lo = jnp.concatenate(t2[:8], axis=0)
        t2_hi = jnp.concatenate(t2[8:16], axis=0)
        lists = [t1[0] + t2_lo]
        for a in range(1, nv):
            nb = nv // (a + 1)
            c = t1[a] + t2_lo
            lists.append(c if nb >= 8 else jnp.where(sub < nb, c, -jnp.inf))
        singles = [t1[0] + t2_hi, jnp.where(sub < 1, t1[0] + t2[16], -jnp.inf)]
        cand = jnp.concatenate(lists + singles, axis=0)
        best = _merge_top(lists, singles, nv)
        tau = best[PEER_TOPK - 1]
        mid = 0.5 * (tau + best[PEER_TOPK])
        top = t1[0] + t2[0]
        z = jnp.sum(jnp.where(cand >= tau, jnp.exp(cand - top), 0.0), axis=0, keepdims=True)
        s2_ref[hd] = s2
        e2_ref[hd] = jnp.exp(s2 - t2[0]) / z
        thr_ref[hd] = mid - s1
        e1_ref[hd] = jnp.exp(s1 - t1[0])
        return carry

    lax.fori_loop(0, nheads, head, 0)


def _router(h2, wq_t, sk, *, tp):
    n, d = h2.shape
    nheads = sk.shape[0] // 2
    nk = sk.shape[1]
    hq = wq_t.shape[0]
    out = jax.ShapeDtypeStruct((nheads, nk, n), F32)
    oblk = pl.BlockSpec((nheads, nk, tp), lambda i: (0, 0, i))
    return pl.pallas_call(
        functools.partial(_router_kernel, nheads=nheads),
        out_shape=(out, out, out, out),
        grid=(n // tp,),
        in_specs=[pl.BlockSpec((tp, d), lambda i: (i, 0)),
                  pl.BlockSpec((hq, d), lambda i: (0, 0)),
                  pl.BlockSpec(sk.shape, lambda i: (0, 0, 0))],
        out_specs=(oblk, oblk, oblk, oblk),
        scratch_shapes=[pltpu.VMEM((hq, tp), F32)],
        compiler_params=_params("arbitrary"),
        name="router",
    )(h2, wq_t, sk)


def _dense_kernel(h_ref, u_ref, vt_ref, s2_ref, e2_ref, thr_ref, e1_ref, x1_ref, g2_ref, fw_ref,
                  y_ref, acc_ref, st_ref, coef_ref, *, nheads, rows_per_step, n_chunks):
    s = pl.program_id(1)
    nk = s2_ref.shape[1]
    tp = h_ref.shape[0]

    @pl.when(s == 0)
    def _():
        acc_ref[...] = jnp.zeros_like(acc_ref)
        st_ref[...] = jnp.zeros_like(st_ref)
        coef_ref[...] = jnp.zeros_like(coef_ref)

    cur = s % 2
    prev = 1 - cur
    acc_ref[...] += jnp.dot(vt_ref[...], coef_ref[cur], preferred_element_type=F32)
    chunk = jnp.clip(s - 1, 0, n_chunks - 1)

    def gates(ii):
        i1 = chunk * rows_per_step + ii
        rows = slice(ii * nk, (ii + 1) * nk)
        thr_rows = [thr_ref[hd, pl.ds(i1, 1), :] for hd in range(nheads)]
        e1_rows = [e1_ref[hd, pl.ds(i1, 1), :] for hd in range(nheads)]
        for pb in range(tp // LANES):
            cols = slice(pb * LANES, (pb + 1) * LANES)
            gate = None
            for hd in range(nheads):
                term = jnp.where(s2_ref[hd, :, cols] >= thr_rows[hd][:, cols],
                                 e2_ref[hd, :, cols] * e1_rows[hd][:, cols], 0.0)
                gate = term if gate is None else gate + term
            coef_ref[prev, rows, cols] = (jax.nn.gelu(st_ref[prev, rows, cols]) * gate).astype(BF16)

    for ii in range(rows_per_step // 2):
        gates(ii)
    st_ref[cur] = lax.dot_general(u_ref[...], h_ref[...], NT_DIMS, preferred_element_type=F32)
    for ii in range(rows_per_step // 2, rows_per_step):
        gates(ii)

    @pl.when(s == pl.num_programs(1) - 1)
    def _():
        peer = acc_ref[...].T.reshape(x1_ref.shape)
        x2 = x1_ref[...] + g2_ref[...] * peer
        y_ref[...] = x2 * lax.rsqrt(jnp.mean(x2 * x2, axis=-1, keepdims=True) + NORM_EPS) * fw_ref[...]


def _dense(h2, u16, vt16, s2t, e2t, thr, e1, x1, g2, fw, *, tp, rows_per_step):
    b, t, d = x1.shape
    n = b * t
    nheads, nk, _ = s2t.shape
    ne = u16.shape[0]
    ce = rows_per_step * nk
    n_chunks = ne // ce
    gb, r, nt = _row_groups(b, t, tp)
    rblk = pl.BlockSpec((nheads, nk, tp), lambda i, s: (0, 0, i))
    tok = pl.BlockSpec((gb, r, d), lambda i, s: (i // nt, i % nt, 0))
    return pl.pallas_call(
        functools.partial(_dense_kernel, nheads=nheads, rows_per_step=rows_per_step, n_chunks=n_chunks),
        out_shape=jax.ShapeDtypeStruct((b, t, d), F32),
        grid=(n // tp, n_chunks + 2),
        in_specs=[pl.BlockSpec((tp, d), lambda i, s: (i, 0)),
                  pl.BlockSpec((ce, d), lambda i, s: (jnp.minimum(s, n_chunks - 1), 0)),
                  pl.BlockSpec((d, ce), lambda i, s: (0, jnp.clip(s - 2, 0, n_chunks - 1))),
                  rblk, rblk, rblk, rblk, tok,
                  pl.BlockSpec((gb, 1, d), lambda i, s: (i // nt, 0, 0)),
                  pl.BlockSpec((1, 1, d), lambda i, s: (0, 0, 0))],
        out_specs=tok,
        scratch_shapes=[pltpu.VMEM((d, tp), F32), pltpu.VMEM((2, ce, tp), F32), pltpu.VMEM((2, ce, tp), BF16)],
        compiler_params=_params("arbitrary", "arbitrary"),
        name="dense",
    )(h2, u16, vt16, s2t, e2t, thr, e1, x1, g2, fw)


def _group(x, mod, lru_conv, lru_h, dn_conv, dn_s, reset_first, p, fw, *, lru_tiles, dn_tiles, tp):
    b, t, d = x.shape
    sh1, sc1, g1, sh2, sc2, g2 = [m.reshape(b, 1, d) for m in jnp.split(mod, 6, axis=-1)]
    pm, ps = _inproj(x, sc1, sh1, p['norm_mix_w'], p['w_main'], p['w_small'])
    pm3 = pm.reshape(b, t, -1)
    ps3 = ps.reshape(b, t, LANES)
    lw = p['lru_lambda'].shape[-1]
    out_lru, new_h = _lru(pm3, lru_conv, lru_h, p['lru_conv_w'], p['lru_conv_b'], p['lru_wg'], p['lru_ba'],
                          p['lru_bx'], p['lru_lambda'], reset_first=reset_first, bb=lru_tiles[0], tc=lru_tiles[1])
    out_dn, new_s = _delta(pm3, ps3, dn_conv, dn_s, p['dn_conv_w'], p['dn_A_log'], p['dn_dt_bias'], p['dn_norm_w'],
                           bb=dn_tiles[0], tb=dn_tiles[1], chunk=dn_tiles[2])
    keep = CONV_W - 1
    new_lru_conv = pm3[:, t - keep:, :lw]
    new_dn_conv = pm3[:, t - keep:, 2 * lw:2 * lw + dn_conv.shape[-1]]
    x1, h2 = _outproj(out_lru, out_dn, x, g1, sc2, sh2, p['norm_ffn_w'], p['w_out_top'], p['w_out_bot'])
    s2t, e2t, thr, e1 = _router(h2, p['wq_t'], p['subkeys'], tp=tp)
    y = _dense(h2, p['peer_u'], p['peer_vt'], s2t, e2t, thr, e1, x1, g2, fw, tp=tp, rows_per_step=4)
    return y, (new_lru_conv, new_h, new_dn_conv, new_s)


def kernel(x_prompt, x_sample, state_lru_conv, state_lru_h, state_dn_conv, state_dn_S, c_prompt, c_sample,
           w_ada, b_ada, norm_mix_w, norm_ffn_w, w_in, lru_conv_w, lru_conv_b, lru_wa, lru_ba, lru_wx, lru_bx,
           lru_lambda, dn_conv_w, dn_A_log, dn_dt_bias, dn_norm_w, w_out, peer_wq, peer_subkeys, peer_u, peer_v,
           final_norm_w):
    depth = w_ada.shape[0]
    assert depth == 1, "the final norm is fused into the last layer's expert kernel"
    bp, seq, d = x_prompt.shape
    bs, dseq, _ = x_sample.shape
    lw = lru_lambda.shape[-1]
    dn_heads = dn_A_log.shape[-1]
    dn_w = dn_heads * LANES
    conv_ch = dn_conv_w.shape[-1]
    assert lru_wa.shape[-1] == LANES and lw == dn_w and conv_ch == 3 * dn_w
    assert peer_subkeys.shape[-1] == LANES and peer_subkeys.shape[-2] == LANES
    main = 2 * lw + conv_ch + dn_w
    l = 0
    w_small = jnp.zeros((d, LANES), F32).at[:, :2 * dn_heads].set(w_in[l][:, main:]).astype(BF16)
    nheads = peer_subkeys.shape[1]
    p = {
        'norm_mix_w': norm_mix_w[l].reshape(1, 1, d), 'norm_ffn_w': norm_ffn_w[l].reshape(1, 1, d),
        'w_main': w_in[l][:, :main].astype(BF16), 'w_small': w_small,
        'lru_conv_w': lru_conv_w[l], 'lru_conv_b': lru_conv_b[l],
        'lru_wg': jnp.concatenate([lru_wa[l], lru_wx[l]], axis=-1).astype(BF16),
        'lru_ba': lru_ba[l], 'lru_bx': lru_bx[l], 'lru_lambda': lru_lambda[l],
        'dn_conv_w': dn_conv_w[l], 'dn_A_log': dn_A_log[l], 'dn_dt_bias': dn_dt_bias[l], 'dn_norm_w': dn_norm_w[l],
        'w_out_top': w_out[l][:lw].astype(BF16), 'w_out_bot': w_out[l][lw:].astype(BF16),
        'wq_t': peer_wq[l].T.astype(BF16),
        'subkeys': peer_subkeys[l].reshape(nheads * 2, LANES, LANES).astype(BF16),
        'peer_u': peer_u[l].astype(BF16), 'peer_vt': peer_v[l].T.astype(BF16),
    }
    fw = final_norm_w.reshape(1, 1, d)
    mod = _ada(jnp.concatenate([c_prompt, c_sample], axis=0), w_ada[l], b_ada[l])
    zeros = lambda *s: jnp.zeros(s, F32)
    yp, sp = _group(x_prompt, mod[:bp], zeros(bp, CONV_W - 1, lw), zeros(bp, lw), zeros(bp, CONV_W - 1, conv_ch),
                    zeros(bp, dn_heads, LANES, LANES), True, p, fw,
                    lru_tiles=(bp, _tile(seq, 128, 8)), dn_tiles=(1, _tile(seq, 256, DN_CHUNK), min(DN_CHUNK, seq)),
                    tp=_tile(bp * seq, 512, LANES))
    ys, ss = _group(x_sample, mod[bp:], state_lru_conv[l], state_lru_h[l], state_dn_conv[l], state_dn_S[l],
                    False, p, fw,
                    lru_tiles=(_tile(bs, 32, 1), dseq), dn_tiles=(_tile(bs, DN_CHUNK // dseq, 1), dseq, dseq),
                    tp=_tile(bs * dseq, 512, LANES))
    stack = lambda v: v[None]
    return (yp, ys, stack(sp[0]), stack(sp[1]), stack(sp[2]), stack(sp[3]),
            stack(ss[0]), stack(ss[1]), stack(ss[2]), stack(ss[3]))
```

```python
import functools
import math

import jax
import jax.numpy as jnp
from jax import lax
from jax.experimental import pallas as pl
from jax.experimental.pallas import tpu as pltpu

F32 = jnp.float32
BF16 = jnp.bfloat16
NORM_EPS = 1e-6
LRU_C = 8.0
CONV_W = 4
CONV_PAD = 8
PEER_TOPK = 16
DN_CHUNK = 64
LANES = 128
VMEM_LIMIT_BYTES = 60 * 1024 * 1024
NT_DIMS = (((1,), (1,)), ((), ()))
TN_DIMS = (((0,), (0,)), ((), ()))


def _params(*semantics):
    return pltpu.CompilerParams(dimension_semantics=semantics, vmem_limit_bytes=VMEM_LIMIT_BYTES)


def _tile(n, target, multiple):
    if n <= target:
        return n
    t = (target // multiple) * multiple
    while t >= multiple:
        if n % t == 0:
            return t
        t -= multiple
    raise ValueError(f"no tile for {n} <= {target} in multiples of {multiple}")


def _softplus(x):
    return jnp.maximum(x, 0.0) + jnp.log1p(jnp.exp(-jnp.abs(x)))


def _row_groups(batch, seq, tm):
    r = min(seq, tm)
    assert seq % r == 0 and tm % r == 0 and (batch * seq) % tm == 0
    return tm // r, r, seq // r


def _ada_kernel(c_ref, w_ref, b_ref, o_ref):
    c = c_ref[...]
    a = (c * jax.nn.sigmoid(c)).astype(BF16)
    o_ref[...] = jnp.dot(a, w_ref[...].astype(BF16), preferred_element_type=F32) + b_ref[...]


def _ada(c, w, b):
    bc, d = c.shape
    n = w.shape[1]
    tn = _tile(n, 1024, LANES)
    return pl.pallas_call(
        _ada_kernel,
        out_shape=jax.ShapeDtypeStruct((bc, n), F32),
        grid=(n // tn,),
        in_specs=[pl.BlockSpec((bc, d), lambda j: (0, 0)),
                  pl.BlockSpec((d, tn), lambda j: (0, j)),
                  pl.BlockSpec((1, tn), lambda j: (0, j))],
        out_specs=pl.BlockSpec((bc, tn), lambda j: (0, j)),
        compiler_params=_params("arbitrary"),
        name="ada",
    )(c, w, b.reshape(1, n))


def _inproj_kernel(x_ref, sc_ref, sh_ref, nw_ref, wm_ref, ws_ref, om_ref, os_ref, h_ref):
    @pl.when(pl.program_id(1) == 0)
    def _():
        x = x_ref[...]
        y = x * lax.rsqrt(jnp.mean(x * x, axis=-1, keepdims=True) + NORM_EPS) * nw_ref[...]
        h = y * (1.0 + sc_ref[...]) + sh_ref[...]
        hb = h.reshape(h_ref.shape).astype(BF16)
        h_ref[...] = hb
        os_ref[...] = jnp.dot(hb, ws_ref[...], preferred_element_type=F32)

    om_ref[...] = jnp.dot(h_ref[...], wm_ref[...], preferred_element_type=F32)


def _inproj(x, sc, sh, nw, wm, ws):
    b, t, d = x.shape
    n = b * t
    m = wm.shape[1]
    tm = _tile(n, 512, 8)
    tn = _tile(m, 3072, LANES)
    gb, r, nt = _row_groups(b, t, tm)
    return pl.pallas_call(
        _inproj_kernel,
        out_shape=(jax.ShapeDtypeStruct((n, m), F32), jax.ShapeDtypeStruct((n, LANES), F32)),
        grid=(n // tm, m // tn),
        in_specs=[pl.BlockSpec((gb, r, d), lambda i, j: (i // nt, i % nt, 0)),
                  pl.BlockSpec((gb, 1, d), lambda i, j: (i // nt, 0, 0)),
                  pl.BlockSpec((gb, 1, d), lambda i, j: (i // nt, 0, 0)),
                  pl.BlockSpec((1, 1, d), lambda i, j: (0, 0, 0)),
                  pl.BlockSpec((d, tn), lambda i, j: (0, j)),
                  pl.BlockSpec((d, LANES), lambda i, j: (0, 0))],
        out_specs=(pl.BlockSpec((tm, tn), lambda i, j: (i, j)),
                   pl.BlockSpec((tm, LANES), lambda i, j: (i, 0))),
        scratch_shapes=[pltpu.VMEM((tm, d), BF16)],
        compiler_params=_params("arbitrary", "arbitrary"),
        name="inproj",
    )(x, sc, sh, nw, wm, ws)


def _causal_conv(x_ref, xs_ref, cw_ref):
    t = x_ref.shape[1]
    xs_ref[:, CONV_PAD:CONV_PAD + t, :] = x_ref[...]
    cw = cw_ref[...]
    y = None
    for k in range(CONV_W):
        lo = CONV_PAD - (CONV_W - 1) + k
        term = xs_ref[:, lo:lo + t, :] * cw[k:k + 1, :][None]
        y = term if y is None else y + term
    xs_ref[:, CONV_PAD - (CONV_W - 1):CONV_PAD, :] = xs_ref[:, t + CONV_PAD - (CONV_W - 1):t + CONV_PAD, :]
    return y


def _lru_kernel(xl_ref, yl_ref, cs_ref, h0_ref, cw_ref, cb_ref, wg_ref, ba_ref, bx_ref, lam_ref,
                o_ref, hn_ref, xs_ref, a_ref, u_ref, h_ref, *, reset_first):
    ti = pl.program_id(1)
    bb, tc, lw = xl_ref.shape
    nheads = lw // LANES

    @pl.when(ti == 0)
    def _():
        xs_ref[:, CONV_PAD - (CONV_W - 1):CONV_PAD, :] = cs_ref[...]
        h_ref[...] = h0_ref[...]

    xc = (_causal_conv(xl_ref, xs_ref, cw_ref) + cb_ref[...][None]).reshape(bb * tc, lw)
    sp = _softplus(-lam_ref[...])
    t_glob = lax.broadcasted_iota(jnp.int32, (bb, tc, LANES), 1) + ti * tc
    for hd in range(nheads):
        sl = slice(hd * LANES, (hd + 1) * LANES)
        xh = xc[:, sl]
        gates = jnp.dot(xh.astype(BF16), wg_ref[hd], preferred_element_type=F32)
        r = jax.nn.sigmoid(gates[:, :LANES] + ba_ref[:, sl])
        i = jax.nn.sigmoid(gates[:, LANES:] + bx_ref[:, sl])
        log_a = -LRU_C * r * sp[:, sl]
        a = jnp.exp(log_a)
        mult = jnp.sqrt(jnp.maximum(-jnp.tanh(log_a) * (a * a + 1.0), 0.0))
        u = i * xh
        a3 = a.reshape(bb, tc, LANES)
        m3 = mult.reshape(bb, tc, LANES)
        if reset_first:
            m3 = jnp.where(t_glob == 0, 1.0, m3)
        a_ref[:, :, sl] = a3
        u_ref[:, :, sl] = m3 * u.reshape(bb, tc, LANES)

    def step(t, h):
        h = a_ref[:, pl.ds(t, 1), :] * h + u_ref[:, pl.ds(t, 1), :]
        u_ref[:, pl.ds(t, 1), :] = h
        return h

    h = lax.fori_loop(0, tc, step, h_ref[...], unroll=8)
    h_ref[...] = h
    hn_ref[...] = h
    o_ref[...] = u_ref[...] * jax.nn.gelu(yl_ref[...])


def _lru(pm3, conv_state, h0, cw, cb, wg, ba, bx, lam, *, reset_first, bb, tc):
    b, t, _ = pm3.shape
    lw = lam.shape[-1]
    assert b % bb == 0 and t % tc == 0 and tc % 8 == 0
    row = lambda v: v.reshape(1, lw)
    kern = functools.partial(_lru_kernel, reset_first=reset_first)
    out, hn = pl.pallas_call(
        kern,
        out_shape=(jax.ShapeDtypeStruct((b, t, lw), F32), jax.ShapeDtypeStruct((b, 1, lw), F32)),
        grid=(b // bb, t // tc),
        in_specs=[pl.BlockSpec((bb, tc, lw), lambda i, j: (i, j, 0)),
                  pl.BlockSpec((bb, tc, lw), lambda i, j: (i, j, 1)),
                  pl.BlockSpec((bb, CONV_W - 1, lw), lambda i, j: (i, 0, 0)),
                  pl.BlockSpec((bb, 1, lw), lambda i, j: (i, 0, 0)),
                  pl.BlockSpec((CONV_W, lw), lambda i, j: (0, 0)),
                  pl.BlockSpec((1, lw), lambda i, j: (0, 0)),
                  pl.BlockSpec(wg.shape, lambda i, j: (0, 0, 0)),
                  pl.BlockSpec((1, lw), lambda i, j: (0, 0)),
                  pl.BlockSpec((1, lw), lambda i, j: (0, 0)),
                  pl.BlockSpec((1, lw), lambda i, j: (0, 0))],
        out_specs=(pl.BlockSpec((bb, tc, lw), lambda i, j: (i, j, 0)),
                   pl.BlockSpec((bb, 1, lw), lambda i, j: (i, 0, 0))),
        scratch_shapes=[pltpu.VMEM((bb, tc + CONV_PAD, lw), F32),
                        pltpu.VMEM((bb, tc, lw), F32),
                        pltpu.VMEM((bb, tc, lw), F32),
                        pltpu.VMEM((bb, 1, lw), F32)],
        compiler_params=_params("arbitrary", "arbitrary"),
        name="lru",
    )(pm3, pm3, conv_state, h0.reshape(b, 1, lw), cw, row(cb), wg, row(ba), row(bx), row(lam))
    return out, hn.reshape(b, lw)


def _delta_kernel(q_ref, k_ref, v_ref, z_ref, ps_ref, csq_ref, csk_ref, csv_ref, s0_ref,
                  cwq_ref, cwk_ref, cwv_ref, ea_ref, dtb_ref, nw_ref,
                  o_ref, sn_ref,
                  xq_ref, xk_ref, xv_ref, qn_ref, kn_ref, vn_ref, beta_ref, g_ref, *, chunk, nheads):
    ti = pl.program_id(1)
    bb, tb, w = q_ref.shape
    n_chunks = tb // chunk
    rows = bb * chunk
    assert bb == 1 or n_chunks == 1
    lo = CONV_PAD - (CONV_W - 1)

    @pl.when(ti == 0)
    def _():
        xq_ref[:, lo:CONV_PAD, :] = csq_ref[...]
        xk_ref[:, lo:CONV_PAD, :] = csk_ref[...]
        xv_ref[:, lo:CONV_PAD, :] = csv_ref[...]
        sn_ref[...] = s0_ref[...]

    def conv_silu(x_ref, xs_ref, cw_ref):
        y = _causal_conv(x_ref, xs_ref, cw_ref)
        return (y * jax.nn.sigmoid(y)).reshape(bb * tb, w)

    qc = conv_silu(q_ref, xq_ref, cwq_ref)
    kc = conv_silu(k_ref, xk_ref, cwk_ref)
    vn_ref[...] = conv_silu(v_ref, xv_ref, cwv_ref)
    for hd in range(nheads):
        sl = slice(hd * LANES, (hd + 1) * LANES)
        qh = qc[:, sl]
        kh = kc[:, sl]
        qn_ref[:, sl] = qh * (lax.rsqrt(jnp.sum(qh * qh, axis=-1, keepdims=True) + NORM_EPS) * (LANES ** -0.5))
        kn_ref[:, sl] = kh * lax.rsqrt(jnp.sum(kh * kh, axis=-1, keepdims=True) + NORM_EPS)
    ps = ps_ref[...].reshape(bb * tb, LANES)
    beta_ref[...] = jax.nn.sigmoid(ps)
    g_ref[...] = -ea_ref[...] * _softplus(ps + dtb_ref[...])

    ii = lax.broadcasted_iota(jnp.int32, (rows, rows), 0)
    jj = lax.broadcasted_iota(jnp.int32, (rows, rows), 1)
    if bb == 1:
        le = jj <= ii
        lt = jj < ii
    else:
        same = (ii // chunk) == (jj // chunk)
        le = jnp.logical_and(same, jj <= ii)
        lt = jnp.logical_and(same, jj < ii)
    le_b = jnp.where(le, 1.0, 0.0).astype(BF16)
    assert chunk & (chunk - 1) == 0 and chunk >= 2
    levels = []
    b = 1
    while b < chunk:
        sh = b.bit_length()
        join = jnp.logical_and((ii >> sh) == (jj >> sh), (ii >> (sh - 1)) != (jj >> (sh - 1)))
        levels.append(jnp.logical_and(join, lt))
        b *= 2
    heads = range(nheads)
    lanes = [slice(hd * LANES, (hd + 1) * LANES) for hd in heads]

    def mm(a, b):
        return jnp.dot(a.astype(BF16), b.astype(BF16), preferred_element_type=F32)

    def do_chunk(c, carry):
        r0 = pl.multiple_of(c * rows, rows)
        rsl = pl.ds(r0, rows)
        g = g_ref[rsl, :]
        beta = beta_ref[rsl, :]
        g_hi = g.astype(BF16)
        g_lo = (g - g_hi.astype(F32)).astype(BF16)
        gc = (jnp.dot(le_b, g_hi, preferred_element_type=F32)
              + jnp.dot(le_b, g_lo, preferred_element_type=F32))
        gcp = gc if rows == LANES else jnp.concatenate([gc, jnp.zeros((LANES - rows, LANES), F32)], axis=0)
        gct = gcp.T
        eg = jnp.exp(gc)
        q = [qn_ref[rsl, sl] for sl in lanes]
        k = [kn_ref[rsl, sl] for sl in lanes]
        v = [vn_ref[rsl, sl] for sl in lanes]
        bcol = [beta[:, hd:hd + 1] for hd in heads]
        gcol = [gc[:, nheads + hd:nheads + hd + 1] for hd in heads]
        egc = [eg[:, nheads + hd:nheads + hd + 1] for hd in heads]
        decay = [jnp.where(le, jnp.exp(jnp.where(le, gcol[hd] - gct[nheads + hd:nheads + hd + 1, :rows], 0.0)), 0.0)
                 for hd in heads]
        kb = [k[hd] * bcol[hd] for hd in heads]
        k16 = [k[hd].astype(BF16) for hd in heads]
        kk = [lax.dot_general(kb[hd].astype(BF16), k16[hd], NT_DIMS, preferred_element_type=F32) for hd in heads]
        qk = [lax.dot_general(q[hd].astype(BF16), k16[hd], NT_DIMS, preferred_element_type=F32) for hd in heads]
        lmat = [jnp.where(lt, kk[hd] * decay[hd], 0.0) for hd in heads]
        attn = [jnp.where(le, qk[hd] * decay[hd], 0.0).astype(BF16) for hd in heads]
        nmat = [jnp.where(levels[0], -lmat[hd], 0.0) for hd in heads]
        for lvl in levels[1:]:
            off = [jnp.where(lvl, lmat[hd], 0.0) for hd in heads]
            m = [off[hd] + mm(nmat[hd], off[hd]) for hd in heads]
            nmat = [nmat[hd] - m[hd] - mm(m[hd], nmat[hd]) for hd in heads]
        rhs = [jnp.concatenate([v[hd] * bcol[hd], kb[hd] * egc[hd]], axis=1) for hd in heads]
        sol = [rhs[hd] + mm(nmat[hd], rhs[hd]) for hd in heads]
        qe = [q[hd] * egc[hd] for hd in heads]
        s_old = [[sn_ref[b, hd] for b in range(bb)] for hd in heads]
        both = [[jnp.dot(jnp.concatenate([sol[hd][b * chunk:(b + 1) * chunk, LANES:],
                                          qe[hd][b * chunk:(b + 1) * chunk]], axis=0).astype(BF16),
                         s_old[hd][b].astype(BF16), preferred_element_type=F32) for b in range(bb)] for hd in heads]
        cat = lambda parts: parts[0] if bb == 1 else jnp.concatenate(parts, axis=0)
        v16 = [(sol[hd][:, :LANES] - cat([both[hd][b][:chunk] for b in range(bb)])).astype(BF16) for hd in heads]
        o = [cat([both[hd][b][chunk:] for b in range(bb)]) + jnp.dot(attn[hd], v16[hd], preferred_element_type=F32)
             for hd in heads]
        for hd in heads:
            for b in range(bb):
                seg = slice(b * chunk, (b + 1) * chunk)
                g_last = gcol[hd][(b + 1) * chunk - 1:(b + 1) * chunk]
                kd = (k[hd][seg] * jnp.exp(g_last - gcol[hd][seg])).astype(BF16)
                sn_ref[b, hd] = s_old[hd][b] * jnp.exp(g_last) + lax.dot_general(
                    kd, v16[hd][seg], TN_DIMS, preferred_element_type=F32)
        for hd in heads:
            sl = lanes[hd]
            on = o[hd] * lax.rsqrt(jnp.mean(o[hd] * o[hd], axis=-1, keepdims=True) + NORM_EPS) * nw_ref[...]
            if bb == 1:
                zz = z_ref[0, rsl, sl]
                o_ref[0, rsl, sl] = on * (zz * jax.nn.sigmoid(zz))
            else:
                zz = z_ref[:, :, sl].reshape(rows, LANES)
                o_ref[:, :, sl] = (on * (zz * jax.nn.sigmoid(zz))).reshape(bb, chunk, LANES)
        return carry

    if n_chunks == 1:
        do_chunk(0, 0)
    else:
        lax.fori_loop(0, n_chunks, do_chunk, 0)


def _delta(pm3, ps3, conv_state, s0, cw, a_log, dt_bias, norm_w, *, bb, tb, chunk):
    b, t, _ = pm3.shape
    nheads = a_log.shape[-1]
    w = nheads * LANES
    assert s0.shape == (b, nheads, LANES, LANES) and nheads <= 8
    assert b % bb == 0 and t % tb == 0 and tb % chunk == 0 and (bb * chunk) % 8 == 0 and bb * chunk <= LANES
    pad = lambda v: jnp.zeros((1, LANES), F32).at[0, nheads:2 * nheads].set(v)
    kern = functools.partial(_delta_kernel, chunk=chunk, nheads=nheads)
    col = lambda c: pl.BlockSpec((bb, tb, w), lambda i, j, c=c: (i, j, c))
    cst = lambda c: pl.BlockSpec((bb, CONV_W - 1, w), lambda i, j, c=c: (i, 0, c))
    cwt = lambda c: pl.BlockSpec((CONV_W, w), lambda i, j, c=c: (0, c))
    vec = pl.BlockSpec((1, LANES), lambda i, j: (0, 0))
    sblk = pl.BlockSpec((bb, nheads, LANES, LANES), lambda i, j: (i, 0, 0, 0))
    out, sn = pl.pallas_call(
        kern,
        out_shape=(jax.ShapeDtypeStruct((b, t, w), F32), jax.ShapeDtypeStruct(s0.shape, F32)),
        grid=(b // bb, t // tb),
        in_specs=[col(2), col(3), col(4), col(5),
                  pl.BlockSpec((bb, tb, LANES), lambda i, j: (i, j, 0)),
                  cst(0), cst(1), cst(2), sblk, cwt(0), cwt(1), cwt(2), vec, vec, vec],
        out_specs=(pl.BlockSpec((bb, tb, w), lambda i, j: (i, j, 0)), sblk),
        scratch_shapes=[pltpu.VMEM((bb, tb + CONV_PAD, w), F32)] * 3
                       + [pltpu.VMEM((bb * tb, w), F32)] * 3
                       + [pltpu.VMEM((bb * tb, LANES), F32)] * 2,
        compiler_params=_params("arbitrary", "arbitrary"),
        name="delta",
    )(pm3, pm3, pm3, pm3, ps3, conv_state, conv_state, conv_state, s0, cw, cw, cw,
      pad(jnp.exp(a_log.astype(F32))), pad(dt_bias.astype(F32)), norm_w.reshape(1, LANES))
    return out, sn


def _outproj_kernel(ol_ref, od_ref, x_ref, g1_ref, sc_ref, sh_ref, nw_ref, wt_ref, wb_ref, x1_ref, h2_ref):
    tm = h2_ref.shape[0]
    a = ol_ref[...].reshape(tm, ol_ref.shape[-1]).astype(BF16)
    b = od_ref[...].reshape(tm, od_ref.shape[-1]).astype(BF16)
    mix = (jnp.dot(a, wt_ref[...], preferred_element_type=F32)
           + jnp.dot(b, wb_ref[...], preferred_element_type=F32))
    x1 = x_ref[...] + g1_ref[...] * mix.reshape(x_ref.shape)
    x1_ref[...] = x1
    y = x1 * lax.rsqrt(jnp.mean(x1 * x1, axis=-1, keepdims=True) + NORM_EPS) * nw_ref[...]
    h2_ref[...] = (y * (1.0 + sc_ref[...]) + sh_ref[...]).reshape(h2_ref.shape).astype(BF16)


def _outproj(ol, od, x, g1, sc, sh, nw, wt, wb):
    b, t, d = x.shape
    n = b * t
    lw, dw = ol.shape[-1], od.shape[-1]
    tm = _tile(n, 512, 16)
    gb, r, nt = _row_groups(b, t, tm)
    tok = lambda c: pl.BlockSpec((gb, r, c), lambda i: (i // nt, i % nt, 0))
    per = pl.BlockSpec((gb, 1, d), lambda i: (i // nt, 0, 0))
    return pl.pallas_call(
        _outproj_kernel,
        out_shape=(jax.ShapeDtypeStruct((b, t, d), F32), jax.ShapeDtypeStruct((n, d), BF16)),
        grid=(n // tm,),
        in_specs=[tok(lw), tok(dw), tok(d), per, per, per,
                  pl.BlockSpec((1, 1, d), lambda i: (0, 0, 0)),
                  pl.BlockSpec((lw, d), lambda i: (0, 0)),
                  pl.BlockSpec((dw, d), lambda i: (0, 0))],
        out_specs=(tok(d), pl.BlockSpec((tm, d), lambda i: (i, 0))),
        compiler_params=_params("arbitrary"),
        name="outproj",
    )(ol, od, x, g1, sc, sh, nw, wt, wb)


def _sorting_network(n):
    pairs = []
    p = 1
    while p < n:
        k = p
        while k >= 1:
            for j in range(k % p, n - k, 2 * k):
                for i in range(min(k, n - j - k)):
                    if (i + j) // (2 * p) == (i + j + k) // (2 * p):
                        pairs.append((i + j, i + j + k))
            k //= 2
        p *= 2
    return pairs


def _merge_top(lists, singles, count):
    lists = list(lists)
    singles = list(singles)
    vals = []
    for r in range(count):
        head = lists[0]
        for x in singles:
            head = jnp.maximum(head, x)
        m = jnp.max(head, axis=0, keepdims=True)
        vals.append(m)
        left = count - r - 1
        if left == 0:
            break
        eq = lists[0] == m
        keep = min(len(lists), left)
        lists = [jnp.where(eq, lists[v + 1] if v + 1 < len(lists) else -jnp.inf, lists[v]) for v in range(keep)]
        singles = [jnp.where(x == m, -jnp.inf, x) for x in singles]
    return vals


def _top_values(s, count):
    tiles = [s[8 * v:8 * (v + 1)] for v in range(s.shape[0] // 8)]
    for i, j in _sorting_network(len(tiles)):
        tiles[i], tiles[j] = jnp.maximum(tiles[i], tiles[j]), jnp.minimum(tiles[i], tiles[j])
    return _merge_top(tiles, [], count)


def _router_kernel(h_ref, wq_ref, sk_ref, s2_ref, e2_ref, thr_ref, e1_ref, q_ref, *, nheads):
    nk = sk_ref.shape[1]
    half = sk_ref.shape[2]
    q_ref[...] = lax.dot_general(wq_ref[...], h_ref[...], NT_DIMS, preferred_element_type=F32)
    p = h_ref.shape[0]
    sub = lax.broadcasted_iota(jnp.int32, (8, p), 0)

    def head(hd, carry):
        base = pl.multiple_of(hd * (2 * half), 2 * half)
        q1 = q_ref[pl.ds(base, half), :].astype(BF16)
        q2 = q_ref[pl.ds(base + half, half), :].astype(BF16)
        s1 = jnp.dot(sk_ref[2 * hd], q1, preferred_element_type=F32)
        s2 = jnp.dot(sk_ref[2 * hd + 1], q2, preferred_element_type=F32)
        nv = PEER_TOPK + 1
        t1 = _top_values(s1, nv)
        t2 = _top_values(s2, nv)
        t2_lo = jnp.concatenate(t2[:8], axis=0)
        t2_hi = jnp.concatenate(t2[8:16], axis=0)
        lists = [t1[0] + t2_lo]
        for a in range(1, nv):
            nb = nv // (a + 1)
            c = t1[a] + t2_lo
            lists.append(c if nb >= 8 else jnp.where(sub < nb, c, -jnp.inf))
        singles = [t1[0] + t2_hi, jnp.where(sub < 1, t1[0] + t2[16], -jnp.inf)]
        cand = jnp.concatenate(lists + singles, axis=0)
        best = _merge_top(lists, singles, nv)
        tau = best[PEER_TOPK - 1]
        mid = 0.5 * (tau + best[PEER_TOPK])
        top = t1[0] + t2[0]
        z = jnp.sum(jnp.where(cand >= tau, jnp.exp(cand - top), 0.0), axis=0, keepdims=True)
        e2 = jnp.exp(s2 - t2[0]) / z
        for pb in range(p // LANES):
            s2_ref[hd, pb] = s2[:, pb * LANES:(pb + 1) * LANES]
            e2_ref[hd, pb] = e2[:, pb * LANES:(pb + 1) * LANES]
        thr_ref[hd] = mid - s1
        e1_ref[hd] = jnp.exp(s1 - t1[0])
        return carry

    lax.fori_loop(0, nheads, head, 0)


def _router(h2, wq_t, sk, *, tp):
    n, d = h2.shape
    nheads = sk.shape[0] // 2
    nk = sk.shape[1]
    hq = wq_t.shape[0]
    out = jax.ShapeDtypeStruct((nheads, nk, n), F32)
    oblk = pl.BlockSpec((nheads, nk, tp), lambda i: (0, 0, i))
    tiled = jax.ShapeDtypeStruct((nheads, n // LANES, nk, LANES), F32)
    tblk = pl.BlockSpec((nheads, tp // LANES, nk, LANES), lambda i: (0, i, 0, 0))
    return pl.pallas_call(
        functools.partial(_router_kernel, nheads=nheads),
        out_shape=(tiled, tiled, out, out),
        grid=(n // tp,),
        in_specs=[pl.BlockSpec((tp, d), lambda i: (i, 0)),
                  pl.BlockSpec((hq, d), lambda i: (0, 0)),
                  pl.BlockSpec(sk.shape, lambda i: (0, 0, 0))],
        out_specs=(tblk, tblk, oblk, oblk),
        scratch_shapes=[pltpu.VMEM((hq, tp), F32)],
        compiler_params=_params("arbitrary"),
        name="router",
    )(h2, wq_t, sk)


def _dense_kernel(h_ref, u_ref, vt_ref, s2_ref, e2_ref, thr_ref, e1_ref, x1_ref, g2_ref, fw_ref,
                  y_ref, acc_ref, st_ref, coef_ref, *, nheads, rows_per_step, n_chunks):
    s = pl.program_id(1)
    nk = thr_ref.shape[1]
    tp = h_ref.shape[0]

    @pl.when(s == 0)
    def _():
        acc_ref[...] = jnp.zeros_like(acc_ref)
        st_ref[...] = jnp.zeros_like(st_ref)
        coef_ref[...] = jnp.zeros_like(coef_ref)

    cur = s % 2
    prev = 1 - cur
    acc_ref[...] += jnp.dot(vt_ref[...], coef_ref[cur], preferred_element_type=F32)
    chunk = jnp.clip(s - 1, 0, n_chunks - 1)

    def gates(ii):
        i1 = chunk * rows_per_step + ii
        rows = slice(ii * nk, (ii + 1) * nk)
        thr_rows = [thr_ref[hd, pl.ds(i1, 1), :] for hd in range(nheads)]
        e1_rows = [e1_ref[hd, pl.ds(i1, 1), :] for hd in range(nheads)]
        for pb in range(tp // LANES):
            cols = slice(pb * LANES, (pb + 1) * LANES)
            gate = None
            for hd in range(nheads):
                term = jnp.where(s2_ref[hd, pb] >= thr_rows[hd][:, cols],
                                 e2_ref[hd, pb] * e1_rows[hd][:, cols], 0.0)
                gate = term if gate is None else gate + term
            coef_ref[prev, rows, cols] = (jax.nn.gelu(st_ref[prev, rows, cols]) * gate).astype(BF16)

    for ii in range(rows_per_step // 2):
        gates(ii)
    st_ref[cur] = lax.dot_general(u_ref[...], h_ref[...], NT_DIMS, preferred_element_type=F32)
    for ii in range(rows_per_step // 2, rows_per_step):
        gates(ii)

    @pl.when(s == pl.num_programs(1) - 1)
    def _():
        peer = acc_ref[...].T.reshape(x1_ref.shape)
        x2 = x1_ref[...] + g2_ref[...] * peer
        y_ref[...] = x2 * lax.rsqrt(jnp.mean(x2 * x2, axis=-1, keepdims=True) + NORM_EPS) * fw_ref[...]


def _dense(h2, u16, vt16, s2t, e2t, thr, e1, x1, g2, fw, *, tp, rows_per_step):
    b, t, d = x1.shape
    n = b * t
    nheads, nk, _ = thr.shape
    ne = u16.shape[0]
    ce = rows_per_step * nk
    n_chunks = ne // ce
    gb, r, nt = _row_groups(b, t, tp)
    rblk = pl.BlockSpec((nheads, nk, tp), lambda i, s: (0, 0, i))
    tblk = pl.BlockSpec((nheads, tp // LANES, nk, LANES), lambda i, s: (0, i, 0, 0))
    tok = pl.BlockSpec((gb, r, d), lambda i, s: (i // nt, i % nt, 0))
    return pl.pallas_call(
        functools.partial(_dense_kernel, nheads=nheads, rows_per_step=rows_per_step, n_chunks=n_chunks),
        out_shape=jax.ShapeDtypeStruct((b, t, d), F32),
        grid=(n // tp, n_chunks + 2),
        in_specs=[pl.BlockSpec((tp, d), lambda i, s: (i, 0)),
                  pl.BlockSpec((ce, d), lambda i, s: (jnp.minimum(s, n_chunks - 1), 0)),
                  pl.BlockSpec((d, ce), lambda i, s: (0, jnp.clip(s - 2, 0, n_chunks - 1))),
                  tblk, tblk, rblk, rblk, tok,
                  pl.BlockSpec((gb, 1, d), lambda i, s: (i // nt, 0, 0)),
                  pl.BlockSpec((1, 1, d), lambda i, s: (0, 0, 0))],
        out_specs=tok,
        scratch_shapes=[pltpu.VMEM((d, tp), F32), pltpu.VMEM((2, ce, tp), F32), pltpu.VMEM((2, ce, tp), BF16)],
        compiler_params=_params("arbitrary", "arbitrary"),
        name="dense",
    )(h2, u16, vt16, s2t, e2t, thr, e1, x1, g2, fw)


def _group(x, mod, lru_conv, lru_h, dn_conv, dn_s, reset_first, p, fw, *, lru_tiles, dn_tiles, tp):
    b, t, d = x.shape
    sh1, sc1, g1, sh2, sc2, g2 = [m.reshape(b, 1, d) for m in jnp.split(mod, 6, axis=-1)]
    pm, ps = _inproj(x, sc1, sh1, p['norm_mix_w'], p['w_main'], p['w_small'])
    pm3 = pm.reshape(b, t, -1)
    ps3 = ps.reshape(b, t, LANES)
    lw = p['lru_lambda'].shape[-1]
    out_lru, new_h = _lru(pm3, lru_conv, lru_h, p['lru_conv_w'], p['lru_conv_b'], p['lru_wg'], p['lru_ba'],
                          p['lru_bx'], p['lru_lambda'], reset_first=reset_first, bb=lru_tiles[0], tc=lru_tiles[1])
    out_dn, new_s = _delta(pm3, ps3, dn_conv, dn_s, p['dn_conv_w'], p['dn_A_log'], p['dn_dt_bias'], p['dn_norm_w'],
                           bb=dn_tiles[0], tb=dn_tiles[1], chunk=dn_tiles[2])
    keep = CONV_W - 1
    new_lru_conv = pm3[:, t - keep:, :lw]
    new_dn_conv = pm3[:, t - keep:, 2 * lw:2 * lw + dn_conv.shape[-1]]
    x1, h2 = _outproj(out_lru, out_dn, x, g1, sc2, sh2, p['norm_ffn_w'], p['w_out_top'], p['w_out_bot'])
    s2t, e2t, thr, e1 = _router(h2, p['wq_t'], p['subkeys'], tp=tp)
    y = _dense(h2, p['peer_u'], p['peer_vt'], s2t, e2t, thr, e1, x1, g2, fw, tp=tp, rows_per_step=4)
    return y, (new_lru_conv, new_h, new_dn_conv, new_s)


def kernel(x_prompt, x_sample, state_lru_conv, state_lru_h, state_dn_conv, state_dn_S, c_prompt, c_sample,
           w_ada, b_ada, norm_mix_w, norm_ffn_w, w_in, lru_conv_w, lru_conv_b, lru_wa, lru_ba, lru_wx, lru_bx,
           lru_lambda, dn_conv_w, dn_A_log, dn_dt_bias, dn_norm_w, w_out, peer_wq, peer_subkeys, peer_u, peer_v,
           final_norm_w):
    depth = w_ada.shape[0]
    assert depth == 1, "the final norm is fused into the last layer's expert kernel"
    bp, seq, d = x_prompt.shape
    bs, dseq, _ = x_sample.shape
    lw = lru_lambda.shape[-1]
    dn_heads = dn_A_log.shape[-1]
    dn_w = dn_heads * LANES
    conv_ch = dn_conv_w.shape[-1]
    assert lru_wa.shape[-1] == LANES and lw == dn_w and conv_ch == 3 * dn_w
    assert peer_subkeys.shape[-1] == LANES and peer_subkeys.shape[-2] == LANES
    main = 2 * lw + conv_ch + dn_w
    l = 0
    w_small = jnp.zeros((d, LANES), F32).at[:, :2 * dn_heads].set(w_in[l][:, main:]).astype(BF16)
    nheads = peer_subkeys.shape[1]
    p = {
        'norm_mix_w': norm_mix_w[l].reshape(1, 1, d), 'norm_ffn_w': norm_ffn_w[l].reshape(1, 1, d),
        'w_main': w_in[l][:, :main].astype(BF16), 'w_small': w_small,
        'lru_conv_w': lru_conv_w[l], 'lru_conv_b': lru_conv_b[l],
        'lru_wg': jnp.concatenate([lru_wa[l], lru_wx[l]], axis=-1).astype(BF16),
        'lru_ba': lru_ba[l], 'lru_bx': lru_bx[l], 'lru_lambda': lru_lambda[l],
        'dn_conv_w': dn_conv_w[l], 'dn_A_log': dn_A_log[l], 'dn_dt_bias': dn_dt_bias[l], 'dn_norm_w': dn_norm_w[l],
        'w_out_top': w_out[l][:lw].astype(BF16), 'w_out_bot': w_out[l][lw:].astype(BF16),
        'wq_t': peer_wq[l].T.astype(BF16),
        'subkeys': peer_subkeys[l].reshape(nheads * 2, LANES, LANES).astype(BF16),
        'peer_u': peer_u[l].astype(BF16), 'peer_vt': peer_v[l].T.astype(BF16),
    }
    fw = final_norm_w.reshape(1, 1, d)
    mod = _ada(jnp.concatenate([c_prompt, c_sample], axis=0), w_ada[l], b_ada[l])
    zeros = lambda *s: jnp.zeros(s, F32)
    yp, sp = _group(x_prompt, mod[:bp], zeros(bp, CONV_W - 1, lw), zeros(bp, lw), zeros(bp, CONV_W - 1, conv_ch),
                    zeros(bp, dn_heads, LANES, LANES), True, p, fw,
                    lru_tiles=(bp, _tile(seq, 128, 8)), dn_tiles=(1, _tile(seq, 256, DN_CHUNK), min(DN_CHUNK, seq)),
                    tp=_tile(bp * seq, 512, LANES))
    ys, ss = _group(x_sample, mod[bp:], state_lru_conv[l], state_lru_h[l], state_dn_conv[l], state_dn_S[l],
                    False, p, fw,
                    lru_tiles=(_tile(bs, 32, 1), dseq), dn_tiles=(_tile(bs, DN_CHUNK // dseq, 1), dseq, dseq),
                    tp=_tile(bs * dseq, 512, LANES))
    stack = lambda v: v[None]
    return (yp, ys, stack(sp[0]), stack(sp[1]), stack(sp[2]), stack(sp[3]),
            stack(ss[0]), stack(ss[1]), stack(ss[2]), stack(ss[3]))
```

```python
import functools
import math

import jax
import jax.numpy as jnp
from jax import lax
from jax.experimental import pallas as pl
from jax.experimental.pallas import tpu as pltpu

F32 = jnp.float32
BF16 = jnp.bfloat16
NORM_EPS = 1e-6
LRU_C = 8.0
CONV_W = 4
CONV_PAD = 8
PEER_TOPK = 16
DN_CHUNK = 64
LOG2E = math.log2(math.e)
LANES = 128
VMEM_LIMIT_BYTES = 60 * 1024 * 1024
NT_DIMS = (((1,), (1,)), ((), ()))
TN_DIMS = (((0,), (0,)), ((), ()))


def _params(*semantics):
    return pltpu.CompilerParams(dimension_semantics=semantics, vmem_limit_bytes=VMEM_LIMIT_BYTES)


def _tile(n, target, multiple):
    if n <= target:
        return n
    t = (target // multiple) * multiple
    while t >= multiple:
        if n % t == 0:
            return t
        t -= multiple
    raise ValueError(f"no tile for {n} <= {target} in multiples of {multiple}")


def _softplus(x):
    return jnp.maximum(x, 0.0) + jnp.log1p(jnp.exp(-jnp.abs(x)))


def _row_groups(batch, seq, tm):
    r = min(seq, tm)
    assert seq % r == 0 and tm % r == 0 and (batch * seq) % tm == 0
    return tm // r, r, seq // r


def _ada_kernel(c_ref, w_ref, b_ref, o_ref):
    c = c_ref[...]
    a = (c * jax.nn.sigmoid(c)).astype(BF16)
    o_ref[...] = jnp.dot(a, w_ref[...].astype(BF16), preferred_element_type=F32) + b_ref[...]


def _ada(c, w, b):
    bc, d = c.shape
    n = w.shape[1]
    tn = _tile(n, 1024, LANES)
    return pl.pallas_call(
        _ada_kernel,
        out_shape=jax.ShapeDtypeStruct((bc, n), F32),
        grid=(n // tn,),
        in_specs=[pl.BlockSpec((bc, d), lambda j: (0, 0)),
                  pl.BlockSpec((d, tn), lambda j: (0, j)),
                  pl.BlockSpec((1, tn), lambda j: (0, j))],
        out_specs=pl.BlockSpec((bc, tn), lambda j: (0, j)),
        compiler_params=_params("arbitrary"),
        name="ada",
    )(c, w, b.reshape(1, n))


def _inproj_kernel(x_ref, sc_ref, sh_ref, nw_ref, wm_ref, ws_ref, om_ref, os_ref, h_ref):
    @pl.when(pl.program_id(1) == 0)
    def _():
        x = x_ref[...]
        y = x * lax.rsqrt(jnp.mean(x * x, axis=-1, keepdims=True) + NORM_EPS) * nw_ref[...]
        h = y * (1.0 + sc_ref[...]) + sh_ref[...]
        hb = h.reshape(h_ref.shape).astype(BF16)
        h_ref[...] = hb
        os_ref[...] = jnp.dot(hb, ws_ref[...], preferred_element_type=F32)

    om_ref[...] = jnp.dot(h_ref[...], wm_ref[...], preferred_element_type=F32)


def _inproj(x, sc, sh, nw, wm, ws):
    b, t, d = x.shape
    n = b * t
    m = wm.shape[1]
    tm = _tile(n, 512, 8)
    tn = _tile(m, 3072, LANES)
    gb, r, nt = _row_groups(b, t, tm)
    return pl.pallas_call(
        _inproj_kernel,
        out_shape=(jax.ShapeDtypeStruct((n, m), F32), jax.ShapeDtypeStruct((n, LANES), F32)),
        grid=(n // tm, m // tn),
        in_specs=[pl.BlockSpec((gb, r, d), lambda i, j: (i // nt, i % nt, 0)),
                  pl.BlockSpec((gb, 1, d), lambda i, j: (i // nt, 0, 0)),
                  pl.BlockSpec((gb, 1, d), lambda i, j: (i // nt, 0, 0)),
                  pl.BlockSpec((1, 1, d), lambda i, j: (0, 0, 0)),
                  pl.BlockSpec((d, tn), lambda i, j: (0, j)),
                  pl.BlockSpec((d, LANES), lambda i, j: (0, 0))],
        out_specs=(pl.BlockSpec((tm, tn), lambda i, j: (i, j)),
                   pl.BlockSpec((tm, LANES), lambda i, j: (i, 0))),
        scratch_shapes=[pltpu.VMEM((tm, d), BF16)],
        compiler_params=_params("arbitrary", "arbitrary"),
        name="inproj",
    )(x, sc, sh, nw, wm, ws)


def _causal_conv(x_ref, xs_ref, cw_ref):
    t = x_ref.shape[1]
    xs_ref[:, CONV_PAD:CONV_PAD + t, :] = x_ref[...]
    cw = cw_ref[...]
    y = None
    for k in range(CONV_W):
        lo = CONV_PAD - (CONV_W - 1) + k
        term = xs_ref[:, lo:lo + t, :] * cw[k:k + 1, :][None]
        y = term if y is None else y + term
    xs_ref[:, CONV_PAD - (CONV_W - 1):CONV_PAD, :] = xs_ref[:, t + CONV_PAD - (CONV_W - 1):t + CONV_PAD, :]
    return y


def _lru_kernel(xl_ref, yl_ref, cs_ref, h0_ref, cw_ref, cb_ref, wg_ref, ba_ref, bx_ref, lam_ref,
                o_ref, hn_ref, xs_ref, a_ref, u_ref, h_ref, *, reset_first):
    ti = pl.program_id(1)
    bb, tc, lw = xl_ref.shape
    nheads = lw // LANES

    @pl.when(ti == 0)
    def _():
        xs_ref[:, CONV_PAD - (CONV_W - 1):CONV_PAD, :] = cs_ref[...]
        h_ref[...] = h0_ref[...]

    xc = (_causal_conv(xl_ref, xs_ref, cw_ref) + cb_ref[...][None]).reshape(bb * tc, lw)
    sp = _softplus(-lam_ref[...])
    t_glob = lax.broadcasted_iota(jnp.int32, (bb, tc, LANES), 1) + ti * tc
    for hd in range(nheads):
        sl = slice(hd * LANES, (hd + 1) * LANES)
        xh = xc[:, sl]
        gates = jnp.dot(xh.astype(BF16), wg_ref[hd], preferred_element_type=F32)
        r = jax.nn.sigmoid(gates[:, :LANES] + ba_ref[:, sl])
        i = jax.nn.sigmoid(gates[:, LANES:] + bx_ref[:, sl])
        log_a = -LRU_C * r * sp[:, sl]
        a = jnp.exp(log_a)
        mult = jnp.sqrt(jnp.maximum(-jnp.tanh(log_a) * (a * a + 1.0), 0.0))
        u = i * xh
        a3 = a.reshape(bb, tc, LANES)
        m3 = mult.reshape(bb, tc, LANES)
        if reset_first:
            m3 = jnp.where(t_glob == 0, 1.0, m3)
        a_ref[:, :, sl] = a3
        u_ref[:, :, sl] = m3 * u.reshape(bb, tc, LANES)

    def step(t, h):
        h = a_ref[:, pl.ds(t, 1), :] * h + u_ref[:, pl.ds(t, 1), :]
        u_ref[:, pl.ds(t, 1), :] = h
        return h

    h = lax.fori_loop(0, tc, step, h_ref[...], unroll=8)
    h_ref[...] = h
    hn_ref[...] = h
    o_ref[...] = u_ref[...] * jax.nn.gelu(yl_ref[...])


def _lru(pm3, conv_state, h0, cw, cb, wg, ba, bx, lam, *, reset_first, bb, tc):
    b, t, _ = pm3.shape
    lw = lam.shape[-1]
    assert b % bb == 0 and t % tc == 0 and tc % 8 == 0
    row = lambda v: v.reshape(1, lw)
    kern = functools.partial(_lru_kernel, reset_first=reset_first)
    out, hn = pl.pallas_call(
        kern,
        out_shape=(jax.ShapeDtypeStruct((b, t, lw), F32), jax.ShapeDtypeStruct((b, 1, lw), F32)),
        grid=(b // bb, t // tc),
        in_specs=[pl.BlockSpec((bb, tc, lw), lambda i, j: (i, j, 0)),
                  pl.BlockSpec((bb, tc, lw), lambda i, j: (i, j, 1)),
                  pl.BlockSpec((bb, CONV_W - 1, lw), lambda i, j: (i, 0, 0)),
                  pl.BlockSpec((bb, 1, lw), lambda i, j: (i, 0, 0)),
                  pl.BlockSpec((CONV_W, lw), lambda i, j: (0, 0)),
                  pl.BlockSpec((1, lw), lambda i, j: (0, 0)),
                  pl.BlockSpec(wg.shape, lambda i, j: (0, 0, 0)),
                  pl.BlockSpec((1, lw), lambda i, j: (0, 0)),
                  pl.BlockSpec((1, lw), lambda i, j: (0, 0)),
                  pl.BlockSpec((1, lw), lambda i, j: (0, 0))],
        out_specs=(pl.BlockSpec((bb, tc, lw), lambda i, j: (i, j, 0)),
                   pl.BlockSpec((bb, 1, lw), lambda i, j: (i, 0, 0))),
        scratch_shapes=[pltpu.VMEM((bb, tc + CONV_PAD, lw), F32),
                        pltpu.VMEM((bb, tc, lw), F32),
                        pltpu.VMEM((bb, tc, lw), F32),
                        pltpu.VMEM((bb, 1, lw), F32)],
        compiler_params=_params("arbitrary", "arbitrary"),
        name="lru",
    )(pm3, pm3, conv_state, h0.reshape(b, 1, lw), cw, row(cb), wg, row(ba), row(bx), row(lam))
    return out, hn.reshape(b, lw)


def _delta_kernel(q_ref, k_ref, v_ref, z_ref, ps_ref, csq_ref, csk_ref, csv_ref, s0_ref,
                  cwq_ref, cwk_ref, cwv_ref, ea_ref, dtb_ref, nw_ref,
                  o_ref, sn_ref,
                  xq_ref, xk_ref, xv_ref, qn_ref, kn_ref, vn_ref, beta_ref, g_ref, *, chunk, nheads):
    ti = pl.program_id(1)
    bb, tb, w = q_ref.shape
    n_chunks = tb // chunk
    rows = bb * chunk
    assert bb == 1 or n_chunks == 1
    lo = CONV_PAD - (CONV_W - 1)

    @pl.when(ti == 0)
    def _():
        xq_ref[:, lo:CONV_PAD, :] = csq_ref[...]
        xk_ref[:, lo:CONV_PAD, :] = csk_ref[...]
        xv_ref[:, lo:CONV_PAD, :] = csv_ref[...]
        sn_ref[...] = s0_ref[...]

    def conv_silu(x_ref, xs_ref, cw_ref):
        y = _causal_conv(x_ref, xs_ref, cw_ref)
        return (y * jax.nn.sigmoid(y)).reshape(bb * tb, w)

    qc = conv_silu(q_ref, xq_ref, cwq_ref)
    kc = conv_silu(k_ref, xk_ref, cwk_ref)
    vn_ref[...] = conv_silu(v_ref, xv_ref, cwv_ref)
    for hd in range(nheads):
        sl = slice(hd * LANES, (hd + 1) * LANES)
        qh = qc[:, sl]
        kh = kc[:, sl]
        qn_ref[:, sl] = qh * (lax.rsqrt(jnp.sum(qh * qh, axis=-1, keepdims=True) + NORM_EPS) * (LANES ** -0.5))
        kn_ref[:, sl] = kh * lax.rsqrt(jnp.sum(kh * kh, axis=-1, keepdims=True) + NORM_EPS)
    ps = ps_ref[...].reshape(bb * tb, LANES)
    beta_ref[...] = jax.nn.sigmoid(ps)
    g_ref[...] = -ea_ref[...] * _softplus(ps + dtb_ref[...])

    ii = lax.broadcasted_iota(jnp.int32, (rows, rows), 0)
    jj = lax.broadcasted_iota(jnp.int32, (rows, rows), 1)
    if bb == 1:
        le = jj <= ii
        lt = jj < ii
    else:
        same = (ii // chunk) == (jj // chunk)
        le = jnp.logical_and(same, jj <= ii)
        lt = jnp.logical_and(same, jj < ii)
    le_b = jnp.where(le, 1.0, 0.0).astype(BF16)
    assert chunk & (chunk - 1) == 0 and chunk >= 2
    levels = []
    b = 1
    while b < chunk:
        sh = b.bit_length()
        join = jnp.logical_and((ii >> sh) == (jj >> sh), (ii >> (sh - 1)) != (jj >> (sh - 1)))
        levels.append(jnp.logical_and(join, lt))
        b *= 2
    heads = range(nheads)
    lanes = [slice(hd * LANES, (hd + 1) * LANES) for hd in heads]

    def mm(a, b):
        return jnp.dot(a.astype(BF16), b.astype(BF16), preferred_element_type=F32)

    def do_chunk(c, carry):
        r0 = pl.multiple_of(c * rows, rows)
        rsl = pl.ds(r0, rows)
        g = g_ref[rsl, :]
        beta = beta_ref[rsl, :]
        g_hi = g.astype(BF16)
        g_lo = (g - g_hi.astype(F32)).astype(BF16)
        gc = (jnp.dot(le_b, g_hi, preferred_element_type=F32)
              + jnp.dot(le_b, g_lo, preferred_element_type=F32))
        gcp = gc if rows == LANES else jnp.concatenate([gc, jnp.zeros((LANES - rows, LANES), F32)], axis=0)
        gct = gcp.T
        eg = jnp.exp(gc)
        q = [qn_ref[rsl, sl] for sl in lanes]
        k = [kn_ref[rsl, sl] for sl in lanes]
        v = [vn_ref[rsl, sl] for sl in lanes]
        bcol = [beta[:, hd:hd + 1] for hd in heads]
        gcol = [gc[:, nheads + hd:nheads + hd + 1] for hd in heads]
        egc = [eg[:, nheads + hd:nheads + hd + 1] for hd in heads]
        decay = [jnp.where(le, jnp.exp(jnp.where(le, gcol[hd] - gct[nheads + hd:nheads + hd + 1, :rows], 0.0)), 0.0)
                 for hd in heads]
        kb = [k[hd] * bcol[hd] for hd in heads]
        k16 = [k[hd].astype(BF16) for hd in heads]
        kk = [lax.dot_general(kb[hd].astype(BF16), k16[hd], NT_DIMS, preferred_element_type=F32) for hd in heads]
        qk = [lax.dot_general(q[hd].astype(BF16), k16[hd], NT_DIMS, preferred_element_type=F32) for hd in heads]
        lmat = [jnp.where(lt, kk[hd] * decay[hd], 0.0) for hd in heads]
        attn = [jnp.where(le, qk[hd] * decay[hd], 0.0).astype(BF16) for hd in heads]
        nmat = [jnp.where(levels[0], -lmat[hd], 0.0) for hd in heads]
        for lvl in levels[1:]:
            off = [jnp.where(lvl, lmat[hd], 0.0) for hd in heads]
            m = [off[hd] + mm(nmat[hd], off[hd]) for hd in heads]
            nmat = [nmat[hd] - m[hd] - mm(m[hd], nmat[hd]) for hd in heads]
        rhs = [jnp.concatenate([v[hd] * bcol[hd], kb[hd] * egc[hd]], axis=1) for hd in heads]
        sol = [rhs[hd] + mm(nmat[hd], rhs[hd]) for hd in heads]
        qe = [q[hd] * egc[hd] for hd in heads]
        s_old = [[sn_ref[b, hd] for b in range(bb)] for hd in heads]
        both = [[jnp.dot(jnp.concatenate([sol[hd][b * chunk:(b + 1) * chunk, LANES:],
                                          qe[hd][b * chunk:(b + 1) * chunk]], axis=0).astype(BF16),
                         s_old[hd][b].astype(BF16), preferred_element_type=F32) for b in range(bb)] for hd in heads]
        cat = lambda parts: parts[0] if bb == 1 else jnp.concatenate(parts, axis=0)
        v16 = [(sol[hd][:, :LANES] - cat([both[hd][b][:chunk] for b in range(bb)])).astype(BF16) for hd in heads]
        o = [cat([both[hd][b][chunk:] for b in range(bb)]) + jnp.dot(attn[hd], v16[hd], preferred_element_type=F32)
             for hd in heads]
        for hd in heads:
            for b in range(bb):
                seg = slice(b * chunk, (b + 1) * chunk)
                g_last = gcol[hd][(b + 1) * chunk - 1:(b + 1) * chunk]
                kd = (k[hd][seg] * jnp.exp(g_last - gcol[hd][seg])).astype(BF16)
                sn_ref[b, hd] = s_old[hd][b] * jnp.exp(g_last) + lax.dot_general(
                    kd, v16[hd][seg], TN_DIMS, preferred_element_type=F32)
        for hd in heads:
            sl = lanes[hd]
            on = o[hd] * lax.rsqrt(jnp.mean(o[hd] * o[hd], axis=-1, keepdims=True) + NORM_EPS) * nw_ref[...]
            if bb == 1:
                zz = z_ref[0, rsl, sl]
                o_ref[0, rsl, sl] = on * (zz * jax.nn.sigmoid(zz))
            else:
                zz = z_ref[:, :, sl].reshape(rows, LANES)
                o_ref[:, :, sl] = (on * (zz * jax.nn.sigmoid(zz))).reshape(bb, chunk, LANES)
        return carry

    if n_chunks == 1:
        do_chunk(0, 0)
    else:
        lax.fori_loop(0, n_chunks, do_chunk, 0)


def _delta(pm3, ps3, conv_state, s0, cw, a_log, dt_bias, norm_w, *, bb, tb, chunk):
    b, t, _ = pm3.shape
    nheads = a_log.shape[-1]
    w = nheads * LANES
    assert s0.shape == (b, nheads, LANES, LANES) and nheads <= 8
    assert b % bb == 0 and t % tb == 0 and tb % chunk == 0 and (bb * chunk) % 8 == 0 and bb * chunk <= LANES
    pad = lambda v: jnp.zeros((1, LANES), F32).at[0, nheads:2 * nheads].set(v)
    kern = functools.partial(_delta_kernel, chunk=chunk, nheads=nheads)
    col = lambda c: pl.BlockSpec((bb, tb, w), lambda i, j, c=c: (i, j, c))
    cst = lambda c: pl.BlockSpec((bb, CONV_W - 1, w), lambda i, j, c=c: (i, 0, c))
    cwt = lambda c: pl.BlockSpec((CONV_W, w), lambda i, j, c=c: (0, c))
    vec = pl.BlockSpec((1, LANES), lambda i, j: (0, 0))
    sblk = pl.BlockSpec((bb, nheads, LANES, LANES), lambda i, j: (i, 0, 0, 0))
    out, sn = pl.pallas_call(
        kern,
        out_shape=(jax.ShapeDtypeStruct((b, t, w), F32), jax.ShapeDtypeStruct(s0.shape, F32)),
        grid=(b // bb, t // tb),
        in_specs=[col(2), col(3), col(4), col(5),
                  pl.BlockSpec((bb, tb, LANES), lambda i, j: (i, j, 0)),
                  cst(0), cst(1), cst(2), sblk, cwt(0), cwt(1), cwt(2), vec, vec, vec],
        out_specs=(pl.BlockSpec((bb, tb, w), lambda i, j: (i, j, 0)), sblk),
        scratch_shapes=[pltpu.VMEM((bb, tb + CONV_PAD, w), F32)] * 3
                       + [pltpu.VMEM((bb * tb, w), F32)] * 3
                       + [pltpu.VMEM((bb * tb, LANES), F32)] * 2,
        compiler_params=_params("arbitrary", "arbitrary"),
        name="delta",
    )(pm3, pm3, pm3, pm3, ps3, conv_state, conv_state, conv_state, s0, cw, cw, cw,
      pad(jnp.exp(a_log.astype(F32))), pad(dt_bias.astype(F32)), norm_w.reshape(1, LANES))
    return out, sn


def _outproj_kernel(ol_ref, od_ref, x_ref, g1_ref, sc_ref, sh_ref, nw_ref, wt_ref, wb_ref, x1_ref, h2_ref):
    tm = h2_ref.shape[0]
    a = ol_ref[...].reshape(tm, ol_ref.shape[-1]).astype(BF16)
    b = od_ref[...].reshape(tm, od_ref.shape[-1]).astype(BF16)
    mix = (jnp.dot(a, wt_ref[...], preferred_element_type=F32)
           + jnp.dot(b, wb_ref[...], preferred_element_type=F32))
    x1 = x_ref[...] + g1_ref[...] * mix.reshape(x_ref.shape)
    x1_ref[...] = x1
    y = x1 * lax.rsqrt(jnp.mean(x1 * x1, axis=-1, keepdims=True) + NORM_EPS) * nw_ref[...]
    h2_ref[...] = (y * (1.0 + sc_ref[...]) + sh_ref[...]).reshape(h2_ref.shape).astype(BF16)


def _outproj(ol, od, x, g1, sc, sh, nw, wt, wb):
    b, t, d = x.shape
    n = b * t
    lw, dw = ol.shape[-1], od.shape[-1]
    tm = _tile(n, 512, 16)
    gb, r, nt = _row_groups(b, t, tm)
    tok = lambda c: pl.BlockSpec((gb, r, c), lambda i: (i // nt, i % nt, 0))
    per = pl.BlockSpec((gb, 1, d), lambda i: (i // nt, 0, 0))
    return pl.pallas_call(
        _outproj_kernel,
        out_shape=(jax.ShapeDtypeStruct((b, t, d), F32), jax.ShapeDtypeStruct((n, d), BF16)),
        grid=(n // tm,),
        in_specs=[tok(lw), tok(dw), tok(d), per, per, per,
                  pl.BlockSpec((1, 1, d), lambda i: (0, 0, 0)),
                  pl.BlockSpec((lw, d), lambda i: (0, 0)),
                  pl.BlockSpec((dw, d), lambda i: (0, 0))],
        out_specs=(tok(d), pl.BlockSpec((tm, d), lambda i: (i, 0))),
        compiler_params=_params("arbitrary"),
        name="outproj",
    )(ol, od, x, g1, sc, sh, nw, wt, wb)


def _sorting_network(n):
    pairs = []
    p = 1
    while p < n:
        k = p
        while k >= 1:
            for j in range(k % p, n - k, 2 * k):
                for i in range(min(k, n - j - k)):
                    if (i + j) // (2 * p) == (i + j + k) // (2 * p):
                        pairs.append((i + j, i + j + k))
            k //= 2
        p *= 2
    return pairs


def _merge_top(lists, singles, count):
    lists = list(lists)
    singles = list(singles)
    vals = []
    for r in range(count):
        head = lists[0]
        for x in singles:
            head = jnp.maximum(head, x)
        m = jnp.max(head, axis=0, keepdims=True)
        vals.append(m)
        left = count - r - 1
        if left == 0:
            break
        eq = lists[0] == m
        keep = min(len(lists), left)
        lists = [jnp.where(eq, lists[v + 1] if v + 1 < len(lists) else -jnp.inf, lists[v]) for v in range(keep)]
        singles = [jnp.where(x == m, -jnp.inf, x) for x in singles]
    return vals


def _top_values(s, count):
    tiles = [s[8 * v:8 * (v + 1)] for v in range(s.shape[0] // 8)]
    for i, j in _sorting_network(len(tiles)):
        tiles[i], tiles[j] = jnp.maximum(tiles[i], tiles[j]), jnp.minimum(tiles[i], tiles[j])
    return _merge_top(tiles, [], count)


def _router_kernel(h_ref, wq_ref, sk_ref, s2_ref, thr_ref, r_ref, q_ref, *, nheads):
    nk = sk_ref.shape[1]
    half = sk_ref.shape[2]
    q_ref[...] = lax.dot_general(wq_ref[...], h_ref[...], NT_DIMS, preferred_element_type=F32)
    p = h_ref.shape[0]
    sub = lax.broadcasted_iota(jnp.int32, (8, p), 0)

    def head(hd, carry):
        base = pl.multiple_of(hd * (2 * half), 2 * half)
        q1 = q_ref[pl.ds(base, half), :].astype(BF16)
        q2 = q_ref[pl.ds(base + half, half), :].astype(BF16)
        s1 = jnp.dot(sk_ref[2 * hd], q1, preferred_element_type=F32)
        s2 = jnp.dot(sk_ref[2 * hd + 1], q2, preferred_element_type=F32)
        nv = PEER_TOPK + 1
        t1 = _top_values(s1, nv)
        t2 = _top_values(s2, nv)
        t2_lo = jnp.concatenate(t2[:8], axis=0)
        t2_hi = jnp.concatenate(t2[8:16], axis=0)
        lists = [t1[0] + t2_lo]
        for a in range(1, nv):
            nb = nv // (a + 1)
            c = t1[a] + t2_lo
            lists.append(c if nb >= 8 else jnp.where(sub < nb, c, -jnp.inf))
        singles = [t1[0] + t2_hi, jnp.where(sub < 1, t1[0] + t2[16], -jnp.inf)]
        cand = jnp.concatenate(lists + singles, axis=0)
        best = _merge_top(lists, singles, nv)
        tau = best[PEER_TOPK - 1]
        mid = 0.5 * (tau + best[PEER_TOPK])
        top = t1[0] + t2[0]
        z = jnp.sum(jnp.where(cand >= tau, jnp.exp(cand - top), 0.0), axis=0, keepdims=True)
        s2_ref[hd] = s2 * LOG2E
        thr_ref[hd] = (mid - s1) * LOG2E
        r_ref[hd] = (s1 - (top + jnp.log(z))) * LOG2E
        return carry

    lax.fori_loop(0, nheads, head, 0)


def _router(h2, wq_t, sk, *, tp):
    n, d = h2.shape
    nheads = sk.shape[0] // 2
    nk = sk.shape[1]
    hq = wq_t.shape[0]
    out = jax.ShapeDtypeStruct((nheads, nk, n), F32)
    oblk = pl.BlockSpec((nheads, nk, tp), lambda i: (0, 0, i))
    return pl.pallas_call(
        functools.partial(_router_kernel, nheads=nheads),
        out_shape=(out, out, out),
        grid=(n // tp,),
        in_specs=[pl.BlockSpec((tp, d), lambda i: (i, 0)),
                  pl.BlockSpec((hq, d), lambda i: (0, 0)),
                  pl.BlockSpec(sk.shape, lambda i: (0, 0, 0))],
        out_specs=(oblk, oblk, oblk),
        scratch_shapes=[pltpu.VMEM((hq, tp), F32)],
        compiler_params=_params("arbitrary"),
        name="router",
    )(h2, wq_t, sk)


def _dense_kernel(h_ref, u_ref, vt_ref, s2_ref, thr_ref, r_ref, x1_ref, g2_ref, fw_ref,
                  y_ref, acc_ref, st_ref, coef_ref, *, nheads, rows_per_step, n_chunks):
    s = pl.program_id(1)
    nk = s2_ref.shape[1]
    tp = h_ref.shape[0]

    @pl.when(s == 0)
    def _():
        acc_ref[...] = jnp.zeros_like(acc_ref)
        st_ref[...] = jnp.zeros_like(st_ref)
        coef_ref[...] = jnp.zeros_like(coef_ref)

    cur = s % 2
    prev = 1 - cur
    acc_ref[...] += jnp.dot(vt_ref[...], coef_ref[cur], preferred_element_type=F32)
    chunk = jnp.clip(s - 1, 0, n_chunks - 1)
    for ii in range(rows_per_step):
        i1 = chunk * rows_per_step + ii
        rows = slice(ii * nk, (ii + 1) * nk)
        thr_rows = [thr_ref[hd, pl.ds(i1, 1), :] for hd in range(nheads)]
        r_rows = [r_ref[hd, pl.ds(i1, 1), :] for hd in range(nheads)]
        for pb in range(tp // LANES):
            cols = slice(pb * LANES, (pb + 1) * LANES)
            gate = None
            for hd in range(nheads):
                s2 = s2_ref[hd, :, cols]
                term = jnp.where(s2 >= thr_rows[hd][:, cols], jnp.exp2(s2 + r_rows[hd][:, cols]), 0.0)
                gate = term if gate is None else gate + term
            coef_ref[prev, rows, cols] = (jax.nn.gelu(st_ref[prev, rows, cols]) * gate).astype(BF16)
    st_ref[cur] = lax.dot_general(u_ref[...], h_ref[...], NT_DIMS, preferred_element_type=F32)

    @pl.when(s == pl.num_programs(1) - 1)
    def _():
        peer = acc_ref[...].T.reshape(x1_ref.shape)
        x2 = x1_ref[...] + g2_ref[...] * peer
        y_ref[...] = x2 * lax.rsqrt(jnp.mean(x2 * x2, axis=-1, keepdims=True) + NORM_EPS) * fw_ref[...]


def _dense(h2, u16, vt16, s2t, thr, rl, x1, g2, fw, *, tp, rows_per_step):
    b, t, d = x1.shape
    n = b * t
    nheads, nk, _ = s2t.shape
    ne = u16.shape[0]
    ce = rows_per_step * nk
    n_chunks = ne // ce
    gb, r, nt = _row_groups(b, t, tp)
    rblk = pl.BlockSpec((nheads, nk, tp), lambda i, s: (0, 0, i))
    tok = pl.BlockSpec((gb, r, d), lambda i, s: (i // nt, i % nt, 0))
    return pl.pallas_call(
        functools.partial(_dense_kernel, nheads=nheads, rows_per_step=rows_per_step, n_chunks=n_chunks),
        out_shape=jax.ShapeDtypeStruct((b, t, d), F32),
        grid=(n // tp, n_chunks + 2),
        in_specs=[pl.BlockSpec((tp, d), lambda i, s: (i, 0)),
                  pl.BlockSpec((ce, d), lambda i, s: (jnp.minimum(s, n_chunks - 1), 0)),
                  pl.BlockSpec((d, ce), lambda i, s: (0, jnp.clip(s - 2, 0, n_chunks - 1))),
                  rblk, rblk, rblk, tok,
                  pl.BlockSpec((gb, 1, d), lambda i, s: (i // nt, 0, 0)),
                  pl.BlockSpec((1, 1, d), lambda i, s: (0, 0, 0))],
        out_specs=tok,
        scratch_shapes=[pltpu.VMEM((d, tp), F32), pltpu.VMEM((2, ce, tp), F32), pltpu.VMEM((2, ce, tp), BF16)],
        compiler_params=_params("arbitrary", "arbitrary"),
        name="dense",
    )(h2, u16, vt16, s2t, thr, rl, x1, g2, fw)


def _group(x, mod, lru_conv, lru_h, dn_conv, dn_s, reset_first, p, fw, *, lru_tiles, dn_tiles, tp):
    b, t, d = x.shape
    sh1, sc1, g1, sh2, sc2, g2 = [m.reshape(b, 1, d) for m in jnp.split(mod, 6, axis=-1)]
    pm, ps = _inproj(x, sc1, sh1, p['norm_mix_w'], p['w_main'], p['w_small'])
    pm3 = pm.reshape(b, t, -1)
    ps3 = ps.reshape(b, t, LANES)
    lw = p['lru_lambda'].shape[-1]
    out_lru, new_h = _lru(pm3, lru_conv, lru_h, p['lru_conv_w'], p['lru_conv_b'], p['lru_wg'], p['lru_ba'],
                          p['lru_bx'], p['lru_lambda'], reset_first=reset_first, bb=lru_tiles[0], tc=lru_tiles[1])
    out_dn, new_s = _delta(pm3, ps3, dn_conv, dn_s, p['dn_conv_w'], p['dn_A_log'], p['dn_dt_bias'], p['dn_norm_w'],
                           bb=dn_tiles[0], tb=dn_tiles[1], chunk=dn_tiles[2])
    keep = CONV_W - 1
    new_lru_conv = pm3[:, t - keep:, :lw]
    new_dn_conv = pm3[:, t - keep:, 2 * lw:2 * lw + dn_conv.shape[-1]]
    x1, h2 = _outproj(out_lru, out_dn, x, g1, sc2, sh2, p['norm_ffn_w'], p['w_out_top'], p['w_out_bot'])
    s2t, thr, r = _router(h2, p['wq_t'], p['subkeys'], tp=tp)
    y = _dense(h2, p['peer_u'], p['peer_vt'], s2t, thr, r, x1, g2, fw, tp=tp, rows_per_step=4)
    return y, (new_lru_conv, new_h, new_dn_conv, new_s)


def kernel(x_prompt, x_sample, state_lru_conv, state_lru_h, state_dn_conv, state_dn_S, c_prompt, c_sample,
           w_ada, b_ada, norm_mix_w, norm_ffn_w, w_in, lru_conv_w, lru_conv_b, lru_wa, lru_ba, lru_wx, lru_bx,
           lru_lambda, dn_conv_w, dn_A_log, dn_dt_bias, dn_norm_w, w_out, peer_wq, peer_subkeys, peer_u, peer_v,
           final_norm_w):
    depth = w_ada.shape[0]
    assert depth == 1, "the final norm is fused into the last layer's expert kernel"
    bp, seq, d = x_prompt.shape
    bs, dseq, _ = x_sample.shape
    lw = lru_lambda.shape[-1]
    dn_heads = dn_A_log.shape[-1]
    dn_w = dn_heads * LANES
    conv_ch = dn_conv_w.shape[-1]
    assert lru_wa.shape[-1] == LANES and lw == dn_w and conv_ch == 3 * dn_w
    assert peer_subkeys.shape[-1] == LANES and peer_subkeys.shape[-2] == LANES
    main = 2 * lw + conv_ch + dn_w
    l = 0
    w_small = jnp.zeros((d, LANES), F32).at[:, :2 * dn_heads].set(w_in[l][:, main:]).astype(BF16)
    nheads = peer_subkeys.shape[1]
    p = {
        'norm_mix_w': norm_mix_w[l].reshape(1, 1, d), 'norm_ffn_w': norm_ffn_w[l].reshape(1, 1, d),
        'w_main': w_in[l][:, :main].astype(BF16), 'w_small': w_small,
        'lru_conv_w': lru_conv_w[l], 'lru_conv_b': lru_conv_b[l],
        'lru_wg': jnp.concatenate([lru_wa[l], lru_wx[l]], axis=-1).astype(BF16),
        'lru_ba': lru_ba[l], 'lru_bx': lru_bx[l], 'lru_lambda': lru_lambda[l],
        'dn_conv_w': dn_conv_w[l], 'dn_A_log': dn_A_log[l], 'dn_dt_bias': dn_dt_bias[l], 'dn_norm_w': dn_norm_w[l],
        'w_out_top': w_out[l][:lw].astype(BF16), 'w_out_bot': w_out[l][lw:].astype(BF16),
        'wq_t': peer_wq[l].T.astype(BF16),
        'subkeys': peer_subkeys[l].reshape(nheads * 2, LANES, LANES).astype(BF16),
        'peer_u': peer_u[l].astype(BF16), 'peer_vt': peer_v[l].T.astype(BF16),
    }
    fw = final_norm_w.reshape(1, 1, d)
    mod = _ada(jnp.concatenate([c_prompt, c_sample], axis=0), w_ada[l], b_ada[l])
    zeros = lambda *s: jnp.zeros(s, F32)
    yp, sp = _group(x_prompt, mod[:bp], zeros(bp, CONV_W - 1, lw), zeros(bp, lw), zeros(bp, CONV_W - 1, conv_ch),
                    zeros(bp, dn_heads, LANES, LANES), True, p, fw,
                    lru_tiles=(bp, _tile(seq, 128, 8)), dn_tiles=(1, _tile(seq, 256, DN_CHUNK), min(DN_CHUNK, seq)),
                    tp=_tile(bp * seq, 512, LANES))
    ys, ss = _group(x_sample, mod[bp:], state_lru_conv[l], state_lru_h[l], state_dn_conv[l], state_dn_S[l],
                    False, p, fw,
                    lru_tiles=(_tile(bs, 32, 1), dseq), dn_tiles=(_tile(bs, DN_CHUNK // dseq, 1), dseq, dseq),
                    tp=_tile(bs * dseq, 512, LANES))
    stack = lambda v: v[None]
    return (yp, ys, stack(sp[0]), stack(sp[1]), stack(sp[2]), stack(sp[3]),
            stack(ss[0]), stack(ss[1]), stack(ss[2]), stack(ss[3]))
```

```python
import functools
import math

import jax
import jax.numpy as jnp
from jax import lax
from jax.experimental import pallas as pl
from jax.experimental.pallas import tpu as pltpu

F32 = jnp.float32
BF16 = jnp.bfloat16
NORM_EPS = 1e-6
LRU_C = 8.0
CONV_W = 4
CONV_PAD = 8
PEER_TOPK = 16
PEER_CHUNK = 512
DN_CHUNK = 64
LOG2E = math.log2(math.e)
LANES = 128
VMEM_LIMIT_BYTES = 60 * 1024 * 1024
NT_DIMS = (((1,), (1,)), ((), ()))
TN_DIMS = (((0,), (0,)), ((), ()))


def _params(*semantics):
    return pltpu.CompilerParams(dimension_semantics=semantics, vmem_limit_bytes=VMEM_LIMIT_BYTES)


def _tile(n, target, multiple):
    if n <= target:
        return n
    t = (target // multiple) * multiple
    while t >= multiple:
        if n % t == 0:
            return t
        t -= multiple
    raise ValueError(f"no tile for {n} <= {target} in multiples of {multiple}")


def _softplus(x):
    return jnp.maximum(x, 0.0) + jnp.log1p(jnp.exp(-jnp.abs(x)))


def _row_groups(batch, seq, tm):
    r = min(seq, tm)
    assert seq % r == 0 and tm % r == 0 and (batch * seq) % tm == 0
    return tm // r, r, seq // r


def _ada_kernel(c_ref, w_ref, b_ref, o_ref):
    c = c_ref[...]
    a = (c * jax.nn.sigmoid(c)).astype(BF16)
    o_ref[...] = jnp.dot(a, w_ref[...].astype(BF16), preferred_element_type=F32) + b_ref[...]


def _ada(c, w, b):
    bc, d = c.shape
    n = w.shape[1]
    tn = _tile(n, 1024, LANES)
    return pl.pallas_call(
        _ada_kernel,
        out_shape=jax.ShapeDtypeStruct((bc, n), F32),
        grid=(n // tn,),
        in_specs=[pl.BlockSpec((bc, d), lambda j: (0, 0)),
                  pl.BlockSpec((d, tn), lambda j: (0, j)),
                  pl.BlockSpec((1, tn), lambda j: (0, j))],
        out_specs=pl.BlockSpec((bc, tn), lambda j: (0, j)),
        compiler_params=_params("arbitrary"),
        name="ada",
    )(c, w, b.reshape(1, n))


def _inproj_kernel(x_ref, sc_ref, sh_ref, nw_ref, wm_ref, ws_ref, om_ref, os_ref, h_ref):
    @pl.when(pl.program_id(1) == 0)
    def _():
        x = x_ref[...]
        y = x * lax.rsqrt(jnp.mean(x * x, axis=-1, keepdims=True) + NORM_EPS) * nw_ref[...]
        h = y * (1.0 + sc_ref[...]) + sh_ref[...]
        hb = h.reshape(h_ref.shape).astype(BF16)
        h_ref[...] = hb
        os_ref[...] = jnp.dot(hb, ws_ref[...], preferred_element_type=F32)

    om_ref[...] = jnp.dot(h_ref[...], wm_ref[...], preferred_element_type=F32)


def _inproj(x, sc, sh, nw, wm, ws):
    b, t, d = x.shape
    n = b * t
    m = wm.shape[1]
    tm = _tile(n, 512, 8)
    tn = _tile(m, 3072, LANES)
    gb, r, nt = _row_groups(b, t, tm)
    return pl.pallas_call(
        _inproj_kernel,
        out_shape=(jax.ShapeDtypeStruct((n, m), F32), jax.ShapeDtypeStruct((n, LANES), F32)),
        grid=(n // tm, m // tn),
        in_specs=[pl.BlockSpec((gb, r, d), lambda i, j: (i // nt, i % nt, 0)),
                  pl.BlockSpec((gb, 1, d), lambda i, j: (i // nt, 0, 0)),
                  pl.BlockSpec((gb, 1, d), lambda i, j: (i // nt, 0, 0)),
                  pl.BlockSpec((1, 1, d), lambda i, j: (0, 0, 0)),
                  pl.BlockSpec((d, tn), lambda i, j: (0, j)),
                  pl.BlockSpec((d, LANES), lambda i, j: (0, 0))],
        out_specs=(pl.BlockSpec((tm, tn), lambda i, j: (i, j)),
                   pl.BlockSpec((tm, LANES), lambda i, j: (i, 0))),
        scratch_shapes=[pltpu.VMEM((tm, d), BF16)],
        compiler_params=_params("arbitrary", "arbitrary"),
        name="inproj",
    )(x, sc, sh, nw, wm, ws)


def _causal_conv(x_ref, xs_ref, cw_ref):
    t = x_ref.shape[1]
    xs_ref[:, CONV_PAD:CONV_PAD + t, :] = x_ref[...]
    cw = cw_ref[...]
    y = None
    for k in range(CONV_W):
        lo = CONV_PAD - (CONV_W - 1) + k
        term = xs_ref[:, lo:lo + t, :] * cw[k:k + 1, :][None]
        y = term if y is None else y + term
    xs_ref[:, CONV_PAD - (CONV_W - 1):CONV_PAD, :] = xs_ref[:, t + CONV_PAD - (CONV_W - 1):t + CONV_PAD, :]
    return y


def _lru_kernel(xl_ref, yl_ref, cs_ref, h0_ref, cw_ref, cb_ref, wg_ref, ba_ref, bx_ref, lam_ref,
                o_ref, hn_ref, xs_ref, a_ref, u_ref, h_ref, *, reset_first):
    ti = pl.program_id(1)
    bb, tc, lw = xl_ref.shape
    nheads = lw // LANES

    @pl.when(ti == 0)
    def _():
        xs_ref[:, CONV_PAD - (CONV_W - 1):CONV_PAD, :] = cs_ref[...]
        h_ref[...] = h0_ref[...]

    xc = (_causal_conv(xl_ref, xs_ref, cw_ref) + cb_ref[...][None]).reshape(bb * tc, lw)
    sp = _softplus(-lam_ref[...])
    t_glob = lax.broadcasted_iota(jnp.int32, (bb, tc, LANES), 1) + ti * tc
    for hd in range(nheads):
        sl = slice(hd * LANES, (hd + 1) * LANES)
        xh = xc[:, sl]
        gates = jnp.dot(xh.astype(BF16), wg_ref[hd], preferred_element_type=F32)
        r = jax.nn.sigmoid(gates[:, :LANES] + ba_ref[:, sl])
        i = jax.nn.sigmoid(gates[:, LANES:] + bx_ref[:, sl])
        log_a = -LRU_C * r * sp[:, sl]
        a = jnp.exp(log_a)
        mult = jnp.sqrt(jnp.maximum(-jnp.tanh(log_a) * (a * a + 1.0), 0.0))
        u = i * xh
        a3 = a.reshape(bb, tc, LANES)
        m3 = mult.reshape(bb, tc, LANES)
        if reset_first:
            m3 = jnp.where(t_glob == 0, 1.0, m3)
        a_ref[:, :, sl] = a3
        u_ref[:, :, sl] = m3 * u.reshape(bb, tc, LANES)

    def step(t, h):
        h = a_ref[:, pl.ds(t, 1), :] * h + u_ref[:, pl.ds(t, 1), :]
        u_ref[:, pl.ds(t, 1), :] = h
        return h

    h = lax.fori_loop(0, tc, step, h_ref[...], unroll=8)
    h_ref[...] = h
    hn_ref[...] = h
    o_ref[...] = u_ref[...] * jax.nn.gelu(yl_ref[...])


def _lru(pm3, conv_state, h0, cw, cb, wg, ba, bx, lam, *, reset_first, bb, tc):
    b, t, _ = pm3.shape
    lw = lam.shape[-1]
    assert b % bb == 0 and t % tc == 0 and tc % 8 == 0
    row = lambda v: v.reshape(1, lw)
    kern = functools.partial(_lru_kernel, reset_first=reset_first)
    out, hn = pl.pallas_call(
        kern,
        out_shape=(jax.ShapeDtypeStruct((b, t, lw), F32), jax.ShapeDtypeStruct((b, 1, lw), F32)),
        grid=(b // bb, t // tc),
        in_specs=[pl.BlockSpec((bb, tc, lw), lambda i, j: (i, j, 0)),
                  pl.BlockSpec((bb, tc, lw), lambda i, j: (i, j, 1)),
                  pl.BlockSpec((bb, CONV_W - 1, lw), lambda i, j: (i, 0, 0)),
                  pl.BlockSpec((bb, 1, lw), lambda i, j: (i, 0, 0)),
                  pl.BlockSpec((CONV_W, lw), lambda i, j: (0, 0)),
                  pl.BlockSpec((1, lw), lambda i, j: (0, 0)),
                  pl.BlockSpec(wg.shape, lambda i, j: (0, 0, 0)),
                  pl.BlockSpec((1, lw), lambda i, j: (0, 0)),
                  pl.BlockSpec((1, lw), lambda i, j: (0, 0)),
                  pl.BlockSpec((1, lw), lambda i, j: (0, 0))],
        out_specs=(pl.BlockSpec((bb, tc, lw), lambda i, j: (i, j, 0)),
                   pl.BlockSpec((bb, 1, lw), lambda i, j: (i, 0, 0))),
        scratch_shapes=[pltpu.VMEM((bb, tc + CONV_PAD, lw), F32),
                        pltpu.VMEM((bb, tc, lw), F32),
                        pltpu.VMEM((bb, tc, lw), F32),
                        pltpu.VMEM((bb, 1, lw), F32)],
        compiler_params=_params("arbitrary", "arbitrary"),
        name="lru",
    )(pm3, pm3, conv_state, h0.reshape(b, 1, lw), cw, row(cb), wg, row(ba), row(bx), row(lam))
    return out, hn.reshape(b, lw)


def _delta_kernel(q_ref, k_ref, v_ref, z_ref, ps_ref, csq_ref, csk_ref, csv_ref, s0_ref,
                  cwq_ref, cwk_ref, cwv_ref, ea_ref, dtb_ref, nw_ref,
                  o_ref, sn_ref,
                  xq_ref, xk_ref, xv_ref, qn_ref, kn_ref, vn_ref, beta_ref, g_ref, *, chunk, nheads):
    ti = pl.program_id(1)
    bb, tb, w = q_ref.shape
    n_chunks = tb // chunk
    rows = bb * chunk
    assert bb == 1 or n_chunks == 1
    lo = CONV_PAD - (CONV_W - 1)

    @pl.when(ti == 0)
    def _():
        xq_ref[:, lo:CONV_PAD, :] = csq_ref[...]
        xk_ref[:, lo:CONV_PAD, :] = csk_ref[...]
        xv_ref[:, lo:CONV_PAD, :] = csv_ref[...]
        sn_ref[...] = s0_ref[...]

    def conv_silu(x_ref, xs_ref, cw_ref):
        y = _causal_conv(x_ref, xs_ref, cw_ref)
        return (y * jax.nn.sigmoid(y)).reshape(bb * tb, w)

    qc = conv_silu(q_ref, xq_ref, cwq_ref)
    kc = conv_silu(k_ref, xk_ref, cwk_ref)
    vn_ref[...] = conv_silu(v_ref, xv_ref, cwv_ref)
    for hd in range(nheads):
        sl = slice(hd * LANES, (hd + 1) * LANES)
        qh = qc[:, sl]
        kh = kc[:, sl]
        qn_ref[:, sl] = qh * (lax.rsqrt(jnp.sum(qh * qh, axis=-1, keepdims=True) + NORM_EPS) * (LANES ** -0.5))
        kn_ref[:, sl] = kh * lax.rsqrt(jnp.sum(kh * kh, axis=-1, keepdims=True) + NORM_EPS)
    ps = ps_ref[...].reshape(bb * tb, LANES)
    beta_ref[...] = jax.nn.sigmoid(ps)
    g_ref[...] = -ea_ref[...] * _softplus(ps + dtb_ref[...])

    ii = lax.broadcasted_iota(jnp.int32, (rows, rows), 0)
    jj = lax.broadcasted_iota(jnp.int32, (rows, rows), 1)
    if bb == 1:
        le = jj <= ii
        lt = jj < ii
    else:
        same = (ii // chunk) == (jj // chunk)
        le = jnp.logical_and(same, jj <= ii)
        lt = jnp.logical_and(same, jj < ii)
    le_b = jnp.where(le, 1.0, 0.0).astype(BF16)
    assert chunk & (chunk - 1) == 0 and chunk >= 2
    levels = []
    b = 1
    while b < chunk:
        sh = b.bit_length()
        join = jnp.logical_and((ii >> sh) == (jj >> sh), (ii >> (sh - 1)) != (jj >> (sh - 1)))
        levels.append(jnp.logical_and(join, lt))
        b *= 2
    heads = range(nheads)
    lanes = [slice(hd * LANES, (hd + 1) * LANES) for hd in heads]

    def mm(a, b):
        return jnp.dot(a.astype(BF16), b.astype(BF16), preferred_element_type=F32)

    def do_chunk(c, carry):
        r0 = pl.multiple_of(c * rows, rows)
        rsl = pl.ds(r0, rows)
        g = g_ref[rsl, :]
        beta = beta_ref[rsl, :]
        g_hi = g.astype(BF16)
        g_lo = (g - g_hi.astype(F32)).astype(BF16)
        gc = (jnp.dot(le_b, g_hi, preferred_element_type=F32)
              + jnp.dot(le_b, g_lo, preferred_element_type=F32))
        gcp = gc if rows == LANES else jnp.concatenate([gc, jnp.zeros((LANES - rows, LANES), F32)], axis=0)
        gct = gcp.T
        eg = jnp.exp(gc)
        q = [qn_ref[rsl, sl] for sl in lanes]
        k = [kn_ref[rsl, sl] for sl in lanes]
        v = [vn_ref[rsl, sl] for sl in lanes]
        bcol = [beta[:, hd:hd + 1] for hd in heads]
        gcol = [gc[:, nheads + hd:nheads + hd + 1] for hd in heads]
        egc = [eg[:, nheads + hd:nheads + hd + 1] for hd in heads]
        decay = [jnp.where(le, jnp.exp(jnp.where(le, gcol[hd] - gct[nheads + hd:nheads + hd + 1, :rows], 0.0)), 0.0)
                 for hd in heads]
        kb = [k[hd] * bcol[hd] for hd in heads]
        k16 = [k[hd].astype(BF16) for hd in heads]
        kk = [lax.dot_general(kb[hd].astype(BF16), k16[hd], NT_DIMS, preferred_element_type=F32) for hd in heads]
        qk = [lax.dot_general(q[hd].astype(BF16), k16[hd], NT_DIMS, preferred_element_type=F32) for hd in heads]
        lmat = [jnp.where(lt, kk[hd] * decay[hd], 0.0) for hd in heads]
        attn = [jnp.where(le, qk[hd] * decay[hd], 0.0).astype(BF16) for hd in heads]
        nmat = [jnp.where(levels[0], -lmat[hd], 0.0) for hd in heads]
        for lvl in levels[1:]:
            off = [jnp.where(lvl, lmat[hd], 0.0) for hd in heads]
            m = [off[hd] + mm(nmat[hd], off[hd]) for hd in heads]
            nmat = [nmat[hd] - m[hd] - mm(m[hd], nmat[hd]) for hd in heads]
        rhs = [jnp.concatenate([v[hd] * bcol[hd], kb[hd] * egc[hd]], axis=1) for hd in heads]
        sol = [rhs[hd] + mm(nmat[hd], rhs[hd]) for hd in heads]
        qe = [q[hd] * egc[hd] for hd in heads]
        s_old = [[sn_ref[b, hd] for b in range(bb)] for hd in heads]
        both = [[jnp.dot(jnp.concatenate([sol[hd][b * chunk:(b + 1) * chunk, LANES:],
                                          qe[hd][b * chunk:(b + 1) * chunk]], axis=0).astype(BF16),
                         s_old[hd][b].astype(BF16), preferred_element_type=F32) for b in range(bb)] for hd in heads]
        cat = lambda parts: parts[0] if bb == 1 else jnp.concatenate(parts, axis=0)
        v16 = [(sol[hd][:, :LANES] - cat([both[hd][b][:chunk] for b in range(bb)])).astype(BF16) for hd in heads]
        o = [cat([both[hd][b][chunk:] for b in range(bb)]) + jnp.dot(attn[hd], v16[hd], preferred_element_type=F32)
             for hd in heads]
        for hd in heads:
            for b in range(bb):
                seg = slice(b * chunk, (b + 1) * chunk)
                g_last = gcol[hd][(b + 1) * chunk - 1:(b + 1) * chunk]
                kd = (k[hd][seg] * jnp.exp(g_last - gcol[hd][seg])).astype(BF16)
                sn_ref[b, hd] = s_old[hd][b] * jnp.exp(g_last) + lax.dot_general(
                    kd, v16[hd][seg], TN_DIMS, preferred_element_type=F32)
        for hd in heads:
            sl = lanes[hd]
            on = o[hd] * lax.rsqrt(jnp.mean(o[hd] * o[hd], axis=-1, keepdims=True) + NORM_EPS) * nw_ref[...]
            if bb == 1:
                zz = z_ref[0, rsl, sl]
                o_ref[0, rsl, sl] = on * (zz * jax.nn.sigmoid(zz))
            else:
                zz = z_ref[:, :, sl].reshape(rows, LANES)
                o_ref[:, :, sl] = (on * (zz * jax.nn.sigmoid(zz))).reshape(bb, chunk, LANES)
        return carry

    if n_chunks == 1:
        do_chunk(0, 0)
    else:
        lax.fori_loop(0, n_chunks, do_chunk, 0)


def _delta(pm3, ps3, conv_state, s0, cw, a_log, dt_bias, norm_w, *, bb, tb, chunk):
    b, t, _ = pm3.shape
    nheads = a_log.shape[-1]
    w = nheads * LANES
    assert s0.shape == (b, nheads, LANES, LANES) and nheads <= 8
    assert b % bb == 0 and t % tb == 0 and tb % chunk == 0 and (bb * chunk) % 8 == 0 and bb * chunk <= LANES
    pad = lambda v: jnp.zeros((1, LANES), F32).at[0, nheads:2 * nheads].set(v)
    kern = functools.partial(_delta_kernel, chunk=chunk, nheads=nheads)
    col = lambda c: pl.BlockSpec((bb, tb, w), lambda i, j, c=c: (i, j, c))
    cst = lambda c: pl.BlockSpec((bb, CONV_W - 1, w), lambda i, j, c=c: (i, 0, c))
    cwt = lambda c: pl.BlockSpec((CONV_W, w), lambda i, j, c=c: (0, c))
    vec = pl.BlockSpec((1, LANES), lambda i, j: (0, 0))
    sblk = pl.BlockSpec((bb, nheads, LANES, LANES), lambda i, j: (i, 0, 0, 0))
    out, sn = pl.pallas_call(
        kern,
        out_shape=(jax.ShapeDtypeStruct((b, t, w), F32), jax.ShapeDtypeStruct(s0.shape, F32)),
        grid=(b // bb, t // tb),
        in_specs=[col(2), col(3), col(4), col(5),
                  pl.BlockSpec((bb, tb, LANES), lambda i, j: (i, j, 0)),
                  cst(0), cst(1), cst(2), sblk, cwt(0), cwt(1), cwt(2), vec, vec, vec],
        out_specs=(pl.BlockSpec((bb, tb, w), lambda i, j: (i, j, 0)), sblk),
        scratch_shapes=[pltpu.VMEM((bb, tb + CONV_PAD, w), F32)] * 3
                       + [pltpu.VMEM((bb * tb, w), F32)] * 3
                       + [pltpu.VMEM((bb * tb, LANES), F32)] * 2,
        compiler_params=_params("arbitrary", "arbitrary"),
        name="delta",
    )(pm3, pm3, pm3, pm3, ps3, conv_state, conv_state, conv_state, s0, cw, cw, cw,
      pad(jnp.exp(a_log.astype(F32))), pad(dt_bias.astype(F32)), norm_w.reshape(1, LANES))
    return out, sn


def _outproj_kernel(ol_ref, od_ref, x_ref, g1_ref, sc_ref, sh_ref, nw_ref, wt_ref, wb_ref, x1_ref, h2_ref):
    tm = h2_ref.shape[0]
    a = ol_ref[...].reshape(tm, ol_ref.shape[-1]).astype(BF16)
    b = od_ref[...].reshape(tm, od_ref.shape[-1]).astype(BF16)
    mix = (jnp.dot(a, wt_ref[...], preferred_element_type=F32)
           + jnp.dot(b, wb_ref[...], preferred_element_type=F32))
    x1 = x_ref[...] + g1_ref[...] * mix.reshape(x_ref.shape)
    x1_ref[...] = x1
    y = x1 * lax.rsqrt(jnp.mean(x1 * x1, axis=-1, keepdims=True) + NORM_EPS) * nw_ref[...]
    h2_ref[...] = (y * (1.0 + sc_ref[...]) + sh_ref[...]).reshape(h2_ref.shape).astype(BF16)


def _outproj(ol, od, x, g1, sc, sh, nw, wt, wb):
    b, t, d = x.shape
    n = b * t
    lw, dw = ol.shape[-1], od.shape[-1]
    tm = _tile(n, 512, 16)
    gb, r, nt = _row_groups(b, t, tm)
    tok = lambda c: pl.BlockSpec((gb, r, c), lambda i: (i // nt, i % nt, 0))
    per = pl.BlockSpec((gb, 1, d), lambda i: (i // nt, 0, 0))
    return pl.pallas_call(
        _outproj_kernel,
        out_shape=(jax.ShapeDtypeStruct((b, t, d), F32), jax.ShapeDtypeStruct((n, d), BF16)),
        grid=(n // tm,),
        in_specs=[tok(lw), tok(dw), tok(d), per, per, per,
                  pl.BlockSpec((1, 1, d), lambda i: (0, 0, 0)),
                  pl.BlockSpec((lw, d), lambda i: (0, 0)),
                  pl.BlockSpec((dw, d), lambda i: (0, 0))],
        out_specs=(tok(d), pl.BlockSpec((tm, d), lambda i: (i, 0))),
        compiler_params=_params("arbitrary"),
        name="outproj",
    )(ol, od, x, g1, sc, sh, nw, wt, wb)


def _sorting_network(n):
    pairs = []
    p = 1
    while p < n:
        k = p
        while k >= 1:
            for j in range(k % p, n - k, 2 * k):
                for i in range(min(k, n - j - k)):
                    if (i + j) // (2 * p) == (i + j + k) // (2 * p):
                        pairs.append((i + j, i + j + k))
            k //= 2
        p *= 2
    return pairs


def _merge_top(lists, singles, count):
    lists = list(lists)
    singles = list(singles)
    vals = []
    for r in range(count):
        head = lists[0]
        for x in singles:
            head = jnp.maximum(head, x)
        m = jnp.max(head, axis=0, keepdims=True)
        vals.append(m)
        left = count - r - 1
        if left == 0:
            break
        eq = lists[0] == m
        keep = min(len(lists), left)
        lists = [jnp.where(eq, lists[v + 1] if v + 1 < len(lists) else -jnp.inf, lists[v]) for v in range(keep)]
        singles = [jnp.where(x == m, -jnp.inf, x) for x in singles]
    return vals


def _top_values(s, count):
    tiles = [s[8 * v:8 * (v + 1)] for v in range(s.shape[0] // 8)]
    for i, j in _sorting_network(len(tiles)):
        tiles[i], tiles[j] = jnp.maximum(tiles[i], tiles[j]), jnp.minimum(tiles[i], tiles[j])
    return _merge_top(tiles, [], count)


def _router_kernel(h_ref, wq_ref, sk_ref, s2_ref, thr_ref, r_ref, q_ref, *, nheads):
    nk = sk_ref.shape[1]
    half = sk_ref.shape[2]
    q_ref[...] = lax.dot_general(wq_ref[...], h_ref[...], NT_DIMS, preferred_element_type=F32)
    p = h_ref.shape[0]
    sub = lax.broadcasted_iota(jnp.int32, (8, p), 0)

    def head(hd, carry):
        base = pl.multiple_of(hd * (2 * half), 2 * half)
        q1 = q_ref[pl.ds(base, half), :].astype(BF16)
        q2 = q_ref[pl.ds(base + half, half), :].astype(BF16)
        s1 = jnp.dot(sk_ref[2 * hd], q1, preferred_element_type=F32)
        s2 = jnp.dot(sk_ref[2 * hd + 1], q2, preferred_element_type=F32)
        nv = PEER_TOPK + 1
        t1 = _top_values(s1, nv)
        t2 = _top_values(s2, nv)
        t2_lo = jnp.concatenate(t2[:8], axis=0)
        t2_hi = jnp.concatenate(t2[8:16], axis=0)
        lists = [t1[0] + t2_lo]
        for a in range(1, nv):
            nb = nv // (a + 1)
            c = t1[a] + t2_lo
            lists.append(c if nb >= 8 else jnp.where(sub < nb, c, -jnp.inf))
        singles = [t1[0] + t2_hi, jnp.where(sub < 1, t1[0] + t2[16], -jnp.inf)]
        cand = jnp.concatenate(lists + singles, axis=0)
        best = _merge_top(lists, singles, nv)
        tau = best[PEER_TOPK - 1]
        mid = 0.5 * (tau + best[PEER_TOPK])
        top = t1[0] + t2[0]
        z = jnp.sum(jnp.where(cand >= tau, jnp.exp(cand - top), 0.0), axis=0, keepdims=True)
        s2_ref[hd] = s2 * LOG2E
        thr_ref[hd] = (mid - s1) * LOG2E
        r_ref[hd] = (s1 - (top + jnp.log(z))) * LOG2E
        return carry

    lax.fori_loop(0, nheads, head, 0)


def _router(h2, wq_t, sk, *, tp):
    n, d = h2.shape
    nheads = sk.shape[0] // 2
    nk = sk.shape[1]
    hq = wq_t.shape[0]
    out = jax.ShapeDtypeStruct((nheads, nk, n), F32)
    oblk = pl.BlockSpec((nheads, nk, tp), lambda i: (0, 0, i))
    return pl.pallas_call(
        functools.partial(_router_kernel, nheads=nheads),
        out_shape=(out, out, out),
        grid=(n // tp,),
        in_specs=[pl.BlockSpec((tp, d), lambda i: (i, 0)),
                  pl.BlockSpec((hq, d), lambda i: (0, 0)),
                  pl.BlockSpec(sk.shape, lambda i: (0, 0, 0))],
        out_specs=(oblk, oblk, oblk),
        scratch_shapes=[pltpu.VMEM((hq, tp), F32)],
        compiler_params=_params("arbitrary"),
        name="router",
    )(h2, wq_t, sk)


def _dense_kernel(h_ref, u_ref, vt_ref, s2_ref, thr_ref, r_ref, x1_ref, g2_ref, fw_ref,
                  y_ref, acc_ref, st_ref, coef_ref, *, nheads, rows_per_step, n_chunks):
    s = pl.program_id(1)
    nk = s2_ref.shape[1]
    tp = h_ref.shape[0]

    @pl.when(s == 0)
    def _():
        acc_ref[...] = jnp.zeros_like(acc_ref)
        st_ref[...] = jnp.zeros_like(st_ref)
        coef_ref[...] = jnp.zeros_like(coef_ref)

    cur = s % 2
    prev = 1 - cur
    acc_ref[...] += jnp.dot(vt_ref[0], coef_ref[cur], preferred_element_type=F32)
    chunk = jnp.clip(s - 1, 0, n_chunks - 1)
    for ii in range(rows_per_step):
        i1 = chunk * rows_per_step + ii
        rows = slice(ii * nk, (ii + 1) * nk)
        thr_rows = [thr_ref[hd, pl.ds(i1, 1), :] for hd in range(nheads)]
        r_rows = [r_ref[hd, pl.ds(i1, 1), :] for hd in range(nheads)]
        for pb in range(tp // LANES):
            cols = slice(pb * LANES, (pb + 1) * LANES)
            gate = None
            for hd in range(nheads):
                s2 = s2_ref[hd, :, cols]
                term = jnp.where(s2 >= thr_rows[hd][:, cols], jnp.exp2(s2 + r_rows[hd][:, cols]), 0.0)
                gate = term if gate is None else gate + term
            coef_ref[prev, rows, cols] = (jax.nn.gelu(st_ref[prev, rows, cols]) * gate).astype(BF16)
    st_ref[cur] = lax.dot_general(u_ref[...], h_ref[...], NT_DIMS, preferred_element_type=F32)

    @pl.when(s == pl.num_programs(1) - 1)
    def _():
        peer = acc_ref[...].T.reshape(x1_ref.shape)
        x2 = x1_ref[...] + g2_ref[...] * peer
        y_ref[...] = x2 * lax.rsqrt(jnp.mean(x2 * x2, axis=-1, keepdims=True) + NORM_EPS) * fw_ref[...]


def _dense(h2, u16, vt16, s2t, thr, rl, x1, g2, fw, *, tp):
    b, t, d = x1.shape
    n = b * t
    nheads, nk, _ = s2t.shape
    n_chunks, _, ce = vt16.shape
    rows_per_step = ce // nk
    assert u16.shape[0] == n_chunks * ce
    gb, r, nt = _row_groups(b, t, tp)
    rblk = pl.BlockSpec((nheads, nk, tp), lambda i, s: (0, 0, i))
    tok = pl.BlockSpec((gb, r, d), lambda i, s: (i // nt, i % nt, 0))
    return pl.pallas_call(
        functools.partial(_dense_kernel, nheads=nheads, rows_per_step=rows_per_step, n_chunks=n_chunks),
        out_shape=jax.ShapeDtypeStruct((b, t, d), F32),
        grid=(n // tp, n_chunks + 2),
        in_specs=[pl.BlockSpec((tp, d), lambda i, s: (i, 0)),
                  pl.BlockSpec((ce, d), lambda i, s: (jnp.minimum(s, n_chunks - 1), 0)),
                  pl.BlockSpec((1, d, ce), lambda i, s: (jnp.clip(s - 2, 0, n_chunks - 1), 0, 0)),
                  rblk, rblk, rblk, tok,
                  pl.BlockSpec((gb, 1, d), lambda i, s: (i // nt, 0, 0)),
                  pl.BlockSpec((1, 1, d), lambda i, s: (0, 0, 0))],
        out_specs=tok,
        scratch_shapes=[pltpu.VMEM((d, tp), F32), pltpu.VMEM((2, ce, tp), F32), pltpu.VMEM((2, ce, tp), BF16)],
        compiler_params=_params("arbitrary", "arbitrary"),
        name="dense",
    )(h2, u16, vt16, s2t, thr, rl, x1, g2, fw)


def _group(x, mod, lru_conv, lru_h, dn_conv, dn_s, reset_first, p, fw, *, lru_tiles, dn_tiles, tp):
    b, t, d = x.shape
    sh1, sc1, g1, sh2, sc2, g2 = [m.reshape(b, 1, d) for m in jnp.split(mod, 6, axis=-1)]
    pm, ps = _inproj(x, sc1, sh1, p['norm_mix_w'], p['w_main'], p['w_small'])
    pm3 = pm.reshape(b, t, -1)
    ps3 = ps.reshape(b, t, LANES)
    lw = p['lru_lambda'].shape[-1]
    out_lru, new_h = _lru(pm3, lru_conv, lru_h, p['lru_conv_w'], p['lru_conv_b'], p['lru_wg'], p['lru_ba'],
                          p['lru_bx'], p['lru_lambda'], reset_first=reset_first, bb=lru_tiles[0], tc=lru_tiles[1])
    out_dn, new_s = _delta(pm3, ps3, dn_conv, dn_s, p['dn_conv_w'], p['dn_A_log'], p['dn_dt_bias'], p['dn_norm_w'],
                           bb=dn_tiles[0], tb=dn_tiles[1], chunk=dn_tiles[2])
    keep = CONV_W - 1
    new_lru_conv = pm3[:, t - keep:, :lw]
    new_dn_conv = pm3[:, t - keep:, 2 * lw:2 * lw + dn_conv.shape[-1]]
    x1, h2 = _outproj(out_lru, out_dn, x, g1, sc2, sh2, p['norm_ffn_w'], p['w_out_top'], p['w_out_bot'])
    s2t, thr, r = _router(h2, p['wq_t'], p['subkeys'], tp=tp)
    y = _dense(h2, p['peer_u'], p['peer_vt'], s2t, thr, r, x1, g2, fw, tp=tp)
    return y, (new_lru_conv, new_h, new_dn_conv, new_s)


def kernel(x_prompt, x_sample, state_lru_conv, state_lru_h, state_dn_conv, state_dn_S, c_prompt, c_sample,
           w_ada, b_ada, norm_mix_w, norm_ffn_w, w_in, lru_conv_w, lru_conv_b, lru_wa, lru_ba, lru_wx, lru_bx,
           lru_lambda, dn_conv_w, dn_A_log, dn_dt_bias, dn_norm_w, w_out, peer_wq, peer_subkeys, peer_u, peer_v,
           final_norm_w):
    depth = w_ada.shape[0]
    assert depth == 1, "the final norm is fused into the last layer's expert kernel"
    bp, seq, d = x_prompt.shape
    bs, dseq, _ = x_sample.shape
    lw = lru_lambda.shape[-1]
    dn_heads = dn_A_log.shape[-1]
    dn_w = dn_heads * LANES
    conv_ch = dn_conv_w.shape[-1]
    assert lru_wa.shape[-1] == LANES and lw == dn_w and conv_ch == 3 * dn_w
    assert peer_subkeys.shape[-1] == LANES and peer_subkeys.shape[-2] == LANES
    main = 2 * lw + conv_ch + dn_w
    l = 0
    w_small = jnp.zeros((d, LANES), F32).at[:, :2 * dn_heads].set(w_in[l][:, main:]).astype(BF16)
    nheads = peer_subkeys.shape[1]
    p = {
        'norm_mix_w': norm_mix_w[l].reshape(1, 1, d), 'norm_ffn_w': norm_ffn_w[l].reshape(1, 1, d),
        'w_main': w_in[l][:, :main].astype(BF16), 'w_small': w_small,
        'lru_conv_w': lru_conv_w[l], 'lru_conv_b': lru_conv_b[l],
        'lru_wg': jnp.concatenate([lru_wa[l], lru_wx[l]], axis=-1).astype(BF16),
        'lru_ba': lru_ba[l], 'lru_bx': lru_bx[l], 'lru_lambda': lru_lambda[l],
        'dn_conv_w': dn_conv_w[l], 'dn_A_log': dn_A_log[l], 'dn_dt_bias': dn_dt_bias[l], 'dn_norm_w': dn_norm_w[l],
        'w_out_top': w_out[l][:lw].astype(BF16), 'w_out_bot': w_out[l][lw:].astype(BF16),
        'wq_t': peer_wq[l].T.astype(BF16),
        'subkeys': peer_subkeys[l].reshape(nheads * 2, LANES, LANES).astype(BF16),
        'peer_u': peer_u[l].astype(BF16),
        'peer_vt': peer_v[l].reshape(-1, PEER_CHUNK, d).transpose(0, 2, 1).astype(BF16),
    }
    fw = final_norm_w.reshape(1, 1, d)
    mod = _ada(jnp.concatenate([c_prompt, c_sample], axis=0), w_ada[l], b_ada[l])
    zeros = lambda *s: jnp.zeros(s, F32)
    yp, sp = _group(x_prompt, mod[:bp], zeros(bp, CONV_W - 1, lw), zeros(bp, lw), zeros(bp, CONV_W - 1, conv_ch),
                    zeros(bp, dn_heads, LANES, LANES), True, p, fw,
                    lru_tiles=(bp, _tile(seq, 128, 8)), dn_tiles=(1, _tile(seq, 256, DN_CHUNK), min(DN_CHUNK, seq)),
                    tp=_tile(bp * seq, 512, LANES))
    ys, ss = _group(x_sample, mod[bp:], state_lru_conv[l], state_lru_h[l], state_dn_conv[l], state_dn_S[l],
                    False, p, fw,
                    lru_tiles=(_tile(bs, 32, 1), dseq), dn_tiles=(_tile(bs, DN_CHUNK // dseq, 1), dseq, dseq),
                    tp=_tile(bs * dseq, 512, LANES))
    stack = lambda v: v[None]
    return (yp, ys, stack(sp[0]), stack(sp[1]), stack(sp[2]), stack(sp[3]),
            stack(ss[0]), stack(ss[1]), stack(ss[2]), stack(ss[3]))
```

```python
import functools
import math

import jax
import jax.numpy as jnp
from jax import lax
from jax.experimental import pallas as pl
from jax.experimental.pallas import tpu as pltpu

F32 = jnp.float32
BF16 = jnp.bfloat16
NORM_EPS = 1e-6
LRU_C = 8.0
CONV_W = 4
CONV_PAD = 8
PEER_TOPK = 16
PEER_CHUNK = 512
DN_CHUNK = 64
LOG2E = math.log2(math.e)
LANES = 128
VMEM_LIMIT_BYTES = 60 * 1024 * 1024
NT_DIMS = (((1,), (1,)), ((), ()))
TN_DIMS = (((0,), (0,)), ((), ()))


def _params(*semantics):
    return pltpu.CompilerParams(dimension_semantics=semantics, vmem_limit_bytes=VMEM_LIMIT_BYTES)


def _tile(n, target, multiple):
    if n <= target:
        return n
    t = (target // multiple) * multiple
    while t >= multiple:
        if n % t == 0:
            return t
        t -= multiple
    raise ValueError(f"no tile for {n} <= {target} in multiples of {multiple}")


def _softplus(x):
    return jnp.maximum(x, 0.0) + jnp.log1p(jnp.exp(-jnp.abs(x)))


def _row_groups(batch, seq, tm):
    r = min(seq, tm)
    assert seq % r == 0 and tm % r == 0 and (batch * seq) % tm == 0
    return tm // r, r, seq // r


def _ada_kernel(c_ref, w_ref, b_ref, o_ref):
    c = c_ref[...]
    a = (c * jax.nn.sigmoid(c)).astype(BF16)
    o_ref[...] = jnp.dot(a, w_ref[...].astype(BF16), preferred_element_type=F32) + b_ref[...]


def _ada(c, w, b):
    bc, d = c.shape
    n = w.shape[1]
    tn = _tile(n, 1024, LANES)
    return pl.pallas_call(
        _ada_kernel,
        out_shape=jax.ShapeDtypeStruct((bc, n), F32),
        grid=(n // tn,),
        in_specs=[pl.BlockSpec((bc, d), lambda j: (0, 0)),
                  pl.BlockSpec((d, tn), lambda j: (0, j)),
                  pl.BlockSpec((1, tn), lambda j: (0, j))],
        out_specs=pl.BlockSpec((bc, tn), lambda j: (0, j)),
        compiler_params=_params("arbitrary"),
        name="ada",
    )(c, w, b.reshape(1, n))


def _inproj_kernel(x_ref, sc_ref, sh_ref, nw_ref, wm_ref, ws_ref, om_ref, os_ref, h_ref):
    @pl.when(pl.program_id(1) == 0)
    def _():
        x = x_ref[...]
        y = x * lax.rsqrt(jnp.mean(x * x, axis=-1, keepdims=True) + NORM_EPS) * nw_ref[...]
        h = y * (1.0 + sc_ref[...]) + sh_ref[...]
        hb = h.reshape(h_ref.shape).astype(BF16)
        h_ref[...] = hb
        os_ref[...] = jnp.dot(hb, ws_ref[...], preferred_element_type=F32)

    om_ref[...] = jnp.dot(h_ref[...], wm_ref[...], preferred_element_type=F32)


def _inproj(x, sc, sh, nw, wm, ws):
    b, t, d = x.shape
    n = b * t
    m = wm.shape[1]
    tm = _tile(n, 512, 8)
    tn = _tile(m, 3072, LANES)
    gb, r, nt = _row_groups(b, t, tm)
    return pl.pallas_call(
        _inproj_kernel,
        out_shape=(jax.ShapeDtypeStruct((n, m), F32), jax.ShapeDtypeStruct((n, LANES), F32)),
        grid=(n // tm, m // tn),
        in_specs=[pl.BlockSpec((gb, r, d), lambda i, j: (i // nt, i % nt, 0)),
                  pl.BlockSpec((gb, 1, d), lambda i, j: (i // nt, 0, 0)),
                  pl.BlockSpec((gb, 1, d), lambda i, j: (i // nt, 0, 0)),
                  pl.BlockSpec((1, 1, d), lambda i, j: (0, 0, 0)),
                  pl.BlockSpec((d, tn), lambda i, j: (0, j)),
                  pl.BlockSpec((d, LANES), lambda i, j: (0, 0))],
        out_specs=(pl.BlockSpec((tm, tn), lambda i, j: (i, j)),
                   pl.BlockSpec((tm, LANES), lambda i, j: (i, 0))),
        scratch_shapes=[pltpu.VMEM((tm, d), BF16)],
        compiler_params=_params("arbitrary", "arbitrary"),
        name="inproj",
    )(x, sc, sh, nw, wm, ws)


def _causal_conv(x_ref, xs_ref, cw_ref):
    t = x_ref.shape[1]
    xs_ref[:, CONV_PAD:CONV_PAD + t, :] = x_ref[...]
    cw = cw_ref[...]
    y = None
    for k in range(CONV_W):
        lo = CONV_PAD - (CONV_W - 1) + k
        term = xs_ref[:, lo:lo + t, :] * cw[k:k + 1, :][None]
        y = term if y is None else y + term
    xs_ref[:, CONV_PAD - (CONV_W - 1):CONV_PAD, :] = xs_ref[:, t + CONV_PAD - (CONV_W - 1):t + CONV_PAD, :]
    return y


def _lru_kernel(xl_ref, yl_ref, cs_ref, h0_ref, cw_ref, cb_ref, wg_ref, ba_ref, bx_ref, lam_ref,
                o_ref, hn_ref, xs_ref, a_ref, u_ref, h_ref, *, reset_first):
    ti = pl.program_id(1)
    bb, tc, lw = xl_ref.shape
    nheads = lw // LANES

    @pl.when(ti == 0)
    def _():
        xs_ref[:, CONV_PAD - (CONV_W - 1):CONV_PAD, :] = cs_ref[...]
        h_ref[...] = h0_ref[...]

    xc = (_causal_conv(xl_ref, xs_ref, cw_ref) + cb_ref[...][None]).reshape(bb * tc, lw)
    sp = _softplus(-lam_ref[...])
    t_glob = lax.broadcasted_iota(jnp.int32, (bb, tc, LANES), 1) + ti * tc
    for hd in range(nheads):
        sl = slice(hd * LANES, (hd + 1) * LANES)
        xh = xc[:, sl]
        gates = jnp.dot(xh.astype(BF16), wg_ref[hd], preferred_element_type=F32)
        r = jax.nn.sigmoid(gates[:, :LANES] + ba_ref[:, sl])
        i = jax.nn.sigmoid(gates[:, LANES:] + bx_ref[:, sl])
        log_a = -LRU_C * r * sp[:, sl]
        a = jnp.exp(log_a)
        mult = jnp.sqrt(jnp.maximum(-jnp.tanh(log_a) * (a * a + 1.0), 0.0))
        u = i * xh
        a3 = a.reshape(bb, tc, LANES)
        m3 = mult.reshape(bb, tc, LANES)
        if reset_first:
            m3 = jnp.where(t_glob == 0, 1.0, m3)
        a_ref[:, :, sl] = a3
        u_ref[:, :, sl] = m3 * u.reshape(bb, tc, LANES)

    def step(t, h):
        h = a_ref[:, pl.ds(t, 1), :] * h + u_ref[:, pl.ds(t, 1), :]
        u_ref[:, pl.ds(t, 1), :] = h
        return h

    h = lax.fori_loop(0, tc, step, h_ref[...], unroll=8)
    h_ref[...] = h
    hn_ref[...] = h
    o_ref[...] = u_ref[...] * jax.nn.gelu(yl_ref[...])


def _lru(pm3, conv_state, h0, cw, cb, wg, ba, bx, lam, *, reset_first, bb, tc):
    b, t, _ = pm3.shape
    lw = lam.shape[-1]
    assert b % bb == 0 and t % tc == 0 and tc % 8 == 0
    row = lambda v: v.reshape(1, lw)
    kern = functools.partial(_lru_kernel, reset_first=reset_first)
    out, hn = pl.pallas_call(
        kern,
        out_shape=(jax.ShapeDtypeStruct((b, t, lw), F32), jax.ShapeDtypeStruct((b, 1, lw), F32)),
        grid=(b // bb, t // tc),
        in_specs=[pl.BlockSpec((bb, tc, lw), lambda i, j: (i, j, 0)),
                  pl.BlockSpec((bb, tc, lw), lambda i, j: (i, j, 1)),
                  pl.BlockSpec((bb, CONV_W - 1, lw), lambda i, j: (i, 0, 0)),
                  pl.BlockSpec((bb, 1, lw), lambda i, j: (i, 0, 0)),
                  pl.BlockSpec((CONV_W, lw), lambda i, j: (0, 0)),
                  pl.BlockSpec((1, lw), lambda i, j: (0, 0)),
                  pl.BlockSpec(wg.shape, lambda i, j: (0, 0, 0)),
                  pl.BlockSpec((1, lw), lambda i, j: (0, 0)),
                  pl.BlockSpec((1, lw), lambda i, j: (0, 0)),
                  pl.BlockSpec((1, lw), lambda i, j: (0, 0))],
        out_specs=(pl.BlockSpec((bb, tc, lw), lambda i, j: (i, j, 0)),
                   pl.BlockSpec((bb, 1, lw), lambda i, j: (i, 0, 0))),
        scratch_shapes=[pltpu.VMEM((bb, tc + CONV_PAD, lw), F32),
                        pltpu.VMEM((bb, tc, lw), F32),
                        pltpu.VMEM((bb, tc, lw), F32),
                        pltpu.VMEM((bb, 1, lw), F32)],
        compiler_params=_params("arbitrary", "arbitrary"),
        name="lru",
    )(pm3, pm3, conv_state, h0.reshape(b, 1, lw), cw, row(cb), wg, row(ba), row(bx), row(lam))
    return out, hn.reshape(b, lw)


def _delta_kernel(q_ref, k_ref, v_ref, z_ref, ps_ref, csq_ref, csk_ref, csv_ref, s0_ref,
                  cwq_ref, cwk_ref, cwv_ref, ea_ref, dtb_ref, nw_ref,
                  o_ref, sn_ref,
                  xq_ref, xk_ref, xv_ref, qn_ref, kn_ref, vn_ref, beta_ref, g_ref, *, chunk, nheads):
    ti = pl.program_id(1)
    bb, tb, w = q_ref.shape
    n_chunks = tb // chunk
    stacked = n_chunks == 1
    units = 1 if stacked else bb
    segs = bb if stacked else 1
    rows = segs * chunk
    lo = CONV_PAD - (CONV_W - 1)

    @pl.when(ti == 0)
    def _():
        xq_ref[:, lo:CONV_PAD, :] = csq_ref[...]
        xk_ref[:, lo:CONV_PAD, :] = csk_ref[...]
        xv_ref[:, lo:CONV_PAD, :] = csv_ref[...]
        sn_ref[...] = s0_ref[...]

    def conv_silu(x_ref, xs_ref, cw_ref):
        y = _causal_conv(x_ref, xs_ref, cw_ref)
        return (y * jax.nn.sigmoid(y)).reshape(bb * tb, w)

    qc = conv_silu(q_ref, xq_ref, cwq_ref)
    kc = conv_silu(k_ref, xk_ref, cwk_ref)
    vn_ref[...] = conv_silu(v_ref, xv_ref, cwv_ref)
    for hd in range(nheads):
        sl = slice(hd * LANES, (hd + 1) * LANES)
        qh = qc[:, sl]
        kh = kc[:, sl]
        qn_ref[:, sl] = qh * (lax.rsqrt(jnp.sum(qh * qh, axis=-1, keepdims=True) + NORM_EPS) * (LANES ** -0.5))
        kn_ref[:, sl] = kh * lax.rsqrt(jnp.sum(kh * kh, axis=-1, keepdims=True) + NORM_EPS)
    ps = ps_ref[...].reshape(bb * tb, LANES)
    beta_ref[...] = jax.nn.sigmoid(ps)
    g_ref[...] = -ea_ref[...] * _softplus(ps + dtb_ref[...])

    ii = lax.broadcasted_iota(jnp.int32, (rows, rows), 0)
    jj = lax.broadcasted_iota(jnp.int32, (rows, rows), 1)
    if segs == 1:
        le = jj <= ii
        lt = jj < ii
    else:
        same = (ii // chunk) == (jj // chunk)
        le = jnp.logical_and(same, jj <= ii)
        lt = jnp.logical_and(same, jj < ii)
    le_b = jnp.where(le, 1.0, 0.0).astype(BF16)
    assert chunk & (chunk - 1) == 0 and chunk >= 2
    levels = []
    b = 1
    while b < chunk:
        sh = b.bit_length()
        join = jnp.logical_and((ii >> sh) == (jj >> sh), (ii >> (sh - 1)) != (jj >> (sh - 1)))
        levels.append(jnp.logical_and(join, lt))
        b *= 2
    lanes = [slice(hd * LANES, (hd + 1) * LANES) for hd in range(nheads)]
    probs = [(u, hd) for u in range(units) for hd in range(nheads)]

    def mm(a, b):
        return jnp.dot(a.astype(BF16), b.astype(BF16), preferred_element_type=F32)

    def do_chunk(c, carry):
        rsl, gc, gct, eg, beta = [], [], [], [], []
        for u in range(units):
            rs = pl.ds(pl.multiple_of(u * tb + c * rows, rows), rows)
            g = g_ref[rs, :]
            g_hi = g.astype(BF16)
            g_lo = (g - g_hi.astype(F32)).astype(BF16)
            gcu = (jnp.dot(le_b, g_hi, preferred_element_type=F32)
                   + jnp.dot(le_b, g_lo, preferred_element_type=F32))
            gcp = gcu if rows == LANES else jnp.concatenate([gcu, jnp.zeros((LANES - rows, LANES), F32)], axis=0)
            rsl.append(rs)
            gc.append(gcu)
            gct.append(gcp.T)
            eg.append(jnp.exp(gcu))
            beta.append(beta_ref[rs, :])
        q = [qn_ref[rsl[u], lanes[hd]] for u, hd in probs]
        k = [kn_ref[rsl[u], lanes[hd]] for u, hd in probs]
        v = [vn_ref[rsl[u], lanes[hd]] for u, hd in probs]
        bcol = [beta[u][:, hd:hd + 1] for u, hd in probs]
        gcol = [gc[u][:, nheads + hd:nheads + hd + 1] for u, hd in probs]
        egc = [eg[u][:, nheads + hd:nheads + hd + 1] for u, hd in probs]
        np_ = range(len(probs))
        decay = [jnp.where(le, jnp.exp(jnp.where(le, gcol[i] - gct[u][nheads + hd:nheads + hd + 1, :rows], 0.0)), 0.0)
                 for i, (u, hd) in enumerate(probs)]
        kb = [k[i] * bcol[i] for i in np_]
        k16 = [k[i].astype(BF16) for i in np_]
        kk = [lax.dot_general(kb[i].astype(BF16), k16[i], NT_DIMS, preferred_element_type=F32) for i in np_]
        qk = [lax.dot_general(q[i].astype(BF16), k16[i], NT_DIMS, preferred_element_type=F32) for i in np_]
        lmat = [jnp.where(lt, kk[i] * decay[i], 0.0) for i in np_]
        attn = [jnp.where(le, qk[i] * decay[i], 0.0).astype(BF16) for i in np_]
        nmat = [jnp.where(levels[0], -lmat[i], 0.0) for i in np_]
        for lvl in levels[1:]:
            off = [jnp.where(lvl, lmat[i], 0.0) for i in np_]
            m = [off[i] + mm(nmat[i], off[i]) for i in np_]
            nmat = [nmat[i] - m[i] - mm(m[i], nmat[i]) for i in np_]
        rhs = [jnp.concatenate([v[i] * bcol[i], kb[i] * egc[i]], axis=1) for i in np_]
        sol = [rhs[i] + mm(nmat[i], rhs[i]) for i in np_]
        qe = [q[i] * egc[i] for i in np_]
        seq = lambda u, b: b if stacked else u
        s_old = [[sn_ref[seq(u, b), hd] for b in range(segs)] for u, hd in probs]
        both = [[jnp.dot(jnp.concatenate([sol[i][b * chunk:(b + 1) * chunk, LANES:],
                                          qe[i][b * chunk:(b + 1) * chunk]], axis=0).astype(BF16),
                         s_old[i][b].astype(BF16), preferred_element_type=F32) for b in range(segs)] for i in np_]
        cat = lambda parts: parts[0] if segs == 1 else jnp.concatenate(parts, axis=0)
        v16 = [(sol[i][:, :LANES] - cat([both[i][b][:chunk] for b in range(segs)])).astype(BF16) for i in np_]
        o = [cat([both[i][b][chunk:] for b in range(segs)]) + jnp.dot(attn[i], v16[i], preferred_element_type=F32)
             for i in np_]
        for i, (u, hd) in enumerate(probs):
            for b in range(segs):
                seg = slice(b * chunk, (b + 1) * chunk)
                g_last = gcol[i][(b + 1) * chunk - 1:(b + 1) * chunk]
                kd = (k[i][seg] * jnp.exp(g_last - gcol[i][seg])).astype(BF16)
                sn_ref[seq(u, b), hd] = s_old[i][b] * jnp.exp(g_last) + lax.dot_general(
                    kd, v16[i][seg], TN_DIMS, preferred_element_type=F32)
        for i, (u, hd) in enumerate(probs):
            sl = lanes[hd]
            on = o[i] * lax.rsqrt(jnp.mean(o[i] * o[i], axis=-1, keepdims=True) + NORM_EPS) * nw_ref[...]
            if stacked:
                zz = z_ref[:, :, sl].reshape(rows, LANES)
                o_ref[:, :, sl] = (on * (zz * jax.nn.sigmoid(zz))).reshape(bb, chunk, LANES)
            else:
                tsl = pl.ds(pl.multiple_of(c * chunk, chunk), chunk)
                zz = z_ref[u, tsl, sl]
                o_ref[u, tsl, sl] = on * (zz * jax.nn.sigmoid(zz))
        return carry

    if n_chunks == 1:
        do_chunk(0, 0)
    else:
        lax.fori_loop(0, n_chunks, do_chunk, 0)


def _delta(pm3, ps3, conv_state, s0, cw, a_log, dt_bias, norm_w, *, bb, tb, chunk):
    b, t, _ = pm3.shape
    nheads = a_log.shape[-1]
    w = nheads * LANES
    assert s0.shape == (b, nheads, LANES, LANES) and nheads <= 8
    assert b % bb == 0 and t % tb == 0 and tb % chunk == 0 and chunk % 8 == 0
    assert tb > chunk or bb * chunk <= LANES
    pad = lambda v: jnp.zeros((1, LANES), F32).at[0, nheads:2 * nheads].set(v)
    kern = functools.partial(_delta_kernel, chunk=chunk, nheads=nheads)
    col = lambda c: pl.BlockSpec((bb, tb, w), lambda i, j, c=c: (i, j, c))
    cst = lambda c: pl.BlockSpec((bb, CONV_W - 1, w), lambda i, j, c=c: (i, 0, c))
    cwt = lambda c: pl.BlockSpec((CONV_W, w), lambda i, j, c=c: (0, c))
    vec = pl.BlockSpec((1, LANES), lambda i, j: (0, 0))
    sblk = pl.BlockSpec((bb, nheads, LANES, LANES), lambda i, j: (i, 0, 0, 0))
    out, sn = pl.pallas_call(
        kern,
        out_shape=(jax.ShapeDtypeStruct((b, t, w), F32), jax.ShapeDtypeStruct(s0.shape, F32)),
        grid=(b // bb, t // tb),
        in_specs=[col(2), col(3), col(4), col(5),
                  pl.BlockSpec((bb, tb, LANES), lambda i, j: (i, j, 0)),
                  cst(0), cst(1), cst(2), sblk, cwt(0), cwt(1), cwt(2), vec, vec, vec],
        out_specs=(pl.BlockSpec((bb, tb, w), lambda i, j: (i, j, 0)), sblk),
        scratch_shapes=[pltpu.VMEM((bb, tb + CONV_PAD, w), F32)] * 3
                       + [pltpu.VMEM((bb * tb, w), F32)] * 3
                       + [pltpu.VMEM((bb * tb, LANES), F32)] * 2,
        compiler_params=_params("arbitrary", "arbitrary"),
        name="delta",
    )(pm3, pm3, pm3, pm3, ps3, conv_state, conv_state, conv_state, s0, cw, cw, cw,
      pad(jnp.exp(a_log.astype(F32))), pad(dt_bias.astype(F32)), norm_w.reshape(1, LANES))
    return out, sn


def _outproj_kernel(ol_ref, od_ref, x_ref, g1_ref, sc_ref, sh_ref, nw_ref, wt_ref, wb_ref, x1_ref, h2_ref):
    tm = h2_ref.shape[0]
    a = ol_ref[...].reshape(tm, ol_ref.shape[-1]).astype(BF16)
    b = od_ref[...].reshape(tm, od_ref.shape[-1]).astype(BF16)
    mix = (jnp.dot(a, wt_ref[...], preferred_element_type=F32)
           + jnp.dot(b, wb_ref[...], preferred_element_type=F32))
    x1 = x_ref[...] + g1_ref[...] * mix.reshape(x_ref.shape)
    x1_ref[...] = x1
    y = x1 * lax.rsqrt(jnp.mean(x1 * x1, axis=-1, keepdims=True) + NORM_EPS) * nw_ref[...]
    h2_ref[...] = (y * (1.0 + sc_ref[...]) + sh_ref[...]).reshape(h2_ref.shape).astype(BF16)


def _outproj(ol, od, x, g1, sc, sh, nw, wt, wb):
    b, t, d = x.shape
    n = b * t
    lw, dw = ol.shape[-1], od.shape[-1]
    tm = _tile(n, 512, 16)
    gb, r, nt = _row_groups(b, t, tm)
    tok = lambda c: pl.BlockSpec((gb, r, c), lambda i: (i // nt, i % nt, 0))
    per = pl.BlockSpec((gb, 1, d), lambda i: (i // nt, 0, 0))
    return pl.pallas_call(
        _outproj_kernel,
        out_shape=(jax.ShapeDtypeStruct((b, t, d), F32), jax.ShapeDtypeStruct((n, d), BF16)),
        grid=(n // tm,),
        in_specs=[tok(lw), tok(dw), tok(d), per, per, per,
                  pl.BlockSpec((1, 1, d), lambda i: (0, 0, 0)),
                  pl.BlockSpec((lw, d), lambda i: (0, 0)),
                  pl.BlockSpec((dw, d), lambda i: (0, 0))],
        out_specs=(tok(d), pl.BlockSpec((tm, d), lambda i: (i, 0))),
        compiler_params=_params("arbitrary"),
        name="outproj",
    )(ol, od, x, g1, sc, sh, nw, wt, wb)


def _sorting_network(n):
    pairs = []
    p = 1
    while p < n:
        k = p
        while k >= 1:
            for j in range(k % p, n - k, 2 * k):
                for i in range(min(k, n - j - k)):
                    if (i + j) // (2 * p) == (i + j + k) // (2 * p):
                        pairs.append((i + j, i + j + k))
            k //= 2
        p *= 2
    return pairs


def _merge_top(lists, singles, count):
    lists = list(lists)
    singles = list(singles)
    vals = []
    for r in range(count):
        head = lists[0]
        for x in singles:
            head = jnp.maximum(head, x)
        m = jnp.max(head, axis=0, keepdims=True)
        vals.append(m)
        left = count - r - 1
        if left == 0:
            break
        eq = lists[0] == m
        keep = min(len(lists), left)
        lists = [jnp.where(eq, lists[v + 1] if v + 1 < len(lists) else -jnp.inf, lists[v]) for v in range(keep)]
        singles = [jnp.where(x == m, -jnp.inf, x) for x in singles]
    return vals


def _top_values(s, count):
    tiles = [s[8 * v:8 * (v + 1)] for v in range(s.shape[0] // 8)]
    for i, j in _sorting_network(len(tiles)):
        tiles[i], tiles[j] = jnp.maximum(tiles[i], tiles[j]), jnp.minimum(tiles[i], tiles[j])
    return _merge_top(tiles, [], count)


def _router_kernel(h_ref, wq_ref, sk_ref, s2_ref, thr_ref, r_ref, q_ref, *, nheads):
    nk = sk_ref.shape[1]
    half = sk_ref.shape[2]
    q_ref[...] = lax.dot_general(wq_ref[...], h_ref[...], NT_DIMS, preferred_element_type=F32)
    p = h_ref.shape[0]
    sub = lax.broadcasted_iota(jnp.int32, (8, p), 0)

    def head(hd, carry):
        base = pl.multiple_of(hd * (2 * half), 2 * half)
        q1 = q_ref[pl.ds(base, half), :].astype(BF16)
        q2 = q_ref[pl.ds(base + half, half), :].astype(BF16)
        s1 = jnp.dot(sk_ref[2 * hd], q1, preferred_element_type=F32)
        s2 = jnp.dot(sk_ref[2 * hd + 1], q2, preferred_element_type=F32)
        nv = PEER_TOPK + 1
        t1 = _top_values(s1, nv)
        t2 = _top_values(s2, nv)
        t2_lo = jnp.concatenate(t2[:8], axis=0)
        t2_hi = jnp.concatenate(t2[8:16], axis=0)
        lists = [t1[0] + t2_lo]
        for a in range(1, nv):
            nb = nv // (a + 1)
            c = t1[a] + t2_lo
            lists.append(c if nb >= 8 else jnp.where(sub < nb, c, -jnp.inf))
        singles = [t1[0] + t2_hi, jnp.where(sub < 1, t1[0] + t2[16], -jnp.inf)]
        cand = jnp.concatenate(lists + singles, axis=0)
        best = _merge_top(lists, singles, nv)
        tau = best[PEER_TOPK - 1]
        mid = 0.5 * (tau + best[PEER_TOPK])
        top = t1[0] + t2[0]
        z = jnp.sum(jnp.where(cand >= tau, jnp.exp(cand - top), 0.0), axis=0, keepdims=True)
        s2_ref[hd] = s2 * LOG2E
        thr_ref[hd] = (mid - s1) * LOG2E
        r_ref[hd] = (s1 - (top + jnp.log(z))) * LOG2E
        return carry

    lax.fori_loop(0, nheads, head, 0)


def _router(h2, wq_t, sk, *, tp):
    n, d = h2.shape
    nheads = sk.shape[0] // 2
    nk = sk.shape[1]
    hq = wq_t.shape[0]
    out = jax.ShapeDtypeStruct((nheads, nk, n), F32)
    oblk = pl.BlockSpec((nheads, nk, tp), lambda i: (0, 0, i))
    return pl.pallas_call(
        functools.partial(_router_kernel, nheads=nheads),
        out_shape=(out, out, out),
        grid=(n // tp,),
        in_specs=[pl.BlockSpec((tp, d), lambda i: (i, 0)),
                  pl.BlockSpec((hq, d), lambda i: (0, 0)),
                  pl.BlockSpec(sk.shape, lambda i: (0, 0, 0))],
        out_specs=(oblk, oblk, oblk),
        scratch_shapes=[pltpu.VMEM((hq, tp), F32)],
        compiler_params=_params("arbitrary"),
        name="router",
    )(h2, wq_t, sk)


def _dense_kernel(h_ref, u_ref, vt_ref, s2_ref, thr_ref, r_ref, x1_ref, g2_ref, fw_ref,
                  y_ref, acc_ref, st_ref, coef_ref, *, nheads, rows_per_step, n_chunks):
    s = pl.program_id(1)
    nk = s2_ref.shape[1]
    tp = h_ref.shape[0]

    @pl.when(s == 0)
    def _():
        acc_ref[...] = jnp.zeros_like(acc_ref)
        st_ref[...] = jnp.zeros_like(st_ref)
        coef_ref[...] = jnp.zeros_like(coef_ref)

    cur = s % 2
    prev = 1 - cur
    acc_ref[...] += jnp.dot(vt_ref[0], coef_ref[cur], preferred_element_type=F32)
    chunk = jnp.clip(s - 1, 0, n_chunks - 1)
    for ii in range(rows_per_step):
        i1 = chunk * rows_per_step + ii
        rows = slice(ii * nk, (ii + 1) * nk)
        thr_rows = [thr_ref[hd, pl.ds(i1, 1), :] for hd in range(nheads)]
        r_rows = [r_ref[hd, pl.ds(i1, 1), :] for hd in range(nheads)]
        for pb in range(tp // LANES):
            cols = slice(pb * LANES, (pb + 1) * LANES)
            gate = None
            for hd in range(nheads):
                s2 = s2_ref[hd, :, cols]
                term = jnp.where(s2 >= thr_rows[hd][:, cols], jnp.exp2(s2 + r_rows[hd][:, cols]), 0.0)
                gate = term if gate is None else gate + term
            coef_ref[prev, rows, cols] = (jax.nn.gelu(st_ref[prev, rows, cols]) * gate).astype(BF16)
    st_ref[cur] = lax.dot_general(u_ref[...], h_ref[...], NT_DIMS, preferred_element_type=F32)

    @pl.when(s == pl.num_programs(1) - 1)
    def _():
        peer = acc_ref[...].T.reshape(x1_ref.shape)
        x2 = x1_ref[...] + g2_ref[...] * peer
        y_ref[...] = x2 * lax.rsqrt(jnp.mean(x2 * x2, axis=-1, keepdims=True) + NORM_EPS) * fw_ref[...]


def _dense(h2, u16, vt16, s2t, thr, rl, x1, g2, fw, *, tp):
    b, t, d = x1.shape
    n = b * t
    nheads, nk, _ = s2t.shape
    n_chunks, _, ce = vt16.shape
    rows_per_step = ce // nk
    assert u16.shape[0] == n_chunks * ce
    gb, r, nt = _row_groups(b, t, tp)
    rblk = pl.BlockSpec((nheads, nk, tp), lambda i, s: (0, 0, i))
    tok = pl.BlockSpec((gb, r, d), lambda i, s: (i // nt, i % nt, 0))
    return pl.pallas_call(
        functools.partial(_dense_kernel, nheads=nheads, rows_per_step=rows_per_step, n_chunks=n_chunks),
        out_shape=jax.ShapeDtypeStruct((b, t, d), F32),
        grid=(n // tp, n_chunks + 2),
        in_specs=[pl.BlockSpec((tp, d), lambda i, s: (i, 0)),
                  pl.BlockSpec((ce, d), lambda i, s: (jnp.minimum(s, n_chunks - 1), 0)),
                  pl.BlockSpec((1, d, ce), lambda i, s: (jnp.clip(s - 2, 0, n_chunks - 1), 0, 0)),
                  rblk, rblk, rblk, tok,
                  pl.BlockSpec((gb, 1, d), lambda i, s: (i // nt, 0, 0)),
                  pl.BlockSpec((1, 1, d), lambda i, s: (0, 0, 0))],
        out_specs=tok,
        scratch_shapes=[pltpu.VMEM((d, tp), F32), pltpu.VMEM((2, ce, tp), F32), pltpu.VMEM((2, ce, tp), BF16)],
        compiler_params=_params("arbitrary", "arbitrary"),
        name="dense",
    )(h2, u16, vt16, s2t, thr, rl, x1, g2, fw)


def _group(x, mod, lru_conv, lru_h, dn_conv, dn_s, reset_first, p, fw, *, lru_tiles, dn_tiles, tp):
    b, t, d = x.shape
    sh1, sc1, g1, sh2, sc2, g2 = [m.reshape(b, 1, d) for m in jnp.split(mod, 6, axis=-1)]
    pm, ps = _inproj(x, sc1, sh1, p['norm_mix_w'], p['w_main'], p['w_small'])
    pm3 = pm.reshape(b, t, -1)
    ps3 = ps.reshape(b, t, LANES)
    lw = p['lru_lambda'].shape[-1]
    out_lru, new_h = _lru(pm3, lru_conv, lru_h, p['lru_conv_w'], p['lru_conv_b'], p['lru_wg'], p['lru_ba'],
                          p['lru_bx'], p['lru_lambda'], reset_first=reset_first, bb=lru_tiles[0], tc=lru_tiles[1])
    out_dn, new_s = _delta(pm3, ps3, dn_conv, dn_s, p['dn_conv_w'], p['dn_A_log'], p['dn_dt_bias'], p['dn_norm_w'],
                           bb=dn_tiles[0], tb=dn_tiles[1], chunk=dn_tiles[2])
    keep = CONV_W - 1
    new_lru_conv = pm3[:, t - keep:, :lw]
    new_dn_conv = pm3[:, t - keep:, 2 * lw:2 * lw + dn_conv.shape[-1]]
    x1, h2 = _outproj(out_lru, out_dn, x, g1, sc2, sh2, p['norm_ffn_w'], p['w_out_top'], p['w_out_bot'])
    s2t, thr, r = _router(h2, p['wq_t'], p['subkeys'], tp=tp)
    y = _dense(h2, p['peer_u'], p['peer_vt'], s2t, thr, r, x1, g2, fw, tp=tp)
    return y, (new_lru_conv, new_h, new_dn_conv, new_s)


def kernel(x_prompt, x_sample, state_lru_conv, state_lru_h, state_dn_conv, state_dn_S, c_prompt, c_sample,
           w_ada, b_ada, norm_mix_w, norm_ffn_w, w_in, lru_conv_w, lru_conv_b, lru_wa, lru_ba, lru_wx, lru_bx,
           lru_lambda, dn_conv_w, dn_A_log, dn_dt_bias, dn_norm_w, w_out, peer_wq, peer_subkeys, peer_u, peer_v,
           final_norm_w):
    depth = w_ada.shape[0]
    assert depth == 1, "the final norm is fused into the last layer's expert kernel"
    bp, seq, d = x_prompt.shape
    bs, dseq, _ = x_sample.shape
    lw = lru_lambda.shape[-1]
    dn_heads = dn_A_log.shape[-1]
    dn_w = dn_heads * LANES
    conv_ch = dn_conv_w.shape[-1]
    assert lru_wa.shape[-1] == LANES and lw == dn_w and conv_ch == 3 * dn_w
    assert peer_subkeys.shape[-1] == LANES and peer_subkeys.shape[-2] == LANES
    main = 2 * lw + conv_ch + dn_w
    l = 0
    w_small = jnp.zeros((d, LANES), F32).at[:, :2 * dn_heads].set(w_in[l][:, main:]).astype(BF16)
    nheads = peer_subkeys.shape[1]
    p = {
        'norm_mix_w': norm_mix_w[l].reshape(1, 1, d), 'norm_ffn_w': norm_ffn_w[l].reshape(1, 1, d),
        'w_main': w_in[l][:, :main].astype(BF16), 'w_small': w_small,
        'lru_conv_w': lru_conv_w[l], 'lru_conv_b': lru_conv_b[l],
        'lru_wg': jnp.concatenate([lru_wa[l], lru_wx[l]], axis=-1).astype(BF16),
        'lru_ba': lru_ba[l], 'lru_bx': lru_bx[l], 'lru_lambda': lru_lambda[l],
        'dn_conv_w': dn_conv_w[l], 'dn_A_log': dn_A_log[l], 'dn_dt_bias': dn_dt_bias[l], 'dn_norm_w': dn_norm_w[l],
        'w_out_top': w_out[l][:lw].astype(BF16), 'w_out_bot': w_out[l][lw:].astype(BF16),
        'wq_t': peer_wq[l].T.astype(BF16),
        'subkeys': peer_subkeys[l].reshape(nheads * 2, LANES, LANES).astype(BF16),
        'peer_u': peer_u[l].astype(BF16),
        'peer_vt': peer_v[l].reshape(-1, PEER_CHUNK, d).transpose(0, 2, 1).astype(BF16),
    }
    fw = final_norm_w.reshape(1, 1, d)
    mod = _ada(jnp.concatenate([c_prompt, c_sample], axis=0), w_ada[l], b_ada[l])
    zeros = lambda *s: jnp.zeros(s, F32)
    yp, sp = _group(x_prompt, mod[:bp], zeros(bp, CONV_W - 1, lw), zeros(bp, lw), zeros(bp, CONV_W - 1, conv_ch),
                    zeros(bp, dn_heads, LANES, LANES), True, p, fw,
                    lru_tiles=(bp, _tile(seq, 128, 8)), dn_tiles=(_tile(bp, 2, 1), _tile(seq, 256, DN_CHUNK), min(DN_CHUNK, seq)),
                    tp=_tile(bp * seq, 512, LANES))
    ys, ss = _group(x_sample, mod[bp:], state_lru_conv[l], state_lru_h[l], state_dn_conv[l], state_dn_S[l],
                    False, p, fw,
                    lru_tiles=(_tile(bs, 32, 1), dseq), dn_tiles=(_tile(bs, DN_CHUNK // dseq, 1), dseq, dseq),
                    tp=_tile(bs * dseq, 512, LANES))
    stack = lambda v: v[None]
    return (yp, ys, stack(sp[0]), stack(sp[1]), stack(sp[2]), stack(sp[3]),
            stack(ss[0]), stack(ss[1]), stack(ss[2]), stack(ss[3]))
```

```python
import functools
import math

import jax
import jax.numpy as jnp
from jax import lax
from jax.experimental import pallas as pl
from jax.experimental.pallas import tpu as pltpu

F32 = jnp.float32
BF16 = jnp.bfloat16
NORM_EPS = 1e-6
LRU_C = 8.0
CONV_W = 4
CONV_PAD = 8
PEER_TOPK = 16
PEER_CHUNK = 512
DN_CHUNK = 64
LOG2E = math.log2(math.e)
LANES = 128
VMEM_LIMIT_BYTES = 60 * 1024 * 1024
NT_DIMS = (((1,), (1,)), ((), ()))
TN_DIMS = (((0,), (0,)), ((), ()))


def _params(*semantics):
    return pltpu.CompilerParams(dimension_semantics=semantics, vmem_limit_bytes=VMEM_LIMIT_BYTES)


def _tile(n, target, multiple):
    if n <= target:
        return n
    t = (target // multiple) * multiple
    while t >= multiple:
        if n % t == 0:
            return t
        t -= multiple
    raise ValueError(f"no tile for {n} <= {target} in multiples of {multiple}")


def _softplus(x):
    return jnp.maximum(x, 0.0) + jnp.log1p(jnp.exp(-jnp.abs(x)))


def _row_groups(batch, seq, tm):
    r = min(seq, tm)
    assert seq % r == 0 and tm % r == 0 and (batch * seq) % tm == 0
    return tm // r, r, seq // r


def _ada_kernel(c_ref, w_ref, b_ref, o_ref):
    c = c_ref[...]
    a = (c * jax.nn.sigmoid(c)).astype(BF16)
    o_ref[...] = jnp.dot(a, w_ref[...].astype(BF16), preferred_element_type=F32) + b_ref[...]


def _ada(c, w, b):
    bc, d = c.shape
    n = w.shape[1]
    tn = _tile(n, 1024, LANES)
    return pl.pallas_call(
        _ada_kernel,
        out_shape=jax.ShapeDtypeStruct((bc, n), F32),
        grid=(n // tn,),
        in_specs=[pl.BlockSpec((bc, d), lambda j: (0, 0)),
                  pl.BlockSpec((d, tn), lambda j: (0, j)),
                  pl.BlockSpec((1, tn), lambda j: (0, j))],
        out_specs=pl.BlockSpec((bc, tn), lambda j: (0, j)),
        compiler_params=_params("arbitrary"),
        name="ada",
    )(c, w, b.reshape(1, n))


def _inproj_kernel(x_ref, sc_ref, sh_ref, nw_ref, wm_ref, ws_ref, om_ref, os_ref, h_ref):
    @pl.when(pl.program_id(1) == 0)
    def _():
        x = x_ref[...]
        y = x * lax.rsqrt(jnp.mean(x * x, axis=-1, keepdims=True) + NORM_EPS) * nw_ref[...]
        h = y * (1.0 + sc_ref[...]) + sh_ref[...]
        hb = h.reshape(h_ref.shape).astype(BF16)
        h_ref[...] = hb
        os_ref[...] = jnp.dot(hb, ws_ref[...], preferred_element_type=F32)

    om_ref[...] = jnp.dot(h_ref[...], wm_ref[...], preferred_element_type=F32)


def _inproj(x, sc, sh, nw, wm, ws):
    b, t, d = x.shape
    n = b * t
    m = wm.shape[1]
    tm = _tile(n, 512, 8)
    tn = _tile(m, 3072, LANES)
    gb, r, nt = _row_groups(b, t, tm)
    return pl.pallas_call(
        _inproj_kernel,
        out_shape=(jax.ShapeDtypeStruct((n, m), F32), jax.ShapeDtypeStruct((n, LANES), F32)),
        grid=(n // tm, m // tn),
        in_specs=[pl.BlockSpec((gb, r, d), lambda i, j: (i // nt, i % nt, 0)),
                  pl.BlockSpec((gb, 1, d), lambda i, j: (i // nt, 0, 0)),
                  pl.BlockSpec((gb, 1, d), lambda i, j: (i // nt, 0, 0)),
                  pl.BlockSpec((1, 1, d), lambda i, j: (0, 0, 0)),
                  pl.BlockSpec((d, tn), lambda i, j: (0, j)),
                  pl.BlockSpec((d, LANES), lambda i, j: (0, 0))],
        out_specs=(pl.BlockSpec((tm, tn), lambda i, j: (i, j)),
                   pl.BlockSpec((tm, LANES), lambda i, j: (i, 0))),
        scratch_shapes=[pltpu.VMEM((tm, d), BF16)],
        compiler_params=_params("arbitrary", "arbitrary"),
        name="inproj",
    )(x, sc, sh, nw, wm, ws)


def _causal_conv(x_ref, xs_ref, cw_ref):
    t = x_ref.shape[1]
    xs_ref[:, CONV_PAD:CONV_PAD + t, :] = x_ref[...]
    cw = cw_ref[...]
    y = None
    for k in range(CONV_W):
        lo = CONV_PAD - (CONV_W - 1) + k
        term = xs_ref[:, lo:lo + t, :] * cw[k:k + 1, :][None]
        y = term if y is None else y + term
    xs_ref[:, CONV_PAD - (CONV_W - 1):CONV_PAD, :] = xs_ref[:, t + CONV_PAD - (CONV_W - 1):t + CONV_PAD, :]
    return y


def _lru_kernel(xl_ref, yl_ref, cs_ref, h0_ref, cw_ref, cb_ref, wg_ref, ba_ref, bx_ref, lam_ref,
                o_ref, hn_ref, xs_ref, a_ref, u_ref, h_ref, *, reset_first):
    ti = pl.program_id(1)
    bb, tc, lw = xl_ref.shape
    nheads = lw // LANES

    @pl.when(ti == 0)
    def _():
        xs_ref[:, CONV_PAD - (CONV_W - 1):CONV_PAD, :] = cs_ref[...]
        h_ref[...] = h0_ref[...]

    xc = (_causal_conv(xl_ref, xs_ref, cw_ref) + cb_ref[...][None]).reshape(bb * tc, lw)
    sp = _softplus(-lam_ref[...])
    t_glob = lax.broadcasted_iota(jnp.int32, (bb, tc, LANES), 1) + ti * tc
    for hd in range(nheads):
        sl = slice(hd * LANES, (hd + 1) * LANES)
        xh = xc[:, sl]
        gates = jnp.dot(xh.astype(BF16), wg_ref[hd], preferred_element_type=F32)
        r = jax.nn.sigmoid(gates[:, :LANES] + ba_ref[:, sl])
        i = jax.nn.sigmoid(gates[:, LANES:] + bx_ref[:, sl])
        log_a = -LRU_C * r * sp[:, sl]
        a = jnp.exp(log_a)
        mult = jnp.sqrt(jnp.maximum(-jnp.tanh(log_a) * (a * a + 1.0), 0.0))
        u = i * xh
        a3 = a.reshape(bb, tc, LANES)
        m3 = mult.reshape(bb, tc, LANES)
        if reset_first:
            m3 = jnp.where(t_glob == 0, 1.0, m3)
        a_ref[:, :, sl] = a3
        u_ref[:, :, sl] = m3 * u.reshape(bb, tc, LANES)

    def step(t, h):
        h = a_ref[:, pl.ds(t, 1), :] * h + u_ref[:, pl.ds(t, 1), :]
        u_ref[:, pl.ds(t, 1), :] = h
        return h

    h = lax.fori_loop(0, tc, step, h_ref[...], unroll=8)
    h_ref[...] = h
    hn_ref[...] = h
    o_ref[...] = u_ref[...] * jax.nn.gelu(yl_ref[...])


def _lru(pm3, conv_state, h0, cw, cb, wg, ba, bx, lam, *, reset_first, bb, tc):
    b, t, _ = pm3.shape
    lw = lam.shape[-1]
    assert b % bb == 0 and t % tc == 0 and tc % 8 == 0
    row = lambda v: v.reshape(1, lw)
    kern = functools.partial(_lru_kernel, reset_first=reset_first)
    out, hn = pl.pallas_call(
        kern,
        out_shape=(jax.ShapeDtypeStruct((b, t, lw), F32), jax.ShapeDtypeStruct((b, 1, lw), F32)),
        grid=(b // bb, t // tc),
        in_specs=[pl.BlockSpec((bb, tc, lw), lambda i, j: (i, j, 0)),
                  pl.BlockSpec((bb, tc, lw), lambda i, j: (i, j, 1)),
                  pl.BlockSpec((bb, CONV_W - 1, lw), lambda i, j: (i, 0, 0)),
                  pl.BlockSpec((bb, 1, lw), lambda i, j: (i, 0, 0)),
                  pl.BlockSpec((CONV_W, lw), lambda i, j: (0, 0)),
                  pl.BlockSpec((1, lw), lambda i, j: (0, 0)),
                  pl.BlockSpec(wg.shape, lambda i, j: (0, 0, 0)),
                  pl.BlockSpec((1, lw), lambda i, j: (0, 0)),
                  pl.BlockSpec((1, lw), lambda i, j: (0, 0)),
                  pl.BlockSpec((1, lw), lambda i, j: (0, 0))],
        out_specs=(pl.BlockSpec((bb, tc, lw), lambda i, j: (i, j, 0)),
                   pl.BlockSpec((bb, 1, lw), lambda i, j: (i, 0, 0))),
        scratch_shapes=[pltpu.VMEM((bb, tc + CONV_PAD, lw), F32),
                        pltpu.VMEM((bb, tc, lw), F32),
                        pltpu.VMEM((bb, tc, lw), F32),
                        pltpu.VMEM((bb, 1, lw), F32)],
        compiler_params=_params("arbitrary", "arbitrary"),
        name="lru",
    )(pm3, pm3, conv_state, h0.reshape(b, 1, lw), cw, row(cb), wg, row(ba), row(bx), row(lam))
    return out, hn.reshape(b, lw)


def _delta_kernel(q_ref, k_ref, v_ref, z_ref, ps_ref, csq_ref, csk_ref, csv_ref, s0_ref,
                  cwq_ref, cwk_ref, cwv_ref, ea_ref, dtb_ref, nw_ref,
                  o_ref, sn_ref,
                  xq_ref, xk_ref, xv_ref, qn_ref, kn_ref, vn_ref, beta_ref, g_ref, *, chunk, nheads):
    ti = pl.program_id(1)
    bb, tb, w = q_ref.shape
    n_chunks = tb // chunk
    stacked = n_chunks == 1
    units = 1 if stacked else bb
    segs = bb if stacked else 1
    rows = segs * chunk
    lo = CONV_PAD - (CONV_W - 1)

    @pl.when(ti == 0)
    def _():
        xq_ref[:, lo:CONV_PAD, :] = csq_ref[...]
        xk_ref[:, lo:CONV_PAD, :] = csk_ref[...]
        xv_ref[:, lo:CONV_PAD, :] = csv_ref[...]
        sn_ref[...] = s0_ref[...]

    def conv_silu(x_ref, xs_ref, cw_ref):
        y = _causal_conv(x_ref, xs_ref, cw_ref)
        return (y * jax.nn.sigmoid(y)).reshape(bb * tb, w)

    qc = conv_silu(q_ref, xq_ref, cwq_ref)
    kc = conv_silu(k_ref, xk_ref, cwk_ref)
    vn_ref[...] = conv_silu(v_ref, xv_ref, cwv_ref)
    for hd in range(nheads):
        sl = slice(hd * LANES, (hd + 1) * LANES)
        qh = qc[:, sl]
        kh = kc[:, sl]
        qn_ref[:, sl] = qh * (lax.rsqrt(jnp.sum(qh * qh, axis=-1, keepdims=True) + NORM_EPS) * (LANES ** -0.5))
        kn_ref[:, sl] = kh * lax.rsqrt(jnp.sum(kh * kh, axis=-1, keepdims=True) + NORM_EPS)
    ps = ps_ref[...].reshape(bb * tb, LANES)
    beta_ref[...] = jax.nn.sigmoid(ps)
    g_ref[...] = -ea_ref[...] * _softplus(ps + dtb_ref[...])

    ii = lax.broadcasted_iota(jnp.int32, (rows, rows), 0)
    jj = lax.broadcasted_iota(jnp.int32, (rows, rows), 1)
    if segs == 1:
        le = jj <= ii
        lt = jj < ii
    else:
        same = (ii // chunk) == (jj // chunk)
        le = jnp.logical_and(same, jj <= ii)
        lt = jnp.logical_and(same, jj < ii)
    le_b = jnp.where(le, 1.0, 0.0).astype(BF16)
    assert chunk & (chunk - 1) == 0 and chunk >= 2
    levels = []
    b = 1
    while b < chunk:
        sh = b.bit_length()
        join = jnp.logical_and((ii >> sh) == (jj >> sh), (ii >> (sh - 1)) != (jj >> (sh - 1)))
        levels.append(jnp.logical_and(join, lt))
        b *= 2
    lanes = [slice(hd * LANES, (hd + 1) * LANES) for hd in range(nheads)]
    probs = [(u, hd) for u in range(units) for hd in range(nheads)]

    def mm(a, b):
        return jnp.dot(a.astype(BF16), b.astype(BF16), preferred_element_type=F32)

    def do_chunk(c, carry):
        rsl, gc, gct, eg, beta = [], [], [], [], []
        for u in range(units):
            rs = pl.ds(pl.multiple_of(u * tb + c * rows, rows), rows)
            g = g_ref[rs, :]
            g_hi = g.astype(BF16)
            g_lo = (g - g_hi.astype(F32)).astype(BF16)
            gcu = (jnp.dot(le_b, g_hi, preferred_element_type=F32)
                   + jnp.dot(le_b, g_lo, preferred_element_type=F32))
            gcp = gcu if rows == LANES else jnp.concatenate([gcu, jnp.zeros((LANES - rows, LANES), F32)], axis=0)
            rsl.append(rs)
            gc.append(gcu)
            gct.append(gcp.T)
            eg.append(jnp.exp(gcu))
            beta.append(beta_ref[rs, :])
        q = [qn_ref[rsl[u], lanes[hd]] for u, hd in probs]
        k = [kn_ref[rsl[u], lanes[hd]] for u, hd in probs]
        v = [vn_ref[rsl[u], lanes[hd]] for u, hd in probs]
        bcol = [beta[u][:, hd:hd + 1] for u, hd in probs]
        gcol = [gc[u][:, nheads + hd:nheads + hd + 1] for u, hd in probs]
        egc = [eg[u][:, nheads + hd:nheads + hd + 1] for u, hd in probs]
        np_ = range(len(probs))
        decay = [jnp.where(le, jnp.exp(jnp.where(le, gcol[i] - gct[u][nheads + hd:nheads + hd + 1, :rows], 0.0)), 0.0)
                 for i, (u, hd) in enumerate(probs)]
        kb = [k[i] * bcol[i] for i in np_]
        k16 = [k[i].astype(BF16) for i in np_]
        kk = [lax.dot_general(kb[i].astype(BF16), k16[i], NT_DIMS, preferred_element_type=F32) for i in np_]
        qk = [lax.dot_general(q[i].astype(BF16), k16[i], NT_DIMS, preferred_element_type=F32) for i in np_]
        lmat = [jnp.where(lt, kk[i] * decay[i], 0.0) for i in np_]
        attn = [jnp.where(le, qk[i] * decay[i], 0.0).astype(BF16) for i in np_]
        nmat = [jnp.where(levels[0], -lmat[i], 0.0) for i in np_]
        for lvl in levels[1:]:
            off = [jnp.where(lvl, lmat[i], 0.0) for i in np_]
            m = [off[i] + mm(nmat[i], off[i]) for i in np_]
            nmat = [nmat[i] - m[i] - mm(m[i], nmat[i]) for i in np_]
        rhs = [jnp.concatenate([v[i] * bcol[i], kb[i] * egc[i]], axis=1) for i in np_]
        sol = [rhs[i] + mm(nmat[i], rhs[i]) for i in np_]
        qe = [q[i] * egc[i] for i in np_]
        seq = lambda u, b: b if stacked else u
        s_old = [[sn_ref[seq(u, b), hd] for b in range(segs)] for u, hd in probs]
        both = [[jnp.dot(jnp.concatenate([sol[i][b * chunk:(b + 1) * chunk, LANES:],
                                          qe[i][b * chunk:(b + 1) * chunk]], axis=0).astype(BF16),
                         s_old[i][b].astype(BF16), preferred_element_type=F32) for b in range(segs)] for i in np_]
        cat = lambda parts: parts[0] if segs == 1 else jnp.concatenate(parts, axis=0)
        v16 = [(sol[i][:, :LANES] - cat([both[i][b][:chunk] for b in range(segs)])).astype(BF16) for i in np_]
        o = [cat([both[i][b][chunk:] for b in range(segs)]) + jnp.dot(attn[i], v16[i], preferred_element_type=F32)
             for i in np_]
        for i, (u, hd) in enumerate(probs):
            for b in range(segs):
                seg = slice(b * chunk, (b + 1) * chunk)
                g_last = gcol[i][(b + 1) * chunk - 1:(b + 1) * chunk]
                kd = (k[i][seg] * jnp.exp(g_last - gcol[i][seg])).astype(BF16)
                sn_ref[seq(u, b), hd] = s_old[i][b] * jnp.exp(g_last) + lax.dot_general(
                    kd, v16[i][seg], TN_DIMS, preferred_element_type=F32)
        for i, (u, hd) in enumerate(probs):
            sl = lanes[hd]
            on = o[i] * lax.rsqrt(jnp.mean(o[i] * o[i], axis=-1, keepdims=True) + NORM_EPS) * nw_ref[...]
            if stacked:
                zz = z_ref[:, :, sl].reshape(rows, LANES)
                o_ref[:, :, sl] = (on * (zz * jax.nn.sigmoid(zz))).reshape(bb, chunk, LANES)
            else:
                tsl = pl.ds(pl.multiple_of(c * chunk, chunk), chunk)
                zz = z_ref[u, tsl, sl]
                o_ref[u, tsl, sl] = on * (zz * jax.nn.sigmoid(zz))
        return carry

    if n_chunks == 1:
        do_chunk(0, 0)
    else:
        lax.fori_loop(0, n_chunks, do_chunk, 0)


def _delta(pm3, ps3, conv_state, s0, cw, a_log, dt_bias, norm_w, *, bb, tb, chunk):
    b, t, _ = pm3.shape
    nheads = a_log.shape[-1]
    w = nheads * LANES
    assert s0.shape == (b, nheads, LANES, LANES) and nheads <= 8
    assert b % bb == 0 and t % tb == 0 and tb % chunk == 0 and chunk % 8 == 0
    assert tb > chunk or bb * chunk <= LANES
    pad = lambda v: jnp.zeros((1, LANES), F32).at[0, nheads:2 * nheads].set(v)
    kern = functools.partial(_delta_kernel, chunk=chunk, nheads=nheads)
    col = lambda c: pl.BlockSpec((bb, tb, w), lambda i, j, c=c: (i, j, c))
    cst = lambda c: pl.BlockSpec((bb, CONV_W - 1, w), lambda i, j, c=c: (i, 0, c))
    cwt = lambda c: pl.BlockSpec((CONV_W, w), lambda i, j, c=c: (0, c))
    vec = pl.BlockSpec((1, LANES), lambda i, j: (0, 0))
    sblk = pl.BlockSpec((bb, nheads, LANES, LANES), lambda i, j: (i, 0, 0, 0))
    out, sn = pl.pallas_call(
        kern,
        out_shape=(jax.ShapeDtypeStruct((b, t, w), F32), jax.ShapeDtypeStruct(s0.shape, F32)),
        grid=(b // bb, t // tb),
        in_specs=[col(2), col(3), col(4), col(5),
                  pl.BlockSpec((bb, tb, LANES), lambda i, j: (i, j, 0)),
                  cst(0), cst(1), cst(2), sblk, cwt(0), cwt(1), cwt(2), vec, vec, vec],
        out_specs=(pl.BlockSpec((bb, tb, w), lambda i, j: (i, j, 0)), sblk),
        scratch_shapes=[pltpu.VMEM((bb, tb + CONV_PAD, w), F32)] * 3
                       + [pltpu.VMEM((bb * tb, w), F32)] * 3
                       + [pltpu.VMEM((bb * tb, LANES), F32)] * 2,
        compiler_params=_params("arbitrary", "arbitrary"),
        name="delta",
    )(pm3, pm3, pm3, pm3, ps3, conv_state, conv_state, conv_state, s0, cw, cw, cw,
      pad(jnp.exp(a_log.astype(F32))), pad(dt_bias.astype(F32)), norm_w.reshape(1, LANES))
    return out, sn


def _outproj_kernel(ol_ref, od_ref, x_ref, g1_ref, sc_ref, sh_ref, nw_ref, wt_ref, wb_ref, x1_ref, h2_ref):
    tm = h2_ref.shape[0]
    a = ol_ref[...].reshape(tm, ol_ref.shape[-1]).astype(BF16)
    b = od_ref[...].reshape(tm, od_ref.shape[-1]).astype(BF16)
    mix = (jnp.dot(a, wt_ref[...], preferred_element_type=F32)
           + jnp.dot(b, wb_ref[...], preferred_element_type=F32))
    x1 = x_ref[...] + g1_ref[...] * mix.reshape(x_ref.shape)
    x1_ref[...] = x1
    y = x1 * lax.rsqrt(jnp.mean(x1 * x1, axis=-1, keepdims=True) + NORM_EPS) * nw_ref[...]
    h2_ref[...] = (y * (1.0 + sc_ref[...]) + sh_ref[...]).reshape(h2_ref.shape).astype(BF16)


def _outproj(ol, od, x, g1, sc, sh, nw, wt, wb):
    b, t, d = x.shape
    n = b * t
    lw, dw = ol.shape[-1], od.shape[-1]
    tm = _tile(n, 512, 16)
    gb, r, nt = _row_groups(b, t, tm)
    tok = lambda c: pl.BlockSpec((gb, r, c), lambda i: (i // nt, i % nt, 0))
    per = pl.BlockSpec((gb, 1, d), lambda i: (i // nt, 0, 0))
    return pl.pallas_call(
        _outproj_kernel,
        out_shape=(jax.ShapeDtypeStruct((b, t, d), F32), jax.ShapeDtypeStruct((n, d), BF16)),
        grid=(n // tm,),
        in_specs=[tok(lw), tok(dw), tok(d), per, per, per,
                  pl.BlockSpec((1, 1, d), lambda i: (0, 0, 0)),
                  pl.BlockSpec((lw, d), lambda i: (0, 0)),
                  pl.BlockSpec((dw, d), lambda i: (0, 0))],
        out_specs=(tok(d), pl.BlockSpec((tm, d), lambda i: (i, 0))),
        compiler_params=_params("arbitrary"),
        name="outproj",
    )(ol, od, x, g1, sc, sh, nw, wt, wb)


def _sorting_network(n):
    pairs = []
    p = 1
    while p < n:
        k = p
        while k >= 1:
            for j in range(k % p, n - k, 2 * k):
                for i in range(min(k, n - j - k)):
                    if (i + j) // (2 * p) == (i + j + k) // (2 * p):
                        pairs.append((i + j, i + j + k))
            k //= 2
        p *= 2
    return pairs


def _merge_top(lists, singles, count):
    lists = list(lists)
    singles = list(singles)
    vals = []
    for r in range(count):
        head = lists[0]
        for x in singles:
            head = jnp.maximum(head, x)
        m = jnp.max(head, axis=0, keepdims=True)
        vals.append(m)
        left = count - r - 1
        if left == 0:
            break
        eq = lists[0] == m
        keep = min(len(lists), left)
        lists = [jnp.where(eq, lists[v + 1] if v + 1 < len(lists) else -jnp.inf, lists[v]) for v in range(keep)]
        singles = [jnp.where(x == m, -jnp.inf, x) for x in singles]
    return vals


def _top_values(s, count):
    tiles = [s[8 * v:8 * (v + 1)] for v in range(s.shape[0] // 8)]
    for i, j in _sorting_network(len(tiles)):
        tiles[i], tiles[j] = jnp.maximum(tiles[i], tiles[j]), jnp.minimum(tiles[i], tiles[j])
    return _merge_top(tiles, [], count)


def _router_kernel(h_ref, wq_ref, sk_ref, s2_ref, thr_ref, r_ref, q_ref, *, nheads):
    nk = sk_ref.shape[1]
    half = sk_ref.shape[2]
    q_ref[...] = lax.dot_general(wq_ref[...], h_ref[...], NT_DIMS, preferred_element_type=F32)
    p = h_ref.shape[0]
    sub = lax.broadcasted_iota(jnp.int32, (8, p), 0)

    def head(hd, carry):
        base = pl.multiple_of(hd * (2 * half), 2 * half)
        q1 = q_ref[pl.ds(base, half), :].astype(BF16)
        q2 = q_ref[pl.ds(base + half, half), :].astype(BF16)
        s1 = jnp.dot(sk_ref[2 * hd], q1, preferred_element_type=F32)
        s2 = jnp.dot(sk_ref[2 * hd + 1], q2, preferred_element_type=F32)
        nv = PEER_TOPK + 1
        t1 = _top_values(s1, nv)
        t2 = _top_values(s2, nv)
        t2_lo = jnp.concatenate(t2[:8], axis=0)
        t2_hi = jnp.concatenate(t2[8:16], axis=0)
        lists = [t1[0] + t2_lo]
        for a in range(1, nv):
            nb = nv // (a + 1)
            c = t1[a] + t2_lo
            lists.append(c if nb >= 8 else jnp.where(sub < nb, c, -jnp.inf))
        singles = [t1[0] + t2_hi, jnp.where(sub < 1, t1[0] + t2[16], -jnp.inf)]
        cand = jnp.concatenate(lists + singles, axis=0)
        best = _merge_top(lists, singles, nv)
        tau = best[PEER_TOPK - 1]
        mid = 0.5 * (tau + best[PEER_TOPK])
        top = t1[0] + t2[0]
        z = jnp.sum(jnp.where(cand >= tau, jnp.exp(cand - top), 0.0), axis=0, keepdims=True)
        s2_ref[hd] = s2 * LOG2E
        thr_ref[hd] = (mid - s1) * LOG2E
        r_ref[hd] = (s1 - (top + jnp.log(z))) * LOG2E
        return carry

    lax.fori_loop(0, nheads, head, 0)


def _router(h2, wq_t, sk, *, tp):
    n, d = h2.shape
    nheads = sk.shape[0] // 2
    nk = sk.shape[1]
    hq = wq_t.shape[0]
    out = jax.ShapeDtypeStruct((nheads, nk, n), F32)
    oblk = pl.BlockSpec((nheads, nk, tp), lambda i: (0, 0, i))
    return pl.pallas_call(
        functools.partial(_router_kernel, nheads=nheads),
        out_shape=(out, out, out),
        grid=(n // tp,),
        in_specs=[pl.BlockSpec((tp, d), lambda i: (i, 0)),
                  pl.BlockSpec((hq, d), lambda i: (0, 0)),
                  pl.BlockSpec(sk.shape, lambda i: (0, 0, 0))],
        out_specs=(oblk, oblk, oblk),
        scratch_shapes=[pltpu.VMEM((hq, tp), F32)],
        compiler_params=_params("arbitrary"),
        name="router",
    )(h2, wq_t, sk)


def _dense_kernel(h_ref, u_ref, vt_ref, s2_ref, thr_ref, r_ref, x1_ref, g2_ref, fw_ref,
                  y_ref, acc_ref, st_ref, coef_ref, *, nheads, rows_per_step, n_chunks):
    s = pl.program_id(1)
    nk = s2_ref.shape[1]
    tp = h_ref.shape[0]

    @pl.when(s == 0)
    def _():
        acc_ref[...] = jnp.zeros_like(acc_ref)
        st_ref[...] = jnp.zeros_like(st_ref)

    cur = s % 2
    prev = 1 - cur
    chunk = jnp.clip(s - 1, 0, n_chunks - 1)
    for ii in range(rows_per_step):
        i1 = chunk * rows_per_step + ii
        rows = slice(ii * nk, (ii + 1) * nk)
        thr_rows = [thr_ref[hd, pl.ds(i1, 1), :] for hd in range(nheads)]
        r_rows = [r_ref[hd, pl.ds(i1, 1), :] for hd in range(nheads)]
        for pb in range(tp // LANES):
            cols = slice(pb * LANES, (pb + 1) * LANES)
            gate = None
            for hd in range(nheads):
                s2 = s2_ref[hd, :, cols]
                term = jnp.where(s2 >= thr_rows[hd][:, cols], jnp.exp2(s2 + r_rows[hd][:, cols]), 0.0)
                gate = term if gate is None else gate + term
            coef_ref[rows, cols] = (jax.nn.gelu(st_ref[prev, rows, cols]) * gate).astype(BF16)
    acc_ref[...] += jnp.dot(vt_ref[0], coef_ref[...], preferred_element_type=F32)
    st_ref[cur] = lax.dot_general(u_ref[...], h_ref[...], NT_DIMS, preferred_element_type=F32)

    @pl.when(s == pl.num_programs(1) - 1)
    def _():
        peer = acc_ref[...].T.reshape(x1_ref.shape)
        x2 = x1_ref[...] + g2_ref[...] * peer
        y_ref[...] = x2 * lax.rsqrt(jnp.mean(x2 * x2, axis=-1, keepdims=True) + NORM_EPS) * fw_ref[...]


def _dense(h2, u16, vt16, s2t, thr, rl, x1, g2, fw, *, tp):
    b, t, d = x1.shape
    n = b * t
    nheads, nk, _ = s2t.shape
    n_chunks, _, ce = vt16.shape
    rows_per_step = ce // nk
    assert u16.shape[0] == n_chunks * ce
    gb, r, nt = _row_groups(b, t, tp)
    rblk = pl.BlockSpec((nheads, nk, tp), lambda i, s: (0, 0, i))
    tok = pl.BlockSpec((gb, r, d), lambda i, s: (i // nt, i % nt, 0))
    return pl.pallas_call(
        functools.partial(_dense_kernel, nheads=nheads, rows_per_step=rows_per_step, n_chunks=n_chunks),
        out_shape=jax.ShapeDtypeStruct((b, t, d), F32),
        grid=(n // tp, n_chunks + 1),
        in_specs=[pl.BlockSpec((tp, d), lambda i, s: (i, 0)),
                  pl.BlockSpec((ce, d), lambda i, s: (jnp.minimum(s, n_chunks - 1), 0)),
                  pl.BlockSpec((1, d, ce), lambda i, s: (jnp.clip(s - 1, 0, n_chunks - 1), 0, 0)),
                  rblk, rblk, rblk, tok,
                  pl.BlockSpec((gb, 1, d), lambda i, s: (i // nt, 0, 0)),
                  pl.BlockSpec((1, 1, d), lambda i, s: (0, 0, 0))],
        out_specs=tok,
        scratch_shapes=[pltpu.VMEM((d, tp), F32), pltpu.VMEM((2, ce, tp), F32), pltpu.VMEM((ce, tp), BF16)],
        compiler_params=_params("arbitrary", "arbitrary"),
        name="dense",
    )(h2, u16, vt16, s2t, thr, rl, x1, g2, fw)


def _group(x, mod, lru_conv, lru_h, dn_conv, dn_s, reset_first, p, fw, *, lru_tiles, dn_tiles, tp):
    b, t, d = x.shape
    sh1, sc1, g1, sh2, sc2, g2 = [m.reshape(b, 1, d) for m in jnp.split(mod, 6, axis=-1)]
    pm, ps = _inproj(x, sc1, sh1, p['norm_mix_w'], p['w_main'], p['w_small'])
    pm3 = pm.reshape(b, t, -1)
    ps3 = ps.reshape(b, t, LANES)
    lw = p['lru_lambda'].shape[-1]
    out_lru, new_h = _lru(pm3, lru_conv, lru_h, p['lru_conv_w'], p['lru_conv_b'], p['lru_wg'], p['lru_ba'],
                          p['lru_bx'], p['lru_lambda'], reset_first=reset_first, bb=lru_tiles[0], tc=lru_tiles[1])
    out_dn, new_s = _delta(pm3, ps3, dn_conv, dn_s, p['dn_conv_w'], p['dn_A_log'], p['dn_dt_bias'], p['dn_norm_w'],
                           bb=dn_tiles[0], tb=dn_tiles[1], chunk=dn_tiles[2])
    keep = CONV_W - 1
    new_lru_conv = pm3[:, t - keep:, :lw]
    new_dn_conv = pm3[:, t - keep:, 2 * lw:2 * lw + dn_conv.shape[-1]]
    x1, h2 = _outproj(out_lru, out_dn, x, g1, sc2, sh2, p['norm_ffn_w'], p['w_out_top'], p['w_out_bot'])
    s2t, thr, r = _router(h2, p['wq_t'], p['subkeys'], tp=tp)
    y = _dense(h2, p['peer_u'], p['peer_vt'], s2t, thr, r, x1, g2, fw, tp=tp)
    return y, (new_lru_conv, new_h, new_dn_conv, new_s)


def kernel(x_prompt, x_sample, state_lru_conv, state_lru_h, state_dn_conv, state_dn_S, c_prompt, c_sample,
           w_ada, b_ada, norm_mix_w, norm_ffn_w, w_in, lru_conv_w, lru_conv_b, lru_wa, lru_ba, lru_wx, lru_bx,
           lru_lambda, dn_conv_w, dn_A_log, dn_dt_bias, dn_norm_w, w_out, peer_wq, peer_subkeys, peer_u, peer_v,
           final_norm_w):
    depth = w_ada.shape[0]
    assert depth == 1, "the final norm is fused into the last layer's expert kernel"
    bp, seq, d = x_prompt.shape
    bs, dseq, _ = x_sample.shape
    lw = lru_lambda.shape[-1]
    dn_heads = dn_A_log.shape[-1]
    dn_w = dn_heads * LANES
    conv_ch = dn_conv_w.shape[-1]
    assert lru_wa.shape[-1] == LANES and lw == dn_w and conv_ch == 3 * dn_w
    assert peer_subkeys.shape[-1] == LANES and peer_subkeys.shape[-2] == LANES
    main = 2 * lw + conv_ch + dn_w
    l = 0
    w_small = jnp.zeros((d, LANES), F32).at[:, :2 * dn_heads].set(w_in[l][:, main:]).astype(BF16)
    nheads = peer_subkeys.shape[1]
    p = {
        'norm_mix_w': norm_mix_w[l].reshape(1, 1, d), 'norm_ffn_w': norm_ffn_w[l].reshape(1, 1, d),
        'w_main': w_in[l][:, :main].astype(BF16), 'w_small': w_small,
        'lru_conv_w': lru_conv_w[l], 'lru_conv_b': lru_conv_b[l],
        'lru_wg': jnp.concatenate([lru_wa[l], lru_wx[l]], axis=-1).astype(BF16),
        'lru_ba': lru_ba[l], 'lru_bx': lru_bx[l], 'lru_lambda': lru_lambda[l],
        'dn_conv_w': dn_conv_w[l], 'dn_A_log': dn_A_log[l], 'dn_dt_bias': dn_dt_bias[l], 'dn_norm_w': dn_norm_w[l],
        'w_out_top': w_out[l][:lw].astype(BF16), 'w_out_bot': w_out[l][lw:].astype(BF16),
        'wq_t': peer_wq[l].T.astype(BF16),
        'subkeys': peer_subkeys[l].reshape(nheads * 2, LANES, LANES).astype(BF16),
        'peer_u': peer_u[l].astype(BF16),
        'peer_vt': peer_v[l].reshape(-1, PEER_CHUNK, d).transpose(0, 2, 1).astype(BF16),
    }
    fw = final_norm_w.reshape(1, 1, d)
    mod = _ada(jnp.concatenate([c_prompt, c_sample], axis=0), w_ada[l], b_ada[l])
    zeros = lambda *s: jnp.zeros(s, F32)
    yp, sp = _group(x_prompt, mod[:bp], zeros(bp, CONV_W - 1, lw), zeros(bp, lw), zeros(bp, CONV_W - 1, conv_ch),
                    zeros(bp, dn_heads, LANES, LANES), True, p, fw,
                    lru_tiles=(bp, _tile(seq, 128, 8)), dn_tiles=(_tile(bp, 2, 1), _tile(seq, 256, DN_CHUNK), min(DN_CHUNK, seq)),
                    tp=_tile(bp * seq, 512, LANES))
    ys, ss = _group(x_sample, mod[bp:], state_lru_conv[l], state_lru_h[l], state_dn_conv[l], state_dn_S[l],
                    False, p, fw,
                    lru_tiles=(_tile(bs, 32, 1), dseq), dn_tiles=(_tile(bs, DN_CHUNK // dseq, 1), dseq, dseq),
                    tp=_tile(bs * dseq, 512, LANES))
    stack = lambda v: v[None]
    return (yp, ys, stack(sp[0]), stack(sp[1]), stack(sp[2]), stack(sp[3]),
            stack(ss[0]), stack(ss[1]), stack(ss[2]), stack(ss[3]))
```

```python
import functools
import math

import jax
import jax.numpy as jnp
from jax import lax
from jax.experimental import pallas as pl
from jax.experimental.pallas import tpu as pltpu

F32 = jnp.float32
BF16 = jnp.bfloat16
NORM_EPS = 1e-6
LRU_C = 8.0
CONV_W = 4
CONV_PAD = 8
PEER_TOPK = 16
PEER_CHUNK = 512
DN_CHUNK = 64
LOG2E = math.log2(math.e)
LANES = 128
VMEM_LIMIT_BYTES = 60 * 1024 * 1024
NT_DIMS = (((1,), (1,)), ((), ()))
TN_DIMS = (((0,), (0,)), ((), ()))


def _params(*semantics):
    return pltpu.CompilerParams(dimension_semantics=semantics, vmem_limit_bytes=VMEM_LIMIT_BYTES)


def _tile(n, target, multiple):
    if n <= target:
        return n
    t = (target // multiple) * multiple
    while t >= multiple:
        if n % t == 0:
            return t
        t -= multiple
    raise ValueError(f"no tile for {n} <= {target} in multiples of {multiple}")


def _softplus(x):
    return jnp.maximum(x, 0.0) + jnp.log1p(jnp.exp(-jnp.abs(x)))


def _row_groups(batch, seq, tm):
    r = min(seq, tm)
    assert seq % r == 0 and tm % r == 0 and (batch * seq) % tm == 0
    return tm // r, r, seq // r


def _ada_kernel(c_ref, w_ref, b_ref, o_ref):
    c = c_ref[...]
    a = (c * jax.nn.sigmoid(c)).astype(BF16)
    o_ref[...] = jnp.dot(a, w_ref[...].astype(BF16), preferred_element_type=F32) + b_ref[...]


def _ada(c, w, b):
    bc, d = c.shape
    n = w.shape[1]
    tn = _tile(n, 1024, LANES)
    return pl.pallas_call(
        _ada_kernel,
        out_shape=jax.ShapeDtypeStruct((bc, n), F32),
        grid=(n // tn,),
        in_specs=[pl.BlockSpec((bc, d), lambda j: (0, 0)),
                  pl.BlockSpec((d, tn), lambda j: (0, j)),
                  pl.BlockSpec((1, tn), lambda j: (0, j))],
        out_specs=pl.BlockSpec((bc, tn), lambda j: (0, j)),
        compiler_params=_params("arbitrary"),
        name="ada",
    )(c, w, b.reshape(1, n))


def _inproj_kernel(x_ref, sc_ref, sh_ref, nw_ref, wm_ref, ws_ref, om_ref, os_ref, h_ref):
    @pl.when(pl.program_id(1) == 0)
    def _():
        x = x_ref[...]
        y = x * lax.rsqrt(jnp.mean(x * x, axis=-1, keepdims=True) + NORM_EPS) * nw_ref[...]
        h = y * (1.0 + sc_ref[...]) + sh_ref[...]
        hb = h.reshape(h_ref.shape).astype(BF16)
        h_ref[...] = hb
        os_ref[...] = jnp.dot(hb, ws_ref[...], preferred_element_type=F32)

    om_ref[...] = jnp.dot(h_ref[...], wm_ref[...], preferred_element_type=F32)


def _inproj(x, sc, sh, nw, wm, ws):
    b, t, d = x.shape
    n = b * t
    m = wm.shape[1]
    tm = _tile(n, 512, 8)
    tn = _tile(m, 3072, LANES)
    gb, r, nt = _row_groups(b, t, tm)
    return pl.pallas_call(
        _inproj_kernel,
        out_shape=(jax.ShapeDtypeStruct((n, m), F32), jax.ShapeDtypeStruct((n, LANES), F32)),
        grid=(n // tm, m // tn),
        in_specs=[pl.BlockSpec((gb, r, d), lambda i, j: (i // nt, i % nt, 0)),
                  pl.BlockSpec((gb, 1, d), lambda i, j: (i // nt, 0, 0)),
                  pl.BlockSpec((gb, 1, d), lambda i, j: (i // nt, 0, 0)),
                  pl.BlockSpec((1, 1, d), lambda i, j: (0, 0, 0)),
                  pl.BlockSpec((d, tn), lambda i, j: (0, j)),
                  pl.BlockSpec((d, LANES), lambda i, j: (0, 0))],
        out_specs=(pl.BlockSpec((tm, tn), lambda i, j: (i, j)),
                   pl.BlockSpec((tm, LANES), lambda i, j: (i, 0))),
        scratch_shapes=[pltpu.VMEM((tm, d), BF16)],
        compiler_params=_params("arbitrary", "arbitrary"),
        name="inproj",
    )(x, sc, sh, nw, wm, ws)


def _causal_conv(x_ref, xs_ref, cw_ref):
    t = x_ref.shape[1]
    xs_ref[:, CONV_PAD:CONV_PAD + t, :] = x_ref[...]
    cw = cw_ref[...]
    y = None
    for k in range(CONV_W):
        lo = CONV_PAD - (CONV_W - 1) + k
        term = xs_ref[:, lo:lo + t, :] * cw[k:k + 1, :][None]
        y = term if y is None else y + term
    xs_ref[:, CONV_PAD - (CONV_W - 1):CONV_PAD, :] = xs_ref[:, t + CONV_PAD - (CONV_W - 1):t + CONV_PAD, :]
    return y


def _lru_kernel(xl_ref, yl_ref, cs_ref, h0_ref, cw_ref, cb_ref, wg_ref, ba_ref, bx_ref, lam_ref,
                o_ref, hn_ref, xs_ref, a_ref, u_ref, h_ref, *, reset_first):
    ti = pl.program_id(1)
    bb, tc, lw = xl_ref.shape
    nheads = lw // LANES

    @pl.when(ti == 0)
    def _():
        xs_ref[:, CONV_PAD - (CONV_W - 1):CONV_PAD, :] = cs_ref[...]
        h_ref[...] = h0_ref[...]

    xc = (_causal_conv(xl_ref, xs_ref, cw_ref) + cb_ref[...][None]).reshape(bb * tc, lw)
    sp = _softplus(-lam_ref[...])
    t_glob = lax.broadcasted_iota(jnp.int32, (bb, tc, LANES), 1) + ti * tc
    for hd in range(nheads):
        sl = slice(hd * LANES, (hd + 1) * LANES)
        xh = xc[:, sl]
        gates = jnp.dot(xh.astype(BF16), wg_ref[hd], preferred_element_type=F32)
        r = jax.nn.sigmoid(gates[:, :LANES] + ba_ref[:, sl])
        i = jax.nn.sigmoid(gates[:, LANES:] + bx_ref[:, sl])
        log_a = -LRU_C * r * sp[:, sl]
        a = jnp.exp(log_a)
        mult = jnp.sqrt(jnp.maximum(-jnp.tanh(log_a) * (a * a + 1.0), 0.0))
        u = i * xh
        a3 = a.reshape(bb, tc, LANES)
        m3 = mult.reshape(bb, tc, LANES)
        if reset_first:
            m3 = jnp.where(t_glob == 0, 1.0, m3)
        a_ref[:, :, sl] = a3
        u_ref[:, :, sl] = m3 * u.reshape(bb, tc, LANES)

    def step(t, h):
        h = a_ref[:, pl.ds(t, 1), :] * h + u_ref[:, pl.ds(t, 1), :]
        u_ref[:, pl.ds(t, 1), :] = h
        return h

    h = lax.fori_loop(0, tc, step, h_ref[...], unroll=8)
    h_ref[...] = h
    hn_ref[...] = h
    o_ref[...] = u_ref[...] * jax.nn.gelu(yl_ref[...])


def _lru(pm3, conv_state, h0, cw, cb, wg, ba, bx, lam, *, reset_first, bb, tc):
    b, t, _ = pm3.shape
    lw = lam.shape[-1]
    assert b % bb == 0 and t % tc == 0 and tc % 8 == 0
    row = lambda v: v.reshape(1, lw)
    kern = functools.partial(_lru_kernel, reset_first=reset_first)
    out, hn = pl.pallas_call(
        kern,
        out_shape=(jax.ShapeDtypeStruct((b, t, lw), F32), jax.ShapeDtypeStruct((b, 1, lw), F32)),
        grid=(b // bb, t // tc),
        in_specs=[pl.BlockSpec((bb, tc, lw), lambda i, j: (i, j, 0)),
                  pl.BlockSpec((bb, tc, lw), lambda i, j: (i, j, 1)),
                  pl.BlockSpec((bb, CONV_W - 1, lw), lambda i, j: (i, 0, 0)),
                  pl.BlockSpec((bb, 1, lw), lambda i, j: (i, 0, 0)),
                  pl.BlockSpec((CONV_W, lw), lambda i, j: (0, 0)),
                  pl.BlockSpec((1, lw), lambda i, j: (0, 0)),
                  pl.BlockSpec(wg.shape, lambda i, j: (0, 0, 0)),
                  pl.BlockSpec((1, lw), lambda i, j: (0, 0)),
                  pl.BlockSpec((1, lw), lambda i, j: (0, 0)),
                  pl.BlockSpec((1, lw), lambda i, j: (0, 0))],
        out_specs=(pl.BlockSpec((bb, tc, lw), lambda i, j: (i, j, 0)),
                   pl.BlockSpec((bb, 1, lw), lambda i, j: (i, 0, 0))),
        scratch_shapes=[pltpu.VMEM((bb, tc + CONV_PAD, lw), F32),
                        pltpu.VMEM((bb, tc, lw), F32),
                        pltpu.VMEM((bb, tc, lw), F32),
                        pltpu.VMEM((bb, 1, lw), F32)],
        compiler_params=_params("arbitrary", "arbitrary"),
        name="lru",
    )(pm3, pm3, conv_state, h0.reshape(b, 1, lw), cw, row(cb), wg, row(ba), row(bx), row(lam))
    return out, hn.reshape(b, lw)


def _delta_kernel(q_ref, k_ref, v_ref, z_ref, ps_ref, csq_ref, csk_ref, csv_ref, s0_ref,
                  cwq_ref, cwk_ref, cwv_ref, ea_ref, dtb_ref, nw_ref,
                  o_ref, sn_ref,
                  xq_ref, xk_ref, xv_ref, qn_ref, kn_ref, vn_ref, beta_ref, g_ref, *, chunk, nheads):
    ti = pl.program_id(1)
    bb, tb, w = q_ref.shape
    n_chunks = tb // chunk
    stacked = n_chunks == 1
    units = 1 if stacked else bb
    segs = bb if stacked else 1
    rows = segs * chunk
    lo = CONV_PAD - (CONV_W - 1)

    @pl.when(ti == 0)
    def _():
        xq_ref[:, lo:CONV_PAD, :] = csq_ref[...]
        xk_ref[:, lo:CONV_PAD, :] = csk_ref[...]
        xv_ref[:, lo:CONV_PAD, :] = csv_ref[...]
        sn_ref[...] = s0_ref[...]

    def conv_silu(x_ref, xs_ref, cw_ref):
        y = _causal_conv(x_ref, xs_ref, cw_ref)
        return (y * jax.nn.sigmoid(y)).reshape(bb * tb, w)

    qc = conv_silu(q_ref, xq_ref, cwq_ref)
    kc = conv_silu(k_ref, xk_ref, cwk_ref)
    vn_ref[...] = conv_silu(v_ref, xv_ref, cwv_ref)
    for hd in range(nheads):
        sl = slice(hd * LANES, (hd + 1) * LANES)
        qh = qc[:, sl]
        kh = kc[:, sl]
        qn_ref[:, sl] = qh * (lax.rsqrt(jnp.sum(qh * qh, axis=-1, keepdims=True) + NORM_EPS) * (LANES ** -0.5))
        kn_ref[:, sl] = kh * lax.rsqrt(jnp.sum(kh * kh, axis=-1, keepdims=True) + NORM_EPS)
    ps = ps_ref[...].reshape(bb * tb, LANES)
    beta_ref[...] = jax.nn.sigmoid(ps)
    g_ref[...] = -ea_ref[...] * _softplus(ps + dtb_ref[...])

    ii = lax.broadcasted_iota(jnp.int32, (rows, rows), 0)
    jj = lax.broadcasted_iota(jnp.int32, (rows, rows), 1)
    if segs == 1:
        le = jj <= ii
        lt = jj < ii
    else:
        same = (ii // chunk) == (jj // chunk)
        le = jnp.logical_and(same, jj <= ii)
        lt = jnp.logical_and(same, jj < ii)
    le_b = jnp.where(le, 1.0, 0.0).astype(BF16)
    assert chunk & (chunk - 1) == 0 and chunk >= 2
    levels = []
    b = 1
    while b < chunk:
        sh = b.bit_length()
        join = jnp.logical_and((ii >> sh) == (jj >> sh), (ii >> (sh - 1)) != (jj >> (sh - 1)))
        levels.append(jnp.logical_and(join, lt))
        b *= 2
    lanes = [slice(hd * LANES, (hd + 1) * LANES) for hd in range(nheads)]
    probs = [(u, hd) for u in range(units) for hd in range(nheads)]

    def mm(a, b):
        return jnp.dot(a.astype(BF16), b.astype(BF16), preferred_element_type=F32)

    def do_chunk(c, carry):
        rsl, gc, gct, eg, beta = [], [], [], [], []
        for u in range(units):
            rs = pl.ds(pl.multiple_of(u * tb + c * rows, rows), rows)
            g = g_ref[rs, :]
            g_hi = g.astype(BF16)
            g_lo = (g - g_hi.astype(F32)).astype(BF16)
            gcu = (jnp.dot(le_b, g_hi, preferred_element_type=F32)
                   + jnp.dot(le_b, g_lo, preferred_element_type=F32))
            gcp = gcu if rows == LANES else jnp.concatenate([gcu, jnp.zeros((LANES - rows, LANES), F32)], axis=0)
            rsl.append(rs)
            gc.append(gcu)
            gct.append(gcp.T)
            eg.append(jnp.exp(gcu))
            beta.append(beta_ref[rs, :])
        q = [qn_ref[rsl[u], lanes[hd]] for u, hd in probs]
        k = [kn_ref[rsl[u], lanes[hd]] for u, hd in probs]
        v = [vn_ref[rsl[u], lanes[hd]] for u, hd in probs]
        bcol = [beta[u][:, hd:hd + 1] for u, hd in probs]
        gcol = [gc[u][:, nheads + hd:nheads + hd + 1] for u, hd in probs]
        egc = [eg[u][:, nheads + hd:nheads + hd + 1] for u, hd in probs]
        np_ = range(len(probs))
        decay = [jnp.where(le, jnp.exp(jnp.where(le, gcol[i] - gct[u][nheads + hd:nheads + hd + 1, :rows], 0.0)), 0.0)
                 for i, (u, hd) in enumerate(probs)]
        kb = [k[i] * bcol[i] for i in np_]
        k16 = [k[i].astype(BF16) for i in np_]
        kk = [lax.dot_general(kb[i].astype(BF16), k16[i], NT_DIMS, preferred_element_type=F32) for i in np_]
        qk = [lax.dot_general(q[i].astype(BF16), k16[i], NT_DIMS, preferred_element_type=F32) for i in np_]
        lmat = [jnp.where(lt, kk[i] * decay[i], 0.0) for i in np_]
        attn = [jnp.where(le, qk[i] * decay[i], 0.0).astype(BF16) for i in np_]
        nmat = [jnp.where(levels[0], -lmat[i], 0.0) for i in np_]
        for lvl in levels[1:]:
            off = [jnp.where(lvl, lmat[i], 0.0) for i in np_]
            m = [off[i] + mm(nmat[i], off[i]) for i in np_]
            nmat = [nmat[i] - m[i] - mm(m[i], nmat[i]) for i in np_]
        rhs = [jnp.concatenate([v[i] * bcol[i], kb[i] * egc[i]], axis=1) for i in np_]
        sol = [rhs[i] + mm(nmat[i], rhs[i]) for i in np_]
        qe = [q[i] * egc[i] for i in np_]
        seq = lambda u, b: b if stacked else u
        s_old = [[sn_ref[seq(u, b), hd] for b in range(segs)] for u, hd in probs]
        both = [[jnp.dot(jnp.concatenate([sol[i][b * chunk:(b + 1) * chunk, LANES:],
                                          qe[i][b * chunk:(b + 1) * chunk]], axis=0).astype(BF16),
                         s_old[i][b].astype(BF16), preferred_element_type=F32) for b in range(segs)] for i in np_]
        cat = lambda parts: parts[0] if segs == 1 else jnp.concatenate(parts, axis=0)
        v16 = [(sol[i][:, :LANES] - cat([both[i][b][:chunk] for b in range(segs)])).astype(BF16) for i in np_]
        o = [cat([both[i][b][chunk:] for b in range(segs)]) + jnp.dot(attn[i], v16[i], preferred_element_type=F32)
             for i in np_]
        for i, (u, hd) in enumerate(probs):
            for b in range(segs):
                seg = slice(b * chunk, (b + 1) * chunk)
                g_last = gcol[i][(b + 1) * chunk - 1:(b + 1) * chunk]
                kd = (k[i][seg] * jnp.exp(g_last - gcol[i][seg])).astype(BF16)
                sn_ref[seq(u, b), hd] = s_old[i][b] * jnp.exp(g_last) + lax.dot_general(
                    kd, v16[i][seg], TN_DIMS, preferred_element_type=F32)
        for i, (u, hd) in enumerate(probs):
            sl = lanes[hd]
            on = o[i] * lax.rsqrt(jnp.mean(o[i] * o[i], axis=-1, keepdims=True) + NORM_EPS) * nw_ref[...]
            if stacked:
                zz = z_ref[:, :, sl].reshape(rows, LANES)
                o_ref[:, :, sl] = (on * (zz * jax.nn.sigmoid(zz))).reshape(bb, chunk, LANES)
            else:
                tsl = pl.ds(pl.multiple_of(c * chunk, chunk), chunk)
                zz = z_ref[u, tsl, sl]
                o_ref[u, tsl, sl] = on * (zz * jax.nn.sigmoid(zz))
        return carry

    if n_chunks == 1:
        do_chunk(0, 0)
    else:
        lax.fori_loop(0, n_chunks, do_chunk, 0)


def _delta(pm3, ps3, conv_state, s0, cw, a_log, dt_bias, norm_w, *, bb, tb, chunk):
    b, t, _ = pm3.shape
    nheads = a_log.shape[-1]
    w = nheads * LANES
    assert s0.shape == (b, nheads, LANES, LANES) and nheads <= 8
    assert b % bb == 0 and t % tb == 0 and tb % chunk == 0 and chunk % 8 == 0
    assert tb > chunk or bb * chunk <= LANES
    pad = lambda v: jnp.zeros((1, LANES), F32).at[0, nheads:2 * nheads].set(v)
    kern = functools.partial(_delta_kernel, chunk=chunk, nheads=nheads)
    col = lambda c: pl.BlockSpec((bb, tb, w), lambda i, j, c=c: (i, j, c))
    cst = lambda c: pl.BlockSpec((bb, CONV_W - 1, w), lambda i, j, c=c: (i, 0, c))
    cwt = lambda c: pl.BlockSpec((CONV_W, w), lambda i, j, c=c: (0, c))
    vec = pl.BlockSpec((1, LANES), lambda i, j: (0, 0))
    sblk = pl.BlockSpec((bb, nheads, LANES, LANES), lambda i, j: (i, 0, 0, 0))
    out, sn = pl.pallas_call(
        kern,
        out_shape=(jax.ShapeDtypeStruct((b, t, w), F32), jax.ShapeDtypeStruct(s0.shape, F32)),
        grid=(b // bb, t // tb),
        in_specs=[col(2), col(3), col(4), col(5),
                  pl.BlockSpec((bb, tb, LANES), lambda i, j: (i, j, 0)),
                  cst(0), cst(1), cst(2), sblk, cwt(0), cwt(1), cwt(2), vec, vec, vec],
        out_specs=(pl.BlockSpec((bb, tb, w), lambda i, j: (i, j, 0)), sblk),
        scratch_shapes=[pltpu.VMEM((bb, tb + CONV_PAD, w), F32)] * 3
                       + [pltpu.VMEM((bb * tb, w), F32)] * 3
                       + [pltpu.VMEM((bb * tb, LANES), F32)] * 2,
        compiler_params=_params("arbitrary", "arbitrary"),
        name="delta",
    )(pm3, pm3, pm3, pm3, ps3, conv_state, conv_state, conv_state, s0, cw, cw, cw,
      pad(jnp.exp(a_log.astype(F32))), pad(dt_bias.astype(F32)), norm_w.reshape(1, LANES))
    return out, sn


def _outproj_kernel(ol_ref, od_ref, x_ref, g1_ref, sc_ref, sh_ref, nw_ref, wt_ref, wb_ref, x1_ref, h2_ref):
    tm = h2_ref.shape[0]
    a = ol_ref[...].reshape(tm, ol_ref.shape[-1]).astype(BF16)
    b = od_ref[...].reshape(tm, od_ref.shape[-1]).astype(BF16)
    mix = (jnp.dot(a, wt_ref[...], preferred_element_type=F32)
           + jnp.dot(b, wb_ref[...], preferred_element_type=F32))
    x1 = x_ref[...] + g1_ref[...] * mix.reshape(x_ref.shape)
    x1_ref[...] = x1
    y = x1 * lax.rsqrt(jnp.mean(x1 * x1, axis=-1, keepdims=True) + NORM_EPS) * nw_ref[...]
    h2_ref[...] = (y * (1.0 + sc_ref[...]) + sh_ref[...]).reshape(h2_ref.shape).astype(BF16)


def _outproj(ol, od, x, g1, sc, sh, nw, wt, wb):
    b, t, d = x.shape
    n = b * t
    lw, dw = ol.shape[-1], od.shape[-1]
    tm = _tile(n, 512, 16)
    gb, r, nt = _row_groups(b, t, tm)
    tok = lambda c: pl.BlockSpec((gb, r, c), lambda i: (i // nt, i % nt, 0))
    per = pl.BlockSpec((gb, 1, d), lambda i: (i // nt, 0, 0))
    return pl.pallas_call(
        _outproj_kernel,
        out_shape=(jax.ShapeDtypeStruct((b, t, d), F32), jax.ShapeDtypeStruct((n, d), BF16)),
        grid=(n // tm,),
        in_specs=[tok(lw), tok(dw), tok(d), per, per, per,
                  pl.BlockSpec((1, 1, d), lambda i: (0, 0, 0)),
                  pl.BlockSpec((lw, d), lambda i: (0, 0)),
                  pl.BlockSpec((dw, d), lambda i: (0, 0))],
        out_specs=(tok(d), pl.BlockSpec((tm, d), lambda i: (i, 0))),
        compiler_params=_params("arbitrary"),
        name="outproj",
    )(ol, od, x, g1, sc, sh, nw, wt, wb)


def _sorting_network(n):
    pairs = []
    p = 1
    while p < n:
        k = p
        while k >= 1:
            for j in range(k % p, n - k, 2 * k):
                for i in range(min(k, n - j - k)):
                    if (i + j) // (2 * p) == (i + j + k) // (2 * p):
                        pairs.append((i + j, i + j + k))
            k //= 2
        p *= 2
    return pairs


def _merge_top(lists, singles, count):
    lists = list(lists)
    singles = list(singles)
    vals = []
    for r in range(count):
        head = lists[0]
        for x in singles:
            head = jnp.maximum(head, x)
        m = jnp.max(head, axis=0, keepdims=True)
        vals.append(m)
        left = count - r - 1
        if left == 0:
            break
        eq = lists[0] == m
        keep = min(len(lists), left)
        lists = [jnp.where(eq, lists[v + 1] if v + 1 < len(lists) else -jnp.inf, lists[v]) for v in range(keep)]
        singles = [jnp.where(x == m, -jnp.inf, x) for x in singles]
    return vals


def _top_values(s, count):
    tiles = [s[8 * v:8 * (v + 1)] for v in range(s.shape[0] // 8)]
    for i, j in _sorting_network(len(tiles)):
        tiles[i], tiles[j] = jnp.maximum(tiles[i], tiles[j]), jnp.minimum(tiles[i], tiles[j])
    return _merge_top(tiles, [], count)


def _router_kernel(h_ref, wq_ref, sk_ref, s2_ref, thr_ref, r_ref, q_ref, *, nheads):
    nk = sk_ref.shape[1]
    half = sk_ref.shape[2]
    q_ref[...] = lax.dot_general(wq_ref[...], h_ref[...], NT_DIMS, preferred_element_type=F32)
    p = h_ref.shape[0]
    sub = lax.broadcasted_iota(jnp.int32, (8, p), 0)

    def head(hd, carry):
        base = pl.multiple_of(hd * (2 * half), 2 * half)
        q1 = q_ref[pl.ds(base, half), :].astype(BF16)
        q2 = q_ref[pl.ds(base + half, half), :].astype(BF16)
        s1 = jnp.dot(sk_ref[2 * hd], q1, preferred_element_type=F32)
        s2 = jnp.dot(sk_ref[2 * hd + 1], q2, preferred_element_type=F32)
        nv = PEER_TOPK + 1
        t1 = _top_values(s1, nv)
        t2 = _top_values(s2, nv)
        t2_lo = jnp.concatenate(t2[:8], axis=0)
        t2_hi = jnp.concatenate(t2[8:16], axis=0)
        lists = [t1[0] + t2_lo]
        for a in range(1, nv):
            nb = nv // (a + 1)
            c = t1[a] + t2_lo
            lists.append(c if nb >= 8 else jnp.where(sub < nb, c, -jnp.inf))
        singles = [t1[0] + t2_hi, jnp.where(sub < 1, t1[0] + t2[16], -jnp.inf)]
        cand = jnp.concatenate(lists + singles, axis=0)
        best = _merge_top(lists, singles, nv)
        tau = best[PEER_TOPK - 1]
        mid = 0.5 * (tau + best[PEER_TOPK])
        top = t1[0] + t2[0]
        z = jnp.sum(jnp.where(cand >= tau, jnp.exp(cand - top), 0.0), axis=0, keepdims=True)
        s2_ref[hd] = s2 * LOG2E
        thr_ref[hd] = (mid - s1) * LOG2E
        r_ref[hd] = (s1 - (top + jnp.log(z))) * LOG2E
        return carry

    lax.fori_loop(0, nheads, head, 0)


def _router(h2, wq_t, sk, *, tp):
    n, d = h2.shape
    nheads = sk.shape[0] // 2
    nk = sk.shape[1]
    hq = wq_t.shape[0]
    out = jax.ShapeDtypeStruct((nheads, nk, n), F32)
    oblk = pl.BlockSpec((nheads, nk, tp), lambda i: (0, 0, i))
    return pl.pallas_call(
        functools.partial(_router_kernel, nheads=nheads),
        out_shape=(out, out, out),
        grid=(n // tp,),
        in_specs=[pl.BlockSpec((tp, d), lambda i: (i, 0)),
                  pl.BlockSpec((hq, d), lambda i: (0, 0)),
                  pl.BlockSpec(sk.shape, lambda i: (0, 0, 0))],
        out_specs=(oblk, oblk, oblk),
        scratch_shapes=[pltpu.VMEM((hq, tp), F32)],
        compiler_params=_params("arbitrary"),
        name="router",
    )(h2, wq_t, sk)


def _dense_kernel(h_ref, u_ref, vt_ref, s2_ref, thr_ref, r_ref, x1_ref, g2_ref, fw_ref,
                  y_ref, acc_ref, st_ref, coef_ref, *, nheads, rows_per_step, n_chunks):
    s = pl.program_id(1)
    nk = s2_ref.shape[1]
    tp = h_ref.shape[0]

    @pl.when(s == 0)
    def _():
        acc_ref[...] = jnp.zeros_like(acc_ref)
        st_ref[...] = jnp.zeros_like(st_ref)

    cur = s % 2
    prev = 1 - cur
    st_ref[cur] = lax.dot_general(u_ref[...], h_ref[...], NT_DIMS, preferred_element_type=F32)
    chunk = jnp.clip(s - 1, 0, n_chunks - 1)
    for ii in range(rows_per_step):
        i1 = chunk * rows_per_step + ii
        rows = slice(ii * nk, (ii + 1) * nk)
        thr_rows = [thr_ref[hd, pl.ds(i1, 1), :] for hd in range(nheads)]
        r_rows = [r_ref[hd, pl.ds(i1, 1), :] for hd in range(nheads)]
        for pb in range(tp // LANES):
            cols = slice(pb * LANES, (pb + 1) * LANES)
            gate = None
            for hd in range(nheads):
                s2 = s2_ref[hd, :, cols]
                term = jnp.where(s2 >= thr_rows[hd][:, cols], jnp.exp2(s2 + r_rows[hd][:, cols]), 0.0)
                gate = term if gate is None else gate + term
            coef_ref[rows, cols] = (jax.nn.gelu(st_ref[prev, rows, cols]) * gate).astype(BF16)
    acc_ref[...] += jnp.dot(vt_ref[0], coef_ref[...], preferred_element_type=F32)

    @pl.when(s == pl.num_programs(1) - 1)
    def _():
        peer = acc_ref[...].T.reshape(x1_ref.shape)
        x2 = x1_ref[...] + g2_ref[...] * peer
        y_ref[...] = x2 * lax.rsqrt(jnp.mean(x2 * x2, axis=-1, keepdims=True) + NORM_EPS) * fw_ref[...]


def _dense(h2, u16, vt16, s2t, thr, rl, x1, g2, fw, *, tp):
    b, t, d = x1.shape
    n = b * t
    nheads, nk, _ = s2t.shape
    n_chunks, _, ce = vt16.shape
    rows_per_step = ce // nk
    assert u16.shape[0] == n_chunks * ce
    gb, r, nt = _row_groups(b, t, tp)
    rblk = pl.BlockSpec((nheads, nk, tp), lambda i, s: (0, 0, i))
    tok = pl.BlockSpec((gb, r, d), lambda i, s: (i // nt, i % nt, 0))
    return pl.pallas_call(
        functools.partial(_dense_kernel, nheads=nheads, rows_per_step=rows_per_step, n_chunks=n_chunks),
        out_shape=jax.ShapeDtypeStruct((b, t, d), F32),
        grid=(n // tp, n_chunks + 1),
        in_specs=[pl.BlockSpec((tp, d), lambda i, s: (i, 0)),
                  pl.BlockSpec((ce, d), lambda i, s: (jnp.minimum(s, n_chunks - 1), 0)),
                  pl.BlockSpec((1, d, ce), lambda i, s: (jnp.clip(s - 1, 0, n_chunks - 1), 0, 0)),
                  rblk, rblk, rblk, tok,
                  pl.BlockSpec((gb, 1, d), lambda i, s: (i // nt, 0, 0)),
                  pl.BlockSpec((1, 1, d), lambda i, s: (0, 0, 0))],
        out_specs=tok,
        scratch_shapes=[pltpu.VMEM((d, tp), F32), pltpu.VMEM((2, ce, tp), F32), pltpu.VMEM((ce, tp), BF16)],
        compiler_params=_params("arbitrary", "arbitrary"),
        name="dense",
    )(h2, u16, vt16, s2t, thr, rl, x1, g2, fw)


def _group(x, mod, lru_conv, lru_h, dn_conv, dn_s, reset_first, p, fw, *, lru_tiles, dn_tiles, tp):
    b, t, d = x.shape
    sh1, sc1, g1, sh2, sc2, g2 = [m.reshape(b, 1, d) for m in jnp.split(mod, 6, axis=-1)]
    pm, ps = _inproj(x, sc1, sh1, p['norm_mix_w'], p['w_main'], p['w_small'])
    pm3 = pm.reshape(b, t, -1)
    ps3 = ps.reshape(b, t, LANES)
    lw = p['lru_lambda'].shape[-1]
    out_lru, new_h = _lru(pm3, lru_conv, lru_h, p['lru_conv_w'], p['lru_conv_b'], p['lru_wg'], p['lru_ba'],
                          p['lru_bx'], p['lru_lambda'], reset_first=reset_first, bb=lru_tiles[0], tc=lru_tiles[1])
    out_dn, new_s = _delta(pm3, ps3, dn_conv, dn_s, p['dn_conv_w'], p['dn_A_log'], p['dn_dt_bias'], p['dn_norm_w'],
                           bb=dn_tiles[0], tb=dn_tiles[1], chunk=dn_tiles[2])
    keep = CONV_W - 1
    new_lru_conv = pm3[:, t - keep:, :lw]
    new_dn_conv = pm3[:, t - keep:, 2 * lw:2 * lw + dn_conv.shape[-1]]
    x1, h2 = _outproj(out_lru, out_dn, x, g1, sc2, sh2, p['norm_ffn_w'], p['w_out_top'], p['w_out_bot'])
    s2t, thr, r = _router(h2, p['wq_t'], p['subkeys'], tp=tp)
    y = _dense(h2, p['peer_u'], p['peer_vt'], s2t, thr, r, x1, g2, fw, tp=tp)
    return y, (new_lru_conv, new_h, new_dn_conv, new_s)


def kernel(x_prompt, x_sample, state_lru_conv, state_lru_h, state_dn_conv, state_dn_S, c_prompt, c_sample,
           w_ada, b_ada, norm_mix_w, norm_ffn_w, w_in, lru_conv_w, lru_conv_b, lru_wa, lru_ba, lru_wx, lru_bx,
           lru_lambda, dn_conv_w, dn_A_log, dn_dt_bias, dn_norm_w, w_out, peer_wq, peer_subkeys, peer_u, peer_v,
           final_norm_w):
    depth = w_ada.shape[0]
    assert depth == 1, "the final norm is fused into the last layer's expert kernel"
    bp, seq, d = x_prompt.shape
    bs, dseq, _ = x_sample.shape
    lw = lru_lambda.shape[-1]
    dn_heads = dn_A_log.shape[-1]
    dn_w = dn_heads * LANES
    conv_ch = dn_conv_w.shape[-1]
    assert lru_wa.shape[-1] == LANES and lw == dn_w and conv_ch == 3 * dn_w
    assert peer_subkeys.shape[-1] == LANES and peer_subkeys.shape[-2] == LANES
    main = 2 * lw + conv_ch + dn_w
    l = 0
    w_small = jnp.zeros((d, LANES), F32).at[:, :2 * dn_heads].set(w_in[l][:, main:]).astype(BF16)
    nheads = peer_subkeys.shape[1]
    p = {
        'norm_mix_w': norm_mix_w[l].reshape(1, 1, d), 'norm_ffn_w': norm_ffn_w[l].reshape(1, 1, d),
        'w_main': w_in[l][:, :main].astype(BF16), 'w_small': w_small,
        'lru_conv_w': lru_conv_w[l], 'lru_conv_b': lru_conv_b[l],
        'lru_wg': jnp.concatenate([lru_wa[l], lru_wx[l]], axis=-1).astype(BF16),
        'lru_ba': lru_ba[l], 'lru_bx': lru_bx[l], 'lru_lambda': lru_lambda[l],
        'dn_conv_w': dn_conv_w[l], 'dn_A_log': dn_A_log[l], 'dn_dt_bias': dn_dt_bias[l], 'dn_norm_w': dn_norm_w[l],
        'w_out_top': w_out[l][:lw].astype(BF16), 'w_out_bot': w_out[l][lw:].astype(BF16),
        'wq_t': peer_wq[l].T.astype(BF16),
        'subkeys': peer_subkeys[l].reshape(nheads * 2, LANES, LANES).astype(BF16),
        'peer_u': peer_u[l].astype(BF16),
        'peer_vt': peer_v[l].reshape(-1, PEER_CHUNK, d).transpose(0, 2, 1).astype(BF16),
    }
    fw = final_norm_w.reshape(1, 1, d)
    mod = _ada(jnp.concatenate([c_prompt, c_sample], axis=0), w_ada[l], b_ada[l])
    zeros = lambda *s: jnp.zeros(s, F32)
    yp, sp = _group(x_prompt, mod[:bp], zeros(bp, CONV_W - 1, lw), zeros(bp, lw), zeros(bp, CONV_W - 1, conv_ch),
                    zeros(bp, dn_heads, LANES, LANES), True, p, fw,
                    lru_tiles=(bp, _tile(seq, 128, 8)), dn_tiles=(_tile(bp, 2, 1), _tile(seq, 256, DN_CHUNK), min(DN_CHUNK, seq)),
                    tp=_tile(bp * seq, 512, LANES))
    ys, ss = _group(x_sample, mod[bp:], state_lru_conv[l], state_lru_h[l], state_dn_conv[l], state_dn_S[l],
                    False, p, fw,
                    lru_tiles=(_tile(bs, 32, 1), dseq), dn_tiles=(_tile(bs, DN_CHUNK // dseq, 1), dseq, dseq),
                    tp=_tile(bs * dseq, 512, LANES))
    stack = lambda v: v[None]
    return (yp, ys, stack(sp[0]), stack(sp[1]), stack(sp[2]), stack(sp[3]),
            stack(ss[0]), stack(ss[1]), stack(ss[2]), stack(ss[3]))
```

```python
import functools
import math

import jax
import jax.numpy as jnp
from jax import lax
from jax.experimental import pallas as pl
from jax.experimental.pallas import tpu as pltpu

F32 = jnp.float32
BF16 = jnp.bfloat16
NORM_EPS = 1e-6
LRU_C = 8.0
CONV_W = 4
CONV_PAD = 8
PEER_TOPK = 16
PEER_CHUNK = 512
DN_CHUNK = 64
LOG2E = math.log2(math.e)
LANES = 128
VMEM_LIMIT_BYTES = 60 * 1024 * 1024
NT_DIMS = (((1,), (1,)), ((), ()))
TN_DIMS = (((0,), (0,)), ((), ()))


def _params(*semantics):
    return pltpu.CompilerParams(dimension_semantics=semantics, vmem_limit_bytes=VMEM_LIMIT_BYTES)


def _tile(n, target, multiple):
    if n <= target:
        return n
    t = (target // multiple) * multiple
    while t >= multiple:
        if n % t == 0:
            return t
        t -= multiple
    raise ValueError(f"no tile for {n} <= {target} in multiples of {multiple}")


def _softplus(x):
    return jnp.maximum(x, 0.0) + jnp.log1p(jnp.exp(-jnp.abs(x)))


def _row_groups(batch, seq, tm):
    r = min(seq, tm)
    assert seq % r == 0 and tm % r == 0 and (batch * seq) % tm == 0
    return tm // r, r, seq // r


def _ada_kernel(c_ref, w_ref, b_ref, o_ref):
    c = c_ref[...]
    a = (c * jax.nn.sigmoid(c)).astype(BF16)
    o_ref[...] = jnp.dot(a, w_ref[...].astype(BF16), preferred_element_type=F32) + b_ref[...]


def _ada(c, w, b):
    bc, d = c.shape
    n = w.shape[1]
    tn = _tile(n, 1024, LANES)
    return pl.pallas_call(
        _ada_kernel,
        out_shape=jax.ShapeDtypeStruct((bc, n), F32),
        grid=(n // tn,),
        in_specs=[pl.BlockSpec((bc, d), lambda j: (0, 0)),
                  pl.BlockSpec((d, tn), lambda j: (0, j)),
                  pl.BlockSpec((1, tn), lambda j: (0, j))],
        out_specs=pl.BlockSpec((bc, tn), lambda j: (0, j)),
        compiler_params=_params("arbitrary"),
        name="ada",
    )(c, w, b.reshape(1, n))


def _inproj_kernel(x_ref, sc_ref, sh_ref, nw_ref, wm_ref, ws_ref, om_ref, os_ref, h_ref):
    @pl.when(pl.program_id(1) == 0)
    def _():
        x = x_ref[...]
        y = x * lax.rsqrt(jnp.mean(x * x, axis=-1, keepdims=True) + NORM_EPS) * nw_ref[...]
        h = y * (1.0 + sc_ref[...]) + sh_ref[...]
        hb = h.reshape(h_ref.shape).astype(BF16)
        h_ref[...] = hb
        os_ref[...] = jnp.dot(hb, ws_ref[...], preferred_element_type=F32)

    om_ref[...] = jnp.dot(h_ref[...], wm_ref[...], preferred_element_type=F32)


def _inproj(x, sc, sh, nw, wm, ws):
    b, t, d = x.shape
    n = b * t
    m = wm.shape[1]
    tm = _tile(n, 1024, 8)
    tn = _tile(m, 1536, LANES)
    gb, r, nt = _row_groups(b, t, tm)
    return pl.pallas_call(
        _inproj_kernel,
        out_shape=(jax.ShapeDtypeStruct((n, m), F32), jax.ShapeDtypeStruct((n, LANES), F32)),
        grid=(n // tm, m // tn),
        in_specs=[pl.BlockSpec((gb, r, d), lambda i, j: (i // nt, i % nt, 0)),
                  pl.BlockSpec((gb, 1, d), lambda i, j: (i // nt, 0, 0)),
                  pl.BlockSpec((gb, 1, d), lambda i, j: (i // nt, 0, 0)),
                  pl.BlockSpec((1, 1, d), lambda i, j: (0, 0, 0)),
                  pl.BlockSpec((d, tn), lambda i, j: (0, j)),
                  pl.BlockSpec((d, LANES), lambda i, j: (0, 0))],
        out_specs=(pl.BlockSpec((tm, tn), lambda i, j: (i, j)),
                   pl.BlockSpec((tm, LANES), lambda i, j: (i, 0))),
        scratch_shapes=[pltpu.VMEM((tm, d), BF16)],
        compiler_params=_params("arbitrary", "arbitrary"),
        name="inproj",
    )(x, sc, sh, nw, wm, ws)


def _causal_conv(x_ref, xs_ref, cw_ref):
    t = x_ref.shape[1]
    xs_ref[:, CONV_PAD:CONV_PAD + t, :] = x_ref[...]
    cw = cw_ref[...]
    y = None
    for k in range(CONV_W):
        lo = CONV_PAD - (CONV_W - 1) + k
        term = xs_ref[:, lo:lo + t, :] * cw[k:k + 1, :][None]
        y = term if y is None else y + term
    xs_ref[:, CONV_PAD - (CONV_W - 1):CONV_PAD, :] = xs_ref[:, t + CONV_PAD - (CONV_W - 1):t + CONV_PAD, :]
    return y


def _lru_kernel(xl_ref, yl_ref, cs_ref, h0_ref, cw_ref, cb_ref, wg_ref, ba_ref, bx_ref, lam_ref,
                o_ref, hn_ref, xs_ref, a_ref, u_ref, h_ref, *, reset_first):
    ti = pl.program_id(1)
    bb, tc, lw = xl_ref.shape
    nheads = lw // LANES

    @pl.when(ti == 0)
    def _():
        xs_ref[:, CONV_PAD - (CONV_W - 1):CONV_PAD, :] = cs_ref[...]
        h_ref[...] = h0_ref[...]

    xc = (_causal_conv(xl_ref, xs_ref, cw_ref) + cb_ref[...][None]).reshape(bb * tc, lw)
    sp = _softplus(-lam_ref[...])
    t_glob = lax.broadcasted_iota(jnp.int32, (bb, tc, LANES), 1) + ti * tc
    for hd in range(nheads):
        sl = slice(hd * LANES, (hd + 1) * LANES)
        xh = xc[:, sl]
        gates = jnp.dot(xh.astype(BF16), wg_ref[hd], preferred_element_type=F32)
        r = jax.nn.sigmoid(gates[:, :LANES] + ba_ref[:, sl])
        i = jax.nn.sigmoid(gates[:, LANES:] + bx_ref[:, sl])
        log_a = -LRU_C * r * sp[:, sl]
        a = jnp.exp(log_a)
        mult = jnp.sqrt(jnp.maximum(-jnp.tanh(log_a) * (a * a + 1.0), 0.0))
        u = i * xh
        a3 = a.reshape(bb, tc, LANES)
        m3 = mult.reshape(bb, tc, LANES)
        if reset_first:
            m3 = jnp.where(t_glob == 0, 1.0, m3)
        a_ref[:, :, sl] = a3
        u_ref[:, :, sl] = m3 * u.reshape(bb, tc, LANES)

    def step(t, h):
        h = a_ref[:, pl.ds(t, 1), :] * h + u_ref[:, pl.ds(t, 1), :]
        u_ref[:, pl.ds(t, 1), :] = h
        return h

    h = lax.fori_loop(0, tc, step, h_ref[...], unroll=8)
    h_ref[...] = h
    hn_ref[...] = h
    o_ref[...] = u_ref[...] * jax.nn.gelu(yl_ref[...])


def _lru(pm3, conv_state, h0, cw, cb, wg, ba, bx, lam, *, reset_first, bb, tc):
    b, t, _ = pm3.shape
    lw = lam.shape[-1]
    assert b % bb == 0 and t % tc == 0 and tc % 8 == 0
    row = lambda v: v.reshape(1, lw)
    kern = functools.partial(_lru_kernel, reset_first=reset_first)
    out, hn = pl.pallas_call(
        kern,
        out_shape=(jax.ShapeDtypeStruct((b, t, lw), F32), jax.ShapeDtypeStruct((b, 1, lw), F32)),
        grid=(b // bb, t // tc),
        in_specs=[pl.BlockSpec((bb, tc, lw), lambda i, j: (i, j, 0)),
                  pl.BlockSpec((bb, tc, lw), lambda i, j: (i, j, 1)),
                  pl.BlockSpec((bb, CONV_W - 1, lw), lambda i, j: (i, 0, 0)),
                  pl.BlockSpec((bb, 1, lw), lambda i, j: (i, 0, 0)),
                  pl.BlockSpec((CONV_W, lw), lambda i, j: (0, 0)),
                  pl.BlockSpec((1, lw), lambda i, j: (0, 0)),
                  pl.BlockSpec(wg.shape, lambda i, j: (0, 0, 0)),
                  pl.BlockSpec((1, lw), lambda i, j: (0, 0)),
                  pl.BlockSpec((1, lw), lambda i, j: (0, 0)),
                  pl.BlockSpec((1, lw), lambda i, j: (0, 0))],
        out_specs=(pl.BlockSpec((bb, tc, lw), lambda i, j: (i, j, 0)),
                   pl.BlockSpec((bb, 1, lw), lambda i, j: (i, 0, 0))),
        scratch_shapes=[pltpu.VMEM((bb, tc + CONV_PAD, lw), F32),
                        pltpu.VMEM((bb, tc, lw), F32),
                        pltpu.VMEM((bb, tc, lw), F32),
                        pltpu.VMEM((bb, 1, lw), F32)],
        compiler_params=_params("arbitrary", "arbitrary"),
        name="lru",
    )(pm3, pm3, conv_state, h0.reshape(b, 1, lw), cw, row(cb), wg, row(ba), row(bx), row(lam))
    return out, hn.reshape(b, lw)


def _delta_kernel(q_ref, k_ref, v_ref, z_ref, ps_ref, csq_ref, csk_ref, csv_ref, s0_ref,
                  cwq_ref, cwk_ref, cwv_ref, ea_ref, dtb_ref, nw_ref,
                  o_ref, sn_ref,
                  xq_ref, xk_ref, xv_ref, qn_ref, kn_ref, vn_ref, beta_ref, g_ref, *, chunk, nheads):
    ti = pl.program_id(1)
    bb, tb, w = q_ref.shape
    n_chunks = tb // chunk
    stacked = n_chunks == 1
    units = 1 if stacked else bb
    segs = bb if stacked else 1
    rows = segs * chunk
    lo = CONV_PAD - (CONV_W - 1)

    @pl.when(ti == 0)
    def _():
        xq_ref[:, lo:CONV_PAD, :] = csq_ref[...]
        xk_ref[:, lo:CONV_PAD, :] = csk_ref[...]
        xv_ref[:, lo:CONV_PAD, :] = csv_ref[...]
        sn_ref[...] = s0_ref[...]

    def conv_silu(x_ref, xs_ref, cw_ref):
        y = _causal_conv(x_ref, xs_ref, cw_ref)
        return (y * jax.nn.sigmoid(y)).reshape(bb * tb, w)

    qc = conv_silu(q_ref, xq_ref, cwq_ref)
    kc = conv_silu(k_ref, xk_ref, cwk_ref)
    vn_ref[...] = conv_silu(v_ref, xv_ref, cwv_ref)
    for hd in range(nheads):
        sl = slice(hd * LANES, (hd + 1) * LANES)
        qh = qc[:, sl]
        kh = kc[:, sl]
        qn_ref[:, sl] = qh * (lax.rsqrt(jnp.sum(qh * qh, axis=-1, keepdims=True) + NORM_EPS) * (LANES ** -0.5))
        kn_ref[:, sl] = kh * lax.rsqrt(jnp.sum(kh * kh, axis=-1, keepdims=True) + NORM_EPS)
    ps = ps_ref[...].reshape(bb * tb, LANES)
    beta_ref[...] = jax.nn.sigmoid(ps)
    g_ref[...] = -ea_ref[...] * _softplus(ps + dtb_ref[...])

    ii = lax.broadcasted_iota(jnp.int32, (rows, rows), 0)
    jj = lax.broadcasted_iota(jnp.int32, (rows, rows), 1)
    if segs == 1:
        le = jj <= ii
        lt = jj < ii
    else:
        same = (ii // chunk) == (jj // chunk)
        le = jnp.logical_and(same, jj <= ii)
        lt = jnp.logical_and(same, jj < ii)
    le_b = jnp.where(le, 1.0, 0.0).astype(BF16)
    assert chunk & (chunk - 1) == 0 and chunk >= 2
    levels = []
    b = 1
    while b < chunk:
        sh = b.bit_length()
        join = jnp.logical_and((ii >> sh) == (jj >> sh), (ii >> (sh - 1)) != (jj >> (sh - 1)))
        levels.append(jnp.logical_and(join, lt))
        b *= 2
    lanes = [slice(hd * LANES, (hd + 1) * LANES) for hd in range(nheads)]
    probs = [(u, hd) for u in range(units) for hd in range(nheads)]

    def mm(a, b):
        return jnp.dot(a.astype(BF16), b.astype(BF16), preferred_element_type=F32)

    def do_chunk(c, carry):
        rsl, gc, gct, eg, beta = [], [], [], [], []
        for u in range(units):
            rs = pl.ds(pl.multiple_of(u * tb + c * rows, rows), rows)
            g = g_ref[rs, :]
            g_hi = g.astype(BF16)
            g_lo = (g - g_hi.astype(F32)).astype(BF16)
            gcu = (jnp.dot(le_b, g_hi, preferred_element_type=F32)
                   + jnp.dot(le_b, g_lo, preferred_element_type=F32))
            gcp = gcu if rows == LANES else jnp.concatenate([gcu, jnp.zeros((LANES - rows, LANES), F32)], axis=0)
            rsl.append(rs)
            gc.append(gcu)
            gct.append(gcp.T)
            eg.append(jnp.exp(gcu))
            beta.append(beta_ref[rs, :])
        q = [qn_ref[rsl[u], lanes[hd]] for u, hd in probs]
        k = [kn_ref[rsl[u], lanes[hd]] for u, hd in probs]
        v = [vn_ref[rsl[u], lanes[hd]] for u, hd in probs]
        bcol = [beta[u][:, hd:hd + 1] for u, hd in probs]
        gcol = [gc[u][:, nheads + hd:nheads + hd + 1] for u, hd in probs]
        egc = [eg[u][:, nheads + hd:nheads + hd + 1] for u, hd in probs]
        np_ = range(len(probs))
        decay = [jnp.where(le, jnp.exp(jnp.where(le, gcol[i] - gct[u][nheads + hd:nheads + hd + 1, :rows], 0.0)), 0.0)
                 for i, (u, hd) in enumerate(probs)]
        kb = [k[i] * bcol[i] for i in np_]
        k16 = [k[i].astype(BF16) for i in np_]
        kk = [lax.dot_general(kb[i].astype(BF16), k16[i], NT_DIMS, preferred_element_type=F32) for i in np_]
        qk = [lax.dot_general(q[i].astype(BF16), k16[i], NT_DIMS, preferred_element_type=F32) for i in np_]
        lmat = [jnp.where(lt, kk[i] * decay[i], 0.0) for i in np_]
        attn = [jnp.where(le, qk[i] * decay[i], 0.0).astype(BF16) for i in np_]
        nmat = [jnp.where(levels[0], -lmat[i], 0.0) for i in np_]
        for lvl in levels[1:]:
            off = [jnp.where(lvl, lmat[i], 0.0) for i in np_]
            m = [off[i] + mm(nmat[i], off[i]) for i in np_]
            nmat = [nmat[i] - m[i] - mm(m[i], nmat[i]) for i in np_]
        rhs = [jnp.concatenate([v[i] * bcol[i], kb[i] * egc[i]], axis=1) for i in np_]
        sol = [rhs[i] + mm(nmat[i], rhs[i]) for i in np_]
        qe = [q[i] * egc[i] for i in np_]
        seq = lambda u, b: b if stacked else u
        s_old = [[sn_ref[seq(u, b), hd] for b in range(segs)] for u, hd in probs]
        both = [[jnp.dot(jnp.concatenate([sol[i][b * chunk:(b + 1) * chunk, LANES:],
                                          qe[i][b * chunk:(b + 1) * chunk]], axis=0).astype(BF16),
                         s_old[i][b].astype(BF16), preferred_element_type=F32) for b in range(segs)] for i in np_]
        cat = lambda parts: parts[0] if segs == 1 else jnp.concatenate(parts, axis=0)
        v16 = [(sol[i][:, :LANES] - cat([both[i][b][:chunk] for b in range(segs)])).astype(BF16) for i in np_]
        o = [cat([both[i][b][chunk:] for b in range(segs)]) + jnp.dot(attn[i], v16[i], preferred_element_type=F32)
             for i in np_]
        for i, (u, hd) in enumerate(probs):
            for b in range(segs):
                seg = slice(b * chunk, (b + 1) * chunk)
                g_last = gcol[i][(b + 1) * chunk - 1:(b + 1) * chunk]
                kd = (k[i][seg] * jnp.exp(g_last - gcol[i][seg])).astype(BF16)
                sn_ref[seq(u, b), hd] = s_old[i][b] * jnp.exp(g_last) + lax.dot_general(
                    kd, v16[i][seg], TN_DIMS, preferred_element_type=F32)
        for i, (u, hd) in enumerate(probs):
            sl = lanes[hd]
            on = o[i] * lax.rsqrt(jnp.mean(o[i] * o[i], axis=-1, keepdims=True) + NORM_EPS) * nw_ref[...]
            if stacked:
                zz = z_ref[:, :, sl].reshape(rows, LANES)
                o_ref[:, :, sl] = (on * (zz * jax.nn.sigmoid(zz))).reshape(bb, chunk, LANES)
            else:
                tsl = pl.ds(pl.multiple_of(c * chunk, chunk), chunk)
                zz = z_ref[u, tsl, sl]
                o_ref[u, tsl, sl] = on * (zz * jax.nn.sigmoid(zz))
        return carry

    if n_chunks == 1:
        do_chunk(0, 0)
    else:
        lax.fori_loop(0, n_chunks, do_chunk, 0)


def _delta(pm3, ps3, conv_state, s0, cw, a_log, dt_bias, norm_w, *, bb, tb, chunk):
    b, t, _ = pm3.shape
    nheads = a_log.shape[-1]
    w = nheads * LANES
    assert s0.shape == (b, nheads, LANES, LANES) and nheads <= 8
    assert b % bb == 0 and t % tb == 0 and tb % chunk == 0 and chunk % 8 == 0
    assert tb > chunk or bb * chunk <= LANES
    pad = lambda v: jnp.zeros((1, LANES), F32).at[0, nheads:2 * nheads].set(v)
    kern = functools.partial(_delta_kernel, chunk=chunk, nheads=nheads)
    col = lambda c: pl.BlockSpec((bb, tb, w), lambda i, j, c=c: (i, j, c))
    cst = lambda c: pl.BlockSpec((bb, CONV_W - 1, w), lambda i, j, c=c: (i, 0, c))
    cwt = lambda c: pl.BlockSpec((CONV_W, w), lambda i, j, c=c: (0, c))
    vec = pl.BlockSpec((1, LANES), lambda i, j: (0, 0))
    sblk = pl.BlockSpec((bb, nheads, LANES, LANES), lambda i, j: (i, 0, 0, 0))
    out, sn = pl.pallas_call(
        kern,
        out_shape=(jax.ShapeDtypeStruct((b, t, w), F32), jax.ShapeDtypeStruct(s0.shape, F32)),
        grid=(b // bb, t // tb),
        in_specs=[col(2), col(3), col(4), col(5),
                  pl.BlockSpec((bb, tb, LANES), lambda i, j: (i, j, 0)),
                  cst(0), cst(1), cst(2), sblk, cwt(0), cwt(1), cwt(2), vec, vec, vec],
        out_specs=(pl.BlockSpec((bb, tb, w), lambda i, j: (i, j, 0)), sblk),
        scratch_shapes=[pltpu.VMEM((bb, tb + CONV_PAD, w), F32)] * 3
                       + [pltpu.VMEM((bb * tb, w), F32)] * 3
                       + [pltpu.VMEM((bb * tb, LANES), F32)] * 2,
        compiler_params=_params("arbitrary", "arbitrary"),
        name="delta",
    )(pm3, pm3, pm3, pm3, ps3, conv_state, conv_state, conv_state, s0, cw, cw, cw,
      pad(jnp.exp(a_log.astype(F32))), pad(dt_bias.astype(F32)), norm_w.reshape(1, LANES))
    return out, sn


def _outproj_kernel(ol_ref, od_ref, x_ref, g1_ref, sc_ref, sh_ref, nw_ref, wt_ref, wb_ref, x1_ref, h2_ref):
    tm = h2_ref.shape[0]
    a = ol_ref[...].reshape(tm, ol_ref.shape[-1]).astype(BF16)
    b = od_ref[...].reshape(tm, od_ref.shape[-1]).astype(BF16)
    mix = (jnp.dot(a, wt_ref[...], preferred_element_type=F32)
           + jnp.dot(b, wb_ref[...], preferred_element_type=F32))
    x1 = x_ref[...] + g1_ref[...] * mix.reshape(x_ref.shape)
    x1_ref[...] = x1
    y = x1 * lax.rsqrt(jnp.mean(x1 * x1, axis=-1, keepdims=True) + NORM_EPS) * nw_ref[...]
    h2_ref[...] = (y * (1.0 + sc_ref[...]) + sh_ref[...]).reshape(h2_ref.shape).astype(BF16)


def _outproj(ol, od, x, g1, sc, sh, nw, wt, wb):
    b, t, d = x.shape
    n = b * t
    lw, dw = ol.shape[-1], od.shape[-1]
    tm = _tile(n, 512, 16)
    gb, r, nt = _row_groups(b, t, tm)
    tok = lambda c: pl.BlockSpec((gb, r, c), lambda i: (i // nt, i % nt, 0))
    per = pl.BlockSpec((gb, 1, d), lambda i: (i // nt, 0, 0))
    return pl.pallas_call(
        _outproj_kernel,
        out_shape=(jax.ShapeDtypeStruct((b, t, d), F32), jax.ShapeDtypeStruct((n, d), BF16)),
        grid=(n // tm,),
        in_specs=[tok(lw), tok(dw), tok(d), per, per, per,
                  pl.BlockSpec((1, 1, d), lambda i: (0, 0, 0)),
                  pl.BlockSpec((lw, d), lambda i: (0, 0)),
                  pl.BlockSpec((dw, d), lambda i: (0, 0))],
        out_specs=(tok(d), pl.BlockSpec((tm, d), lambda i: (i, 0))),
        compiler_params=_params("arbitrary"),
        name="outproj",
    )(ol, od, x, g1, sc, sh, nw, wt, wb)


def _sorting_network(n):
    pairs = []
    p = 1
    while p < n:
        k = p
        while k >= 1:
            for j in range(k % p, n - k, 2 * k):
                for i in range(min(k, n - j - k)):
                    if (i + j) // (2 * p) == (i + j + k) // (2 * p):
                        pairs.append((i + j, i + j + k))
            k //= 2
        p *= 2
    return pairs


def _merge_top(lists, singles, count):
    lists = list(lists)
    singles = list(singles)
    vals = []
    for r in range(count):
        head = lists[0]
        for x in singles:
            head = jnp.maximum(head, x)
        m = jnp.max(head, axis=0, keepdims=True)
        vals.append(m)
        left = count - r - 1
        if left == 0:
            break
        eq = lists[0] == m
        keep = min(len(lists), left)
        lists = [jnp.where(eq, lists[v + 1] if v + 1 < len(lists) else -jnp.inf, lists[v]) for v in range(keep)]
        singles = [jnp.where(x == m, -jnp.inf, x) for x in singles]
    return vals


def _top_values(s, count):
    tiles = [s[8 * v:8 * (v + 1)] for v in range(s.shape[0] // 8)]
    for i, j in _sorting_network(len(tiles)):
        tiles[i], tiles[j] = jnp.maximum(tiles[i], tiles[j]), jnp.minimum(tiles[i], tiles[j])
    return _merge_top(tiles, [], count)


def _router_kernel(h_ref, wq_ref, sk_ref, s2_ref, thr_ref, r_ref, q_ref, *, nheads):
    nk = sk_ref.shape[1]
    half = sk_ref.shape[2]
    q_ref[...] = lax.dot_general(wq_ref[...], h_ref[...], NT_DIMS, preferred_element_type=F32)
    p = h_ref.shape[0]
    sub = lax.broadcasted_iota(jnp.int32, (8, p), 0)

    def head(hd, carry):
        base = pl.multiple_of(hd * (2 * half), 2 * half)
        q1 = q_ref[pl.ds(base, half), :].astype(BF16)
        q2 = q_ref[pl.ds(base + half, half), :].astype(BF16)
        s1 = jnp.dot(sk_ref[2 * hd], q1, preferred_element_type=F32)
        s2 = jnp.dot(sk_ref[2 * hd + 1], q2, preferred_element_type=F32)
        nv = PEER_TOPK + 1
        t1 = _top_values(s1, nv)
        t2 = _top_values(s2, nv)
        t2_lo = jnp.concatenate(t2[:8], axis=0)
        t2_hi = jnp.concatenate(t2[8:16], axis=0)
        lists = [t1[0] + t2_lo]
        for a in range(1, nv):
            nb = nv // (a + 1)
            c = t1[a] + t2_lo
            lists.append(c if nb >= 8 else jnp.where(sub < nb, c, -jnp.inf))
        singles = [t1[0] + t2_hi, jnp.where(sub < 1, t1[0] + t2[16], -jnp.inf)]
        cand = jnp.concatenate(lists + singles, axis=0)
        best = _merge_top(lists, singles, nv)
        tau = best[PEER_TOPK - 1]
        mid = 0.5 * (tau + best[PEER_TOPK])
        top = t1[0] + t2[0]
        z = jnp.sum(jnp.where(cand >= tau, jnp.exp(cand - top), 0.0), axis=0, keepdims=True)
        s2_ref[hd] = s2 * LOG2E
        thr_ref[hd] = (mid - s1) * LOG2E
        r_ref[hd] = (s1 - (top + jnp.log(z))) * LOG2E
        return carry

    lax.fori_loop(0, nheads, head, 0)


def _router(h2, wq_t, sk, *, tp):
    n, d = h2.shape
    nheads = sk.shape[0] // 2
    nk = sk.shape[1]
    hq = wq_t.shape[0]
    out = jax.ShapeDtypeStruct((nheads, nk, n), F32)
    oblk = pl.BlockSpec((nheads, nk, tp), lambda i: (0, 0, i))
    return pl.pallas_call(
        functools.partial(_router_kernel, nheads=nheads),
        out_shape=(out, out, out),
        grid=(n // tp,),
        in_specs=[pl.BlockSpec((tp, d), lambda i: (i, 0)),
                  pl.BlockSpec((hq, d), lambda i: (0, 0)),
                  pl.BlockSpec(sk.shape, lambda i: (0, 0, 0))],
        out_specs=(oblk, oblk, oblk),
        scratch_shapes=[pltpu.VMEM((hq, tp), F32)],
        compiler_params=_params("arbitrary"),
        name="router",
    )(h2, wq_t, sk)


def _dense_kernel(h_ref, u_ref, vt_ref, s2_ref, thr_ref, r_ref, x1_ref, g2_ref, fw_ref,
                  y_ref, acc_ref, st_ref, coef_ref, *, nheads, rows_per_step, n_chunks):
    s = pl.program_id(1)
    nk = s2_ref.shape[1]
    tp = h_ref.shape[0]

    @pl.when(s == 0)
    def _():
        acc_ref[...] = jnp.zeros_like(acc_ref)
        st_ref[...] = jnp.zeros_like(st_ref)

    cur = s % 2
    prev = 1 - cur
    chunk = jnp.clip(s - 1, 0, n_chunks - 1)
    for ii in range(rows_per_step):
        i1 = chunk * rows_per_step + ii
        rows = slice(ii * nk, (ii + 1) * nk)
        thr_rows = [thr_ref[hd, pl.ds(i1, 1), :] for hd in range(nheads)]
        r_rows = [r_ref[hd, pl.ds(i1, 1), :] for hd in range(nheads)]
        for pb in range(tp // LANES):
            cols = slice(pb * LANES, (pb + 1) * LANES)
            gate = None
            for hd in range(nheads):
                s2 = s2_ref[hd, :, cols]
                term = jnp.where(s2 >= thr_rows[hd][:, cols], jnp.exp2(s2 + r_rows[hd][:, cols]), 0.0)
                gate = term if gate is None else gate + term
            coef_ref[rows, cols] = (jax.nn.gelu(st_ref[prev, rows, cols]) * gate).astype(BF16)
    acc_ref[...] += jnp.dot(vt_ref[0], coef_ref[...], preferred_element_type=F32)
    st_ref[cur] = lax.dot_general(u_ref[...], h_ref[...], NT_DIMS, preferred_element_type=F32)

    @pl.when(s == pl.num_programs(1) - 1)
    def _():
        peer = acc_ref[...].T.reshape(x1_ref.shape)
        x2 = x1_ref[...] + g2_ref[...] * peer
        y_ref[...] = x2 * lax.rsqrt(jnp.mean(x2 * x2, axis=-1, keepdims=True) + NORM_EPS) * fw_ref[...]


def _dense(h2, u16, vt16, s2t, thr, rl, x1, g2, fw, *, tp):
    b, t, d = x1.shape
    n = b * t
    nheads, nk, _ = s2t.shape
    n_chunks, _, ce = vt16.shape
    rows_per_step = ce // nk
    assert u16.shape[0] == n_chunks * ce
    gb, r, nt = _row_groups(b, t, tp)
    rblk = pl.BlockSpec((nheads, nk, tp), lambda i, s: (0, 0, i))
    tok = pl.BlockSpec((gb, r, d), lambda i, s: (i // nt, i % nt, 0))
    return pl.pallas_call(
        functools.partial(_dense_kernel, nheads=nheads, rows_per_step=rows_per_step, n_chunks=n_chunks),
        out_shape=jax.ShapeDtypeStruct((b, t, d), F32),
        grid=(n // tp, n_chunks + 1),
        in_specs=[pl.BlockSpec((tp, d), lambda i, s: (i, 0)),
                  pl.BlockSpec((ce, d), lambda i, s: (jnp.minimum(s, n_chunks - 1), 0)),
                  pl.BlockSpec((1, d, ce), lambda i, s: (jnp.clip(s - 1, 0, n_chunks - 1), 0, 0)),
                  rblk, rblk, rblk, tok,
                  pl.BlockSpec((gb, 1, d), lambda i, s: (i // nt, 0, 0)),
                  pl.BlockSpec((1, 1, d), lambda i, s: (0, 0, 0))],
        out_specs=tok,
        scratch_shapes=[pltpu.VMEM((d, tp), F32), pltpu.VMEM((2, ce, tp), F32), pltpu.VMEM((ce, tp), BF16)],
        compiler_params=_params("arbitrary", "arbitrary"),
        name="dense",
    )(h2, u16, vt16, s2t, thr, rl, x1, g2, fw)


def _group(x, mod, lru_conv, lru_h, dn_conv, dn_s, reset_first, p, fw, *, lru_tiles, dn_tiles, tp):
    b, t, d = x.shape
    sh1, sc1, g1, sh2, sc2, g2 = [m.reshape(b, 1, d) for m in jnp.split(mod, 6, axis=-1)]
    pm, ps = _inproj(x, sc1, sh1, p['norm_mix_w'], p['w_main'], p['w_small'])
    pm3 = pm.reshape(b, t, -1)
    ps3 = ps.reshape(b, t, LANES)
    lw = p['lru_lambda'].shape[-1]
    out_lru, new_h = _lru(pm3, lru_conv, lru_h, p['lru_conv_w'], p['lru_conv_b'], p['lru_wg'], p['lru_ba'],
                          p['lru_bx'], p['lru_lambda'], reset_first=reset_first, bb=lru_tiles[0], tc=lru_tiles[1])
    out_dn, new_s = _delta(pm3, ps3, dn_conv, dn_s, p['dn_conv_w'], p['dn_A_log'], p['dn_dt_bias'], p['dn_norm_w'],
                           bb=dn_tiles[0], tb=dn_tiles[1], chunk=dn_tiles[2])
    keep = CONV_W - 1
    new_lru_conv = pm3[:, t - keep:, :lw]
    new_dn_conv = pm3[:, t - keep:, 2 * lw:2 * lw + dn_conv.shape[-1]]
    x1, h2 = _outproj(out_lru, out_dn, x, g1, sc2, sh2, p['norm_ffn_w'], p['w_out_top'], p['w_out_bot'])
    s2t, thr, r = _router(h2, p['wq_t'], p['subkeys'], tp=tp)
    y = _dense(h2, p['peer_u'], p['peer_vt'], s2t, thr, r, x1, g2, fw, tp=tp)
    return y, (new_lru_conv, new_h, new_dn_conv, new_s)


def kernel(x_prompt, x_sample, state_lru_conv, state_lru_h, state_dn_conv, state_dn_S, c_prompt, c_sample,
           w_ada, b_ada, norm_mix_w, norm_ffn_w, w_in, lru_conv_w, lru_conv_b, lru_wa, lru_ba, lru_wx, lru_bx,
           lru_lambda, dn_conv_w, dn_A_log, dn_dt_bias, dn_norm_w, w_out, peer_wq, peer_subkeys, peer_u, peer_v,
           final_norm_w):
    depth = w_ada.shape[0]
    assert depth == 1, "the final norm is fused into the last layer's expert kernel"
    bp, seq, d = x_prompt.shape
    bs, dseq, _ = x_sample.shape
    lw = lru_lambda.shape[-1]
    dn_heads = dn_A_log.shape[-1]
    dn_w = dn_heads * LANES
    conv_ch = dn_conv_w.shape[-1]
    assert lru_wa.shape[-1] == LANES and lw == dn_w and conv_ch == 3 * dn_w
    assert peer_subkeys.shape[-1] == LANES and peer_subkeys.shape[-2] == LANES
    main = 2 * lw + conv_ch + dn_w
    l = 0
    w_small = jnp.zeros((d, LANES), F32).at[:, :2 * dn_heads].set(w_in[l][:, main:]).astype(BF16)
    nheads = peer_subkeys.shape[1]
    p = {
        'norm_mix_w': norm_mix_w[l].reshape(1, 1, d), 'norm_ffn_w': norm_ffn_w[l].reshape(1, 1, d),
        'w_main': w_in[l][:, :main].astype(BF16), 'w_small': w_small,
        'lru_conv_w': lru_conv_w[l], 'lru_conv_b': lru_conv_b[l],
        'lru_wg': jnp.concatenate([lru_wa[l], lru_wx[l]], axis=-1).astype(BF16),
        'lru_ba': lru_ba[l], 'lru_bx': lru_bx[l], 'lru_lambda': lru_lambda[l],
        'dn_conv_w': dn_conv_w[l], 'dn_A_log': dn_A_log[l], 'dn_dt_bias': dn_dt_bias[l], 'dn_norm_w': dn_norm_w[l],
        'w_out_top': w_out[l][:lw].astype(BF16), 'w_out_bot': w_out[l][lw:].astype(BF16),
        'wq_t': peer_wq[l].T.astype(BF16),
        'subkeys': peer_subkeys[l].reshape(nheads * 2, LANES, LANES).astype(BF16),
        'peer_u': peer_u[l].astype(BF16),
        'peer_vt': peer_v[l].reshape(-1, PEER_CHUNK, d).transpose(0, 2, 1).astype(BF16),
    }
    fw = final_norm_w.reshape(1, 1, d)
    mod = _ada(jnp.concatenate([c_prompt, c_sample], axis=0), w_ada[l], b_ada[l])
    zeros = lambda *s: jnp.zeros(s, F32)
    yp, sp = _group(x_prompt, mod[:bp], zeros(bp, CONV_W - 1, lw), zeros(bp, lw), zeros(bp, CONV_W - 1, conv_ch),
                    zeros(bp, dn_heads, LANES, LANES), True, p, fw,
                    lru_tiles=(bp, _tile(seq, 128, 8)), dn_tiles=(_tile(bp, 4, 1), _tile(seq, 128, DN_CHUNK), min(DN_CHUNK, seq)),
                    tp=_tile(bp * seq, 512, LANES))
    ys, ss = _group(x_sample, mod[bp:], state_lru_conv[l], state_lru_h[l], state_dn_conv[l], state_dn_S[l],
                    False, p, fw,
                    lru_tiles=(_tile(bs, 32, 1), dseq), dn_tiles=(_tile(bs, DN_CHUNK // dseq, 1), dseq, dseq),
                    tp=_tile(bs * dseq, 512, LANES))
    stack = lambda v: v[None]
    return (yp, ys, stack(sp[0]), stack(sp[1]), stack(sp[2]), stack(sp[3]),
            stack(ss[0]), stack(ss[1]), stack(ss[2]), stack(ss[3]))
```

```python
import functools
import math

import jax
import jax.numpy as jnp
from jax import lax
from jax.experimental import pallas as pl
from jax.experimental.pallas import tpu as pltpu

F32 = jnp.float32
BF16 = jnp.bfloat16
NORM_EPS = 1e-6
LRU_C = 8.0
CONV_W = 4
CONV_PAD = 8
PEER_TOPK = 16
PEER_CHUNK = 1024
DN_CHUNK = 64
LOG2E = math.log2(math.e)
LANES = 128
VMEM_LIMIT_BYTES = 60 * 1024 * 1024
NT_DIMS = (((1,), (1,)), ((), ()))
TN_DIMS = (((0,), (0,)), ((), ()))


def _params(*semantics):
    return pltpu.CompilerParams(dimension_semantics=semantics, vmem_limit_bytes=VMEM_LIMIT_BYTES)


def _tile(n, target, multiple):
    if n <= target:
        return n
    t = (target // multiple) * multiple
    while t >= multiple:
        if n % t == 0:
            return t
        t -= multiple
    raise ValueError(f"no tile for {n} <= {target} in multiples of {multiple}")


def _softplus(x):
    return jnp.maximum(x, 0.0) + jnp.log1p(jnp.exp(-jnp.abs(x)))


def _row_groups(batch, seq, tm):
    r = min(seq, tm)
    assert seq % r == 0 and tm % r == 0 and (batch * seq) % tm == 0
    return tm // r, r, seq // r


def _ada_kernel(c_ref, w_ref, b_ref, o_ref):
    c = c_ref[...]
    a = (c * jax.nn.sigmoid(c)).astype(BF16)
    o_ref[...] = jnp.dot(a, w_ref[...].astype(BF16), preferred_element_type=F32) + b_ref[...]


def _ada(c, w, b):
    bc, d = c.shape
    n = w.shape[1]
    tn = _tile(n, 1024, LANES)
    return pl.pallas_call(
        _ada_kernel,
        out_shape=jax.ShapeDtypeStruct((bc, n), F32),
        grid=(n // tn,),
        in_specs=[pl.BlockSpec((bc, d), lambda j: (0, 0)),
                  pl.BlockSpec((d, tn), lambda j: (0, j)),
                  pl.BlockSpec((1, tn), lambda j: (0, j))],
        out_specs=pl.BlockSpec((bc, tn), lambda j: (0, j)),
        compiler_params=_params("arbitrary"),
        name="ada",
    )(c, w, b.reshape(1, n))


def _inproj_kernel(x_ref, sc_ref, sh_ref, nw_ref, wm_ref, ws_ref, om_ref, os_ref, h_ref):
    @pl.when(pl.program_id(1) == 0)
    def _():
        x = x_ref[...]
        y = x * lax.rsqrt(jnp.mean(x * x, axis=-1, keepdims=True) + NORM_EPS) * nw_ref[...]
        h = y * (1.0 + sc_ref[...]) + sh_ref[...]
        hb = h.reshape(h_ref.shape).astype(BF16)
        h_ref[...] = hb
        os_ref[...] = jnp.dot(hb, ws_ref[...], preferred_element_type=F32)

    om_ref[...] = jnp.dot(h_ref[...], wm_ref[...], preferred_element_type=F32)


def _inproj(x, sc, sh, nw, wm, ws):
    b, t, d = x.shape
    n = b * t
    m = wm.shape[1]
    tm = _tile(n, 1024, 8)
    tn = _tile(m, 1536, LANES)
    gb, r, nt = _row_groups(b, t, tm)
    return pl.pallas_call(
        _inproj_kernel,
        out_shape=(jax.ShapeDtypeStruct((n, m), F32), jax.ShapeDtypeStruct((n, LANES), F32)),
        grid=(n // tm, m // tn),
        in_specs=[pl.BlockSpec((gb, r, d), lambda i, j: (i // nt, i % nt, 0)),
                  pl.BlockSpec((gb, 1, d), lambda i, j: (i // nt, 0, 0)),
                  pl.BlockSpec((gb, 1, d), lambda i, j: (i // nt, 0, 0)),
                  pl.BlockSpec((1, 1, d), lambda i, j: (0, 0, 0)),
                  pl.BlockSpec((d, tn), lambda i, j: (0, j)),
                  pl.BlockSpec((d, LANES), lambda i, j: (0, 0))],
        out_specs=(pl.BlockSpec((tm, tn), lambda i, j: (i, j)),
                   pl.BlockSpec((tm, LANES), lambda i, j: (i, 0))),
        scratch_shapes=[pltpu.VMEM((tm, d), BF16)],
        compiler_params=_params("arbitrary", "arbitrary"),
        name="inproj",
    )(x, sc, sh, nw, wm, ws)


def _causal_conv(x_ref, xs_ref, cw_ref):
    t = x_ref.shape[1]
    xs_ref[:, CONV_PAD:CONV_PAD + t, :] = x_ref[...]
    cw = cw_ref[...]
    y = None
    for k in range(CONV_W):
        lo = CONV_PAD - (CONV_W - 1) + k
        term = xs_ref[:, lo:lo + t, :] * cw[k:k + 1, :][None]
        y = term if y is None else y + term
    xs_ref[:, CONV_PAD - (CONV_W - 1):CONV_PAD, :] = xs_ref[:, t + CONV_PAD - (CONV_W - 1):t + CONV_PAD, :]
    return y


def _lru_kernel(xl_ref, yl_ref, cs_ref, h0_ref, cw_ref, cb_ref, wg_ref, ba_ref, bx_ref, lam_ref,
                o_ref, hn_ref, xs_ref, a_ref, u_ref, h_ref, *, reset_first):
    ti = pl.program_id(1)
    bb, tc, lw = xl_ref.shape
    nheads = lw // LANES

    @pl.when(ti == 0)
    def _():
        xs_ref[:, CONV_PAD - (CONV_W - 1):CONV_PAD, :] = cs_ref[...]
        h_ref[...] = h0_ref[...]

    xc = (_causal_conv(xl_ref, xs_ref, cw_ref) + cb_ref[...][None]).reshape(bb * tc, lw)
    sp = _softplus(-lam_ref[...])
    t_glob = lax.broadcasted_iota(jnp.int32, (bb, tc, LANES), 1) + ti * tc
    for hd in range(nheads):
        sl = slice(hd * LANES, (hd + 1) * LANES)
        xh = xc[:, sl]
        gates = jnp.dot(xh.astype(BF16), wg_ref[hd], preferred_element_type=F32)
        r = jax.nn.sigmoid(gates[:, :LANES] + ba_ref[:, sl])
        i = jax.nn.sigmoid(gates[:, LANES:] + bx_ref[:, sl])
        log_a = -LRU_C * r * sp[:, sl]
        a = jnp.exp(log_a)
        mult = jnp.sqrt(jnp.maximum(-jnp.tanh(log_a) * (a * a + 1.0), 0.0))
        u = i * xh
        a3 = a.reshape(bb, tc, LANES)
        m3 = mult.reshape(bb, tc, LANES)
        if reset_first:
            m3 = jnp.where(t_glob == 0, 1.0, m3)
        a_ref[:, :, sl] = a3
        u_ref[:, :, sl] = m3 * u.reshape(bb, tc, LANES)

    def step(t, h):
        h = a_ref[:, pl.ds(t, 1), :] * h + u_ref[:, pl.ds(t, 1), :]
        u_ref[:, pl.ds(t, 1), :] = h
        return h

    h = lax.fori_loop(0, tc, step, h_ref[...], unroll=8)
    h_ref[...] = h
    hn_ref[...] = h
    o_ref[...] = u_ref[...] * jax.nn.gelu(yl_ref[...])


def _lru(pm3, conv_state, h0, cw, cb, wg, ba, bx, lam, *, reset_first, bb, tc):
    b, t, _ = pm3.shape
    lw = lam.shape[-1]
    assert b % bb == 0 and t % tc == 0 and tc % 8 == 0
    row = lambda v: v.reshape(1, lw)
    kern = functools.partial(_lru_kernel, reset_first=reset_first)
    out, hn = pl.pallas_call(
        kern,
        out_shape=(jax.ShapeDtypeStruct((b, t, lw), F32), jax.ShapeDtypeStruct((b, 1, lw), F32)),
        grid=(b // bb, t // tc),
        in_specs=[pl.BlockSpec((bb, tc, lw), lambda i, j: (i, j, 0)),
                  pl.BlockSpec((bb, tc, lw), lambda i, j: (i, j, 1)),
                  pl.BlockSpec((bb, CONV_W - 1, lw), lambda i, j: (i, 0, 0)),
                  pl.BlockSpec((bb, 1, lw), lambda i, j: (i, 0, 0)),
                  pl.BlockSpec((CONV_W, lw), lambda i, j: (0, 0)),
                  pl.BlockSpec((1, lw), lambda i, j: (0, 0)),
                  pl.BlockSpec(wg.shape, lambda i, j: (0, 0, 0)),
                  pl.BlockSpec((1, lw), lambda i, j: (0, 0)),
                  pl.BlockSpec((1, lw), lambda i, j: (0, 0)),
                  pl.BlockSpec((1, lw), lambda i, j: (0, 0))],
        out_specs=(pl.BlockSpec((bb, tc, lw), lambda i, j: (i, j, 0)),
                   pl.BlockSpec((bb, 1, lw), lambda i, j: (i, 0, 0))),
        scratch_shapes=[pltpu.VMEM((bb, tc + CONV_PAD, lw), F32),
                        pltpu.VMEM((bb, tc, lw), F32),
                        pltpu.VMEM((bb, tc, lw), F32),
                        pltpu.VMEM((bb, 1, lw), F32)],
        compiler_params=_params("arbitrary", "arbitrary"),
        name="lru",
    )(pm3, pm3, conv_state, h0.reshape(b, 1, lw), cw, row(cb), wg, row(ba), row(bx), row(lam))
    return out, hn.reshape(b, lw)


def _delta_kernel(q_ref, k_ref, v_ref, z_ref, ps_ref, csq_ref, csk_ref, csv_ref, s0_ref,
                  cwq_ref, cwk_ref, cwv_ref, ea_ref, dtb_ref, nw_ref,
                  o_ref, sn_ref,
                  xq_ref, xk_ref, xv_ref, qn_ref, kn_ref, vn_ref, beta_ref, g_ref, *, chunk, nheads):
    ti = pl.program_id(1)
    bb, tb, w = q_ref.shape
    n_chunks = tb // chunk
    stacked = n_chunks == 1
    units = 1 if stacked else bb
    segs = bb if stacked else 1
    rows = segs * chunk
    lo = CONV_PAD - (CONV_W - 1)

    @pl.when(ti == 0)
    def _():
        xq_ref[:, lo:CONV_PAD, :] = csq_ref[...]
        xk_ref[:, lo:CONV_PAD, :] = csk_ref[...]
        xv_ref[:, lo:CONV_PAD, :] = csv_ref[...]
        sn_ref[...] = s0_ref[...]

    def conv_silu(x_ref, xs_ref, cw_ref):
        y = _causal_conv(x_ref, xs_ref, cw_ref)
        return (y * jax.nn.sigmoid(y)).reshape(bb * tb, w)

    qc = conv_silu(q_ref, xq_ref, cwq_ref)
    kc = conv_silu(k_ref, xk_ref, cwk_ref)
    vn_ref[...] = conv_silu(v_ref, xv_ref, cwv_ref)
    for hd in range(nheads):
        sl = slice(hd * LANES, (hd + 1) * LANES)
        qh = qc[:, sl]
        kh = kc[:, sl]
        qn_ref[:, sl] = qh * (lax.rsqrt(jnp.sum(qh * qh, axis=-1, keepdims=True) + NORM_EPS) * (LANES ** -0.5))
        kn_ref[:, sl] = kh * lax.rsqrt(jnp.sum(kh * kh, axis=-1, keepdims=True) + NORM_EPS)
    ps = ps_ref[...].reshape(bb * tb, LANES)
    beta_ref[...] = jax.nn.sigmoid(ps)
    g_ref[...] = -ea_ref[...] * _softplus(ps + dtb_ref[...])

    ii = lax.broadcasted_iota(jnp.int32, (rows, rows), 0)
    jj = lax.broadcasted_iota(jnp.int32, (rows, rows), 1)
    if segs == 1:
        le = jj <= ii
        lt = jj < ii
    else:
        same = (ii // chunk) == (jj // chunk)
        le = jnp.logical_and(same, jj <= ii)
        lt = jnp.logical_and(same, jj < ii)
    le_b = jnp.where(le, 1.0, 0.0).astype(BF16)
    assert chunk & (chunk - 1) == 0 and chunk >= 2
    levels = []
    b = 1
    while b < chunk:
        sh = b.bit_length()
        join = jnp.logical_and((ii >> sh) == (jj >> sh), (ii >> (sh - 1)) != (jj >> (sh - 1)))
        levels.append(jnp.logical_and(join, lt))
        b *= 2
    lanes = [slice(hd * LANES, (hd + 1) * LANES) for hd in range(nheads)]
    probs = [(u, hd) for u in range(units) for hd in range(nheads)]

    def mm(a, b):
        return jnp.dot(a.astype(BF16), b.astype(BF16), preferred_element_type=F32)

    def do_chunk(c, carry):
        rsl, gc, gct, eg, beta = [], [], [], [], []
        for u in range(units):
            rs = pl.ds(pl.multiple_of(u * tb + c * rows, rows), rows)
            g = g_ref[rs, :]
            g_hi = g.astype(BF16)
            g_lo = (g - g_hi.astype(F32)).astype(BF16)
            gcu = (jnp.dot(le_b, g_hi, preferred_element_type=F32)
                   + jnp.dot(le_b, g_lo, preferred_element_type=F32))
            gcp = gcu if rows == LANES else jnp.concatenate([gcu, jnp.zeros((LANES - rows, LANES), F32)], axis=0)
            rsl.append(rs)
            gc.append(gcu)
            gct.append(gcp.T)
            eg.append(jnp.exp(gcu))
            beta.append(beta_ref[rs, :])
        q = [qn_ref[rsl[u], lanes[hd]] for u, hd in probs]
        k = [kn_ref[rsl[u], lanes[hd]] for u, hd in probs]
        v = [vn_ref[rsl[u], lanes[hd]] for u, hd in probs]
        bcol = [beta[u][:, hd:hd + 1] for u, hd in probs]
        gcol = [gc[u][:, nheads + hd:nheads + hd + 1] for u, hd in probs]
        egc = [eg[u][:, nheads + hd:nheads + hd + 1] for u, hd in probs]
        np_ = range(len(probs))
        decay = [jnp.where(le, jnp.exp(jnp.where(le, gcol[i] - gct[u][nheads + hd:nheads + hd + 1, :rows], 0.0)), 0.0)
                 for i, (u, hd) in enumerate(probs)]
        kb = [k[i] * bcol[i] for i in np_]
        k16 = [k[i].astype(BF16) for i in np_]
        kk = [lax.dot_general(kb[i].astype(BF16), k16[i], NT_DIMS, preferred_element_type=F32) for i in np_]
        qk = [lax.dot_general(q[i].astype(BF16), k16[i], NT_DIMS, preferred_element_type=F32) for i in np_]
        lmat = [jnp.where(lt, kk[i] * decay[i], 0.0) for i in np_]
        attn = [jnp.where(le, qk[i] * decay[i], 0.0).astype(BF16) for i in np_]
        nmat = [jnp.where(levels[0], -lmat[i], 0.0) for i in np_]
        for lvl in levels[1:]:
            off = [jnp.where(lvl, lmat[i], 0.0) for i in np_]
            m = [off[i] + mm(nmat[i], off[i]) for i in np_]
            nmat = [nmat[i] - m[i] - mm(m[i], nmat[i]) for i in np_]
        rhs = [jnp.concatenate([v[i] * bcol[i], kb[i] * egc[i]], axis=1) for i in np_]
        sol = [rhs[i] + mm(nmat[i], rhs[i]) for i in np_]
        qe = [q[i] * egc[i] for i in np_]
        seq = lambda u, b: b if stacked else u
        s_old = [[sn_ref[seq(u, b), hd] for b in range(segs)] for u, hd in probs]
        both = [[jnp.dot(jnp.concatenate([sol[i][b * chunk:(b + 1) * chunk, LANES:],
                                          qe[i][b * chunk:(b + 1) * chunk]], axis=0).astype(BF16),
                         s_old[i][b].astype(BF16), preferred_element_type=F32) for b in range(segs)] for i in np_]
        cat = lambda parts: parts[0] if segs == 1 else jnp.concatenate(parts, axis=0)
        v16 = [(sol[i][:, :LANES] - cat([both[i][b][:chunk] for b in range(segs)])).astype(BF16) for i in np_]
        o = [cat([both[i][b][chunk:] for b in range(segs)]) + jnp.dot(attn[i], v16[i], preferred_element_type=F32)
             for i in np_]
        for i, (u, hd) in enumerate(probs):
            for b in range(segs):
                seg = slice(b * chunk, (b + 1) * chunk)
                g_last = gcol[i][(b + 1) * chunk - 1:(b + 1) * chunk]
                kd = (k[i][seg] * jnp.exp(g_last - gcol[i][seg])).astype(BF16)
                sn_ref[seq(u, b), hd] = s_old[i][b] * jnp.exp(g_last) + lax.dot_general(
                    kd, v16[i][seg], TN_DIMS, preferred_element_type=F32)
        for i, (u, hd) in enumerate(probs):
            sl = lanes[hd]
            on = o[i] * lax.rsqrt(jnp.mean(o[i] * o[i], axis=-1, keepdims=True) + NORM_EPS) * nw_ref[...]
            if stacked:
                zz = z_ref[:, :, sl].reshape(rows, LANES)
                o_ref[:, :, sl] = (on * (zz * jax.nn.sigmoid(zz))).reshape(bb, chunk, LANES)
            else:
                tsl = pl.ds(pl.multiple_of(c * chunk, chunk), chunk)
                zz = z_ref[u, tsl, sl]
                o_ref[u, tsl, sl] = on * (zz * jax.nn.sigmoid(zz))
        return carry

    if n_chunks == 1:
        do_chunk(0, 0)
    else:
        lax.fori_loop(0, n_chunks, do_chunk, 0)


def _delta(pm3, ps3, conv_state, s0, cw, a_log, dt_bias, norm_w, *, bb, tb, chunk):
    b, t, _ = pm3.shape
    nheads = a_log.shape[-1]
    w = nheads * LANES
    assert s0.shape == (b, nheads, LANES, LANES) and nheads <= 8
    assert b % bb == 0 and t % tb == 0 and tb % chunk == 0 and chunk % 8 == 0
    assert tb > chunk or bb * chunk <= LANES
    pad = lambda v: jnp.zeros((1, LANES), F32).at[0, nheads:2 * nheads].set(v)
    kern = functools.partial(_delta_kernel, chunk=chunk, nheads=nheads)
    col = lambda c: pl.BlockSpec((bb, tb, w), lambda i, j, c=c: (i, j, c))
    cst = lambda c: pl.BlockSpec((bb, CONV_W - 1, w), lambda i, j, c=c: (i, 0, c))
    cwt = lambda c: pl.BlockSpec((CONV_W, w), lambda i, j, c=c: (0, c))
    vec = pl.BlockSpec((1, LANES), lambda i, j: (0, 0))
    sblk = pl.BlockSpec((bb, nheads, LANES, LANES), lambda i, j: (i, 0, 0, 0))
    out, sn = pl.pallas_call(
        kern,
        out_shape=(jax.ShapeDtypeStruct((b, t, w), F32), jax.ShapeDtypeStruct(s0.shape, F32)),
        grid=(b // bb, t // tb),
        in_specs=[col(2), col(3), col(4), col(5),
                  pl.BlockSpec((bb, tb, LANES), lambda i, j: (i, j, 0)),
                  cst(0), cst(1), cst(2), sblk, cwt(0), cwt(1), cwt(2), vec, vec, vec],
        out_specs=(pl.BlockSpec((bb, tb, w), lambda i, j: (i, j, 0)), sblk),
        scratch_shapes=[pltpu.VMEM((bb, tb + CONV_PAD, w), F32)] * 3
                       + [pltpu.VMEM((bb * tb, w), F32)] * 3
                       + [pltpu.VMEM((bb * tb, LANES), F32)] * 2,
        compiler_params=_params("arbitrary", "arbitrary"),
        name="delta",
    )(pm3, pm3, pm3, pm3, ps3, conv_state, conv_state, conv_state, s0, cw, cw, cw,
      pad(jnp.exp(a_log.astype(F32))), pad(dt_bias.astype(F32)), norm_w.reshape(1, LANES))
    return out, sn


def _outproj_kernel(ol_ref, od_ref, x_ref, g1_ref, sc_ref, sh_ref, nw_ref, wt_ref, wb_ref, x1_ref, h2_ref):
    tm = h2_ref.shape[0]
    a = ol_ref[...].reshape(tm, ol_ref.shape[-1]).astype(BF16)
    b = od_ref[...].reshape(tm, od_ref.shape[-1]).astype(BF16)
    mix = (jnp.dot(a, wt_ref[...], preferred_element_type=F32)
           + jnp.dot(b, wb_ref[...], preferred_element_type=F32))
    x1 = x_ref[...] + g1_ref[...] * mix.reshape(x_ref.shape)
    x1_ref[...] = x1
    y = x1 * lax.rsqrt(jnp.mean(x1 * x1, axis=-1, keepdims=True) + NORM_EPS) * nw_ref[...]
    h2_ref[...] = (y * (1.0 + sc_ref[...]) + sh_ref[...]).reshape(h2_ref.shape).astype(BF16)


def _outproj(ol, od, x, g1, sc, sh, nw, wt, wb):
    b, t, d = x.shape
    n = b * t
    lw, dw = ol.shape[-1], od.shape[-1]
    tm = _tile(n, 512, 16)
    gb, r, nt = _row_groups(b, t, tm)
    tok = lambda c: pl.BlockSpec((gb, r, c), lambda i: (i // nt, i % nt, 0))
    per = pl.BlockSpec((gb, 1, d), lambda i: (i // nt, 0, 0))
    return pl.pallas_call(
        _outproj_kernel,
        out_shape=(jax.ShapeDtypeStruct((b, t, d), F32), jax.ShapeDtypeStruct((n, d), BF16)),
        grid=(n // tm,),
        in_specs=[tok(lw), tok(dw), tok(d), per, per, per,
                  pl.BlockSpec((1, 1, d), lambda i: (0, 0, 0)),
                  pl.BlockSpec((lw, d), lambda i: (0, 0)),
                  pl.BlockSpec((dw, d), lambda i: (0, 0))],
        out_specs=(tok(d), pl.BlockSpec((tm, d), lambda i: (i, 0))),
        compiler_params=_params("arbitrary"),
        name="outproj",
    )(ol, od, x, g1, sc, sh, nw, wt, wb)


def _sorting_network(n):
    pairs = []
    p = 1
    while p < n:
        k = p
        while k >= 1:
            for j in range(k % p, n - k, 2 * k):
                for i in range(min(k, n - j - k)):
                    if (i + j) // (2 * p) == (i + j + k) // (2 * p):
                        pairs.append((i + j, i + j + k))
            k //= 2
        p *= 2
    return pairs


def _merge_top(lists, singles, count):
    lists = list(lists)
    singles = list(singles)
    vals = []
    for r in range(count):
        head = lists[0]
        for x in singles:
            head = jnp.maximum(head, x)
        m = jnp.max(head, axis=0, keepdims=True)
        vals.append(m)
        left = count - r - 1
        if left == 0:
            break
        eq = lists[0] == m
        keep = min(len(lists), left)
        lists = [jnp.where(eq, lists[v + 1] if v + 1 < len(lists) else -jnp.inf, lists[v]) for v in range(keep)]
        singles = [jnp.where(x == m, -jnp.inf, x) for x in singles]
    return vals


def _top_values(s, count):
    tiles = [s[8 * v:8 * (v + 1)] for v in range(s.shape[0] // 8)]
    for i, j in _sorting_network(len(tiles)):
        tiles[i], tiles[j] = jnp.maximum(tiles[i], tiles[j]), jnp.minimum(tiles[i], tiles[j])
    return _merge_top(tiles, [], count)


def _router_kernel(h_ref, wq_ref, sk_ref, s2_ref, thr_ref, r_ref, q_ref, *, nheads):
    nk = sk_ref.shape[1]
    half = sk_ref.shape[2]
    q_ref[...] = lax.dot_general(wq_ref[...], h_ref[...], NT_DIMS, preferred_element_type=F32)
    p = h_ref.shape[0]
    sub = lax.broadcasted_iota(jnp.int32, (8, p), 0)

    def head(hd, carry):
        base = pl.multiple_of(hd * (2 * half), 2 * half)
        q1 = q_ref[pl.ds(base, half), :].astype(BF16)
        q2 = q_ref[pl.ds(base + half, half), :].astype(BF16)
        s1 = jnp.dot(sk_ref[2 * hd], q1, preferred_element_type=F32)
        s2 = jnp.dot(sk_ref[2 * hd + 1], q2, preferred_element_type=F32)
        nv = PEER_TOPK + 1
        t1 = _top_values(s1, nv)
        t2 = _top_values(s2, nv)
        t2_lo = jnp.concatenate(t2[:8], axis=0)
        t2_hi = jnp.concatenate(t2[8:16], axis=0)
        lists = [t1[0] + t2_lo]
        for a in range(1, nv):
            nb = nv // (a + 1)
            c = t1[a] + t2_lo
            lists.append(c if nb >= 8 else jnp.where(sub < nb, c, -jnp.inf))
        singles = [t1[0] + t2_hi, jnp.where(sub < 1, t1[0] + t2[16], -jnp.inf)]
        cand = jnp.concatenate(lists + singles, axis=0)
        best = _merge_top(lists, singles, nv)
        tau = best[PEER_TOPK - 1]
        mid = 0.5 * (tau + best[PEER_TOPK])
        top = t1[0] + t2[0]
        z = jnp.sum(jnp.where(cand >= tau, jnp.exp(cand - top), 0.0), axis=0, keepdims=True)
        s2_ref[hd] = s2 * LOG2E
        thr_ref[hd] = (mid - s1) * LOG2E
        r_ref[hd] = (s1 - (top + jnp.log(z))) * LOG2E
        return carry

    lax.fori_loop(0, nheads, head, 0)


def _router(h2, wq_t, sk, *, tp):
    n, d = h2.shape
    nheads = sk.shape[0] // 2
    nk = sk.shape[1]
    hq = wq_t.shape[0]
    out = jax.ShapeDtypeStruct((nheads, nk, n), F32)
    oblk = pl.BlockSpec((nheads, nk, tp), lambda i: (0, 0, i))
    return pl.pallas_call(
        functools.partial(_router_kernel, nheads=nheads),
        out_shape=(out, out, out),
        grid=(n // tp,),
        in_specs=[pl.BlockSpec((tp, d), lambda i: (i, 0)),
                  pl.BlockSpec((hq, d), lambda i: (0, 0)),
                  pl.BlockSpec(sk.shape, lambda i: (0, 0, 0))],
        out_specs=(oblk, oblk, oblk),
        scratch_shapes=[pltpu.VMEM((hq, tp), F32)],
        compiler_params=_params("arbitrary"),
        name="router",
    )(h2, wq_t, sk)


def _dense_kernel(h_ref, u_ref, vt_ref, s2_ref, thr_ref, r_ref, x1_ref, g2_ref, fw_ref,
                  y_ref, acc_ref, st_ref, coef_ref, *, nheads, rows_per_step, n_chunks):
    s = pl.program_id(1)
    nk = s2_ref.shape[1]
    tp = h_ref.shape[0]

    @pl.when(s == 0)
    def _():
        acc_ref[...] = jnp.zeros_like(acc_ref)
        st_ref[...] = jnp.zeros_like(st_ref)

    cur = s % 2
    prev = 1 - cur
    chunk = jnp.clip(s - 1, 0, n_chunks - 1)
    for ii in range(rows_per_step):
        i1 = chunk * rows_per_step + ii
        rows = slice(ii * nk, (ii + 1) * nk)
        thr_rows = [thr_ref[hd, pl.ds(i1, 1), :] for hd in range(nheads)]
        r_rows = [r_ref[hd, pl.ds(i1, 1), :] for hd in range(nheads)]
        for pb in range(tp // LANES):
            cols = slice(pb * LANES, (pb + 1) * LANES)
            gate = None
            for hd in range(nheads):
                s2 = s2_ref[hd, :, cols]
                term = jnp.where(s2 >= thr_rows[hd][:, cols], jnp.exp2(s2 + r_rows[hd][:, cols]), 0.0)
                gate = term if gate is None else gate + term
            coef_ref[rows, cols] = (jax.nn.gelu(st_ref[prev, rows, cols]) * gate).astype(BF16)
    acc_ref[...] += jnp.dot(vt_ref[0], coef_ref[...], preferred_element_type=F32)
    st_ref[cur] = lax.dot_general(u_ref[...], h_ref[...], NT_DIMS, preferred_element_type=F32)

    @pl.when(s == pl.num_programs(1) - 1)
    def _():
        peer = acc_ref[...].T.reshape(x1_ref.shape)
        x2 = x1_ref[...] + g2_ref[...] * peer
        y_ref[...] = x2 * lax.rsqrt(jnp.mean(x2 * x2, axis=-1, keepdims=True) + NORM_EPS) * fw_ref[...]


def _dense(h2, u16, vt16, s2t, thr, rl, x1, g2, fw, *, tp):
    b, t, d = x1.shape
    n = b * t
    nheads, nk, _ = s2t.shape
    n_chunks, _, ce = vt16.shape
    rows_per_step = ce // nk
    assert u16.shape[0] == n_chunks * ce
    gb, r, nt = _row_groups(b, t, tp)
    rblk = pl.BlockSpec((nheads, nk, tp), lambda i, s: (0, 0, i), pipeline_mode=pl.Buffered(1))
    tok = pl.BlockSpec((gb, r, d), lambda i, s: (i // nt, i % nt, 0))
    tok1 = pl.BlockSpec((gb, r, d), lambda i, s: (i // nt, i % nt, 0), pipeline_mode=pl.Buffered(1))
    return pl.pallas_call(
        functools.partial(_dense_kernel, nheads=nheads, rows_per_step=rows_per_step, n_chunks=n_chunks),
        out_shape=jax.ShapeDtypeStruct((b, t, d), F32),
        grid=(n // tp, n_chunks + 1),
        in_specs=[pl.BlockSpec((tp, d), lambda i, s: (i, 0)),
                  pl.BlockSpec((ce, d), lambda i, s: (jnp.minimum(s, n_chunks - 1), 0)),
                  pl.BlockSpec((1, d, ce), lambda i, s: (jnp.clip(s - 1, 0, n_chunks - 1), 0, 0)),
                  rblk, rblk, rblk, tok1,
                  pl.BlockSpec((gb, 1, d), lambda i, s: (i // nt, 0, 0)),
                  pl.BlockSpec((1, 1, d), lambda i, s: (0, 0, 0))],
        out_specs=tok,
        scratch_shapes=[pltpu.VMEM((d, tp), F32), pltpu.VMEM((2, ce, tp), F32), pltpu.VMEM((ce, tp), BF16)],
        compiler_params=_params("arbitrary", "arbitrary"),
        name="dense",
    )(h2, u16, vt16, s2t, thr, rl, x1, g2, fw)


def _group(x, mod, lru_conv, lru_h, dn_conv, dn_s, reset_first, p, fw, *, lru_tiles, dn_tiles, tp):
    b, t, d = x.shape
    sh1, sc1, g1, sh2, sc2, g2 = [m.reshape(b, 1, d) for m in jnp.split(mod, 6, axis=-1)]
    pm, ps = _inproj(x, sc1, sh1, p['norm_mix_w'], p['w_main'], p['w_small'])
    pm3 = pm.reshape(b, t, -1)
    ps3 = ps.reshape(b, t, LANES)
    lw = p['lru_lambda'].shape[-1]
    out_lru, new_h = _lru(pm3, lru_conv, lru_h, p['lru_conv_w'], p['lru_conv_b'], p['lru_wg'], p['lru_ba'],
                          p['lru_bx'], p['lru_lambda'], reset_first=reset_first, bb=lru_tiles[0], tc=lru_tiles[1])
    out_dn, new_s = _delta(pm3, ps3, dn_conv, dn_s, p['dn_conv_w'], p['dn_A_log'], p['dn_dt_bias'], p['dn_norm_w'],
                           bb=dn_tiles[0], tb=dn_tiles[1], chunk=dn_tiles[2])
    keep = CONV_W - 1
    new_lru_conv = pm3[:, t - keep:, :lw]
    new_dn_conv = pm3[:, t - keep:, 2 * lw:2 * lw + dn_conv.shape[-1]]
    x1, h2 = _outproj(out_lru, out_dn, x, g1, sc2, sh2, p['norm_ffn_w'], p['w_out_top'], p['w_out_bot'])
    s2t, thr, r = _router(h2, p['wq_t'], p['subkeys'], tp=tp)
    y = _dense(h2, p['peer_u'], p['peer_vt'], s2t, thr, r, x1, g2, fw, tp=tp)
    return y, (new_lru_conv, new_h, new_dn_conv, new_s)


def kernel(x_prompt, x_sample, state_lru_conv, state_lru_h, state_dn_conv, state_dn_S, c_prompt, c_sample,
           w_ada, b_ada, norm_mix_w, norm_ffn_w, w_in, lru_conv_w, lru_conv_b, lru_wa, lru_ba, lru_wx, lru_bx,
           lru_lambda, dn_conv_w, dn_A_log, dn_dt_bias, dn_norm_w, w_out, peer_wq, peer_subkeys, peer_u, peer_v,
           final_norm_w):
    depth = w_ada.shape[0]
    assert depth == 1, "the final norm is fused into the last layer's expert kernel"
    bp, seq, d = x_prompt.shape
    bs, dseq, _ = x_sample.shape
    lw = lru_lambda.shape[-1]
    dn_heads = dn_A_log.shape[-1]
    dn_w = dn_heads * LANES
    conv_ch = dn_conv_w.shape[-1]
    assert lru_wa.shape[-1] == LANES and lw == dn_w and conv_ch == 3 * dn_w
    assert peer_subkeys.shape[-1] == LANES and peer_subkeys.shape[-2] == LANES
    main = 2 * lw + conv_ch + dn_w
    l = 0
    w_small = jnp.zeros((d, LANES), F32).at[:, :2 * dn_heads].set(w_in[l][:, main:]).astype(BF16)
    nheads = peer_subkeys.shape[1]
    p = {
        'norm_mix_w': norm_mix_w[l].reshape(1, 1, d), 'norm_ffn_w': norm_ffn_w[l].reshape(1, 1, d),
        'w_main': w_in[l][:, :main].astype(BF16), 'w_small': w_small,
        'lru_conv_w': lru_conv_w[l], 'lru_conv_b': lru_conv_b[l],
        'lru_wg': jnp.concatenate([lru_wa[l], lru_wx[l]], axis=-1).astype(BF16),
        'lru_ba': lru_ba[l], 'lru_bx': lru_bx[l], 'lru_lambda': lru_lambda[l],
        'dn_conv_w': dn_conv_w[l], 'dn_A_log': dn_A_log[l], 'dn_dt_bias': dn_dt_bias[l], 'dn_norm_w': dn_norm_w[l],
        'w_out_top': w_out[l][:lw].astype(BF16), 'w_out_bot': w_out[l][lw:].astype(BF16),
        'wq_t': peer_wq[l].T.astype(BF16),
        'subkeys': peer_subkeys[l].reshape(nheads * 2, LANES, LANES).astype(BF16),
        'peer_u': peer_u[l].astype(BF16),
        'peer_vt': peer_v[l].reshape(-1, PEER_CHUNK, d).transpose(0, 2, 1).astype(BF16),
    }
    fw = final_norm_w.reshape(1, 1, d)
    mod = _ada(jnp.concatenate([c_prompt, c_sample], axis=0), w_ada[l], b_ada[l])
    zeros = lambda *s: jnp.zeros(s, F32)
    yp, sp = _group(x_prompt, mod[:bp], zeros(bp, CONV_W - 1, lw), zeros(bp, lw), zeros(bp, CONV_W - 1, conv_ch),
                    zeros(bp, dn_heads, LANES, LANES), True, p, fw,
                    lru_tiles=(bp, _tile(seq, 128, 8)), dn_tiles=(_tile(bp, 4, 1), _tile(seq, 128, DN_CHUNK), min(DN_CHUNK, seq)),
                    tp=_tile(bp * seq, 512, LANES))
    ys, ss = _group(x_sample, mod[bp:], state_lru_conv[l], state_lru_h[l], state_dn_conv[l], state_dn_S[l],
                    False, p, fw,
                    lru_tiles=(_tile(bs, 32, 1), dseq), dn_tiles=(_tile(bs, DN_CHUNK // dseq, 1), dseq, dseq),
                    tp=_tile(bs * dseq, 512, LANES))
    stack = lambda v: v[None]
    return (yp, ys, stack(sp[0]), stack(sp[1]), stack(sp[2]), stack(sp[3]),
            stack(ss[0]), stack(ss[1]), stack(ss[2]), stack(ss[3]))
```

```python
import functools
import math

import jax
import jax.numpy as jnp
from jax import lax
from jax.experimental import pallas as pl
from jax.experimental.pallas import tpu as pltpu

F32 = jnp.float32
BF16 = jnp.bfloat16
NORM_EPS = 1e-6
LRU_C = 8.0
CONV_W = 4
CONV_PAD = 8
PEER_TOPK = 16
PEER_CHUNK = 512
DN_CHUNK = 64
LOG2E = math.log2(math.e)
LANES = 128
VMEM_LIMIT_BYTES = 60 * 1024 * 1024
NT_DIMS = (((1,), (1,)), ((), ()))
TN_DIMS = (((0,), (0,)), ((), ()))


def _params(*semantics):
    return pltpu.CompilerParams(dimension_semantics=semantics, vmem_limit_bytes=VMEM_LIMIT_BYTES)


def _tile(n, target, multiple):
    if n <= target:
        return n
    t = (target // multiple) * multiple
    while t >= multiple:
        if n % t == 0:
            return t
        t -= multiple
    raise ValueError(f"no tile for {n} <= {target} in multiples of {multiple}")


def _softplus(x):
    return jnp.maximum(x, 0.0) + jnp.log1p(jnp.exp(-jnp.abs(x)))


_GELU_C0 = -2.0 * math.sqrt(2.0 / math.pi) * LOG2E
_GELU_C1 = 0.044715 * _GELU_C0


def _gelu(x):
    return x / (1.0 + jnp.exp2(x * (_GELU_C0 + _GELU_C1 * (x * x))))


def _row_groups(batch, seq, tm):
    r = min(seq, tm)
    assert seq % r == 0 and tm % r == 0 and (batch * seq) % tm == 0
    return tm // r, r, seq // r


def _ada_kernel(c_ref, w_ref, b_ref, o_ref):
    c = c_ref[...]
    a = (c * jax.nn.sigmoid(c)).astype(BF16)
    o_ref[...] = jnp.dot(a, w_ref[...].astype(BF16), preferred_element_type=F32) + b_ref[...]


def _ada(c, w, b):
    bc, d = c.shape
    n = w.shape[1]
    tn = _tile(n, 1024, LANES)
    return pl.pallas_call(
        _ada_kernel,
        out_shape=jax.ShapeDtypeStruct((bc, n), F32),
        grid=(n // tn,),
        in_specs=[pl.BlockSpec((bc, d), lambda j: (0, 0)),
                  pl.BlockSpec((d, tn), lambda j: (0, j)),
                  pl.BlockSpec((1, tn), lambda j: (0, j))],
        out_specs=pl.BlockSpec((bc, tn), lambda j: (0, j)),
        compiler_params=_params("arbitrary"),
        name="ada",
    )(c, w, b.reshape(1, n))


def _inproj_kernel(x_ref, sc_ref, sh_ref, nw_ref, wm_ref, ws_ref, om_ref, os_ref, h_ref):
    @pl.when(pl.program_id(1) == 0)
    def _():
        x = x_ref[...]
        y = x * lax.rsqrt(jnp.mean(x * x, axis=-1, keepdims=True) + NORM_EPS) * nw_ref[...]
        h = y * (1.0 + sc_ref[...]) + sh_ref[...]
        hb = h.reshape(h_ref.shape).astype(BF16)
        h_ref[...] = hb
        os_ref[...] = jnp.dot(hb, ws_ref[...], preferred_element_type=F32)

    om_ref[...] = jnp.dot(h_ref[...], wm_ref[...], preferred_element_type=F32)


def _inproj(x, sc, sh, nw, wm, ws):
    b, t, d = x.shape
    n = b * t
    m = wm.shape[1]
    tm = _tile(n, 1024, 8)
    tn = _tile(m, 1536, LANES)
    gb, r, nt = _row_groups(b, t, tm)
    return pl.pallas_call(
        _inproj_kernel,
        out_shape=(jax.ShapeDtypeStruct((n, m), F32), jax.ShapeDtypeStruct((n, LANES), F32)),
        grid=(n // tm, m // tn),
        in_specs=[pl.BlockSpec((gb, r, d), lambda i, j: (i // nt, i % nt, 0)),
                  pl.BlockSpec((gb, 1, d), lambda i, j: (i // nt, 0, 0)),
                  pl.BlockSpec((gb, 1, d), lambda i, j: (i // nt, 0, 0)),
                  pl.BlockSpec((1, 1, d), lambda i, j: (0, 0, 0)),
                  pl.BlockSpec((d, tn), lambda i, j: (0, j)),
                  pl.BlockSpec((d, LANES), lambda i, j: (0, 0))],
        out_specs=(pl.BlockSpec((tm, tn), lambda i, j: (i, j)),
                   pl.BlockSpec((tm, LANES), lambda i, j: (i, 0))),
        scratch_shapes=[pltpu.VMEM((tm, d), BF16)],
        compiler_params=_params("arbitrary", "arbitrary"),
        name="inproj",
    )(x, sc, sh, nw, wm, ws)


def _causal_conv(x_ref, xs_ref, cw_ref):
    t = x_ref.shape[1]
    xs_ref[:, CONV_PAD:CONV_PAD + t, :] = x_ref[...]
    cw = cw_ref[...]
    y = None
    for k in range(CONV_W):
        lo = CONV_PAD - (CONV_W - 1) + k
        term = xs_ref[:, lo:lo + t, :] * cw[k:k + 1, :][None]
        y = term if y is None else y + term
    xs_ref[:, CONV_PAD - (CONV_W - 1):CONV_PAD, :] = xs_ref[:, t + CONV_PAD - (CONV_W - 1):t + CONV_PAD, :]
    return y


def _lru_kernel(xl_ref, yl_ref, cs_ref, h0_ref, cw_ref, cb_ref, wg_ref, ba_ref, bx_ref, lam_ref,
                o_ref, hn_ref, xs_ref, a_ref, u_ref, h_ref, *, reset_first):
    ti = pl.program_id(1)
    bb, tc, lw = xl_ref.shape
    nheads = lw // LANES

    @pl.when(ti == 0)
    def _():
        xs_ref[:, CONV_PAD - (CONV_W - 1):CONV_PAD, :] = cs_ref[...]
        h_ref[...] = h0_ref[...]

    xc = (_causal_conv(xl_ref, xs_ref, cw_ref) + cb_ref[...][None]).reshape(bb * tc, lw)
    sp = _softplus(-lam_ref[...])
    t_glob = lax.broadcasted_iota(jnp.int32, (bb, tc, LANES), 1) + ti * tc
    for hd in range(nheads):
        sl = slice(hd * LANES, (hd + 1) * LANES)
        xh = xc[:, sl]
        gates = jnp.dot(xh.astype(BF16), wg_ref[hd], preferred_element_type=F32)
        r = jax.nn.sigmoid(gates[:, :LANES] + ba_ref[:, sl])
        i = jax.nn.sigmoid(gates[:, LANES:] + bx_ref[:, sl])
        log_a = -LRU_C * r * sp[:, sl]
        a = jnp.exp(log_a)
        mult = jnp.sqrt(jnp.maximum(-jnp.tanh(log_a) * (a * a + 1.0), 0.0))
        u = i * xh
        a3 = a.reshape(bb, tc, LANES)
        m3 = mult.reshape(bb, tc, LANES)
        if reset_first:
            m3 = jnp.where(t_glob == 0, 1.0, m3)
        a_ref[:, :, sl] = a3
        u_ref[:, :, sl] = m3 * u.reshape(bb, tc, LANES)

    def step(t, h):
        h = a_ref[:, pl.ds(t, 1), :] * h + u_ref[:, pl.ds(t, 1), :]
        u_ref[:, pl.ds(t, 1), :] = h
        return h

    h = lax.fori_loop(0, tc, step, h_ref[...], unroll=8)
    h_ref[...] = h
    hn_ref[...] = h
    o_ref[...] = u_ref[...] * jax.nn.gelu(yl_ref[...])


def _lru(pm3, conv_state, h0, cw, cb, wg, ba, bx, lam, *, reset_first, bb, tc):
    b, t, _ = pm3.shape
    lw = lam.shape[-1]
    assert b % bb == 0 and t % tc == 0 and tc % 8 == 0
    row = lambda v: v.reshape(1, lw)
    kern = functools.partial(_lru_kernel, reset_first=reset_first)
    out, hn = pl.pallas_call(
        kern,
        out_shape=(jax.ShapeDtypeStruct((b, t, lw), F32), jax.ShapeDtypeStruct((b, 1, lw), F32)),
        grid=(b // bb, t // tc),
        in_specs=[pl.BlockSpec((bb, tc, lw), lambda i, j: (i, j, 0)),
                  pl.BlockSpec((bb, tc, lw), lambda i, j: (i, j, 1)),
                  pl.BlockSpec((bb, CONV_W - 1, lw), lambda i, j: (i, 0, 0)),
                  pl.BlockSpec((bb, 1, lw), lambda i, j: (i, 0, 0)),
                  pl.BlockSpec((CONV_W, lw), lambda i, j: (0, 0)),
                  pl.BlockSpec((1, lw), lambda i, j: (0, 0)),
                  pl.BlockSpec(wg.shape, lambda i, j: (0, 0, 0)),
                  pl.BlockSpec((1, lw), lambda i, j: (0, 0)),
                  pl.BlockSpec((1, lw), lambda i, j: (0, 0)),
                  pl.BlockSpec((1, lw), lambda i, j: (0, 0))],
        out_specs=(pl.BlockSpec((bb, tc, lw), lambda i, j: (i, j, 0)),
                   pl.BlockSpec((bb, 1, lw), lambda i, j: (i, 0, 0))),
        scratch_shapes=[pltpu.VMEM((bb, tc + CONV_PAD, lw), F32),
                        pltpu.VMEM((bb, tc, lw), F32),
                        pltpu.VMEM((bb, tc, lw), F32),
                        pltpu.VMEM((bb, 1, lw), F32)],
        compiler_params=_params("arbitrary", "arbitrary"),
        name="lru",
    )(pm3, pm3, conv_state, h0.reshape(b, 1, lw), cw, row(cb), wg, row(ba), row(bx), row(lam))
    return out, hn.reshape(b, lw)


def _delta_kernel(q_ref, k_ref, v_ref, z_ref, ps_ref, csq_ref, csk_ref, csv_ref, s0_ref,
                  cwq_ref, cwk_ref, cwv_ref, ea_ref, dtb_ref, nw_ref,
                  o_ref, sn_ref,
                  xq_ref, xk_ref, xv_ref, qn_ref, kn_ref, vn_ref, beta_ref, g_ref, *, chunk, nheads):
    ti = pl.program_id(1)
    bb, tb, w = q_ref.shape
    n_chunks = tb // chunk
    stacked = n_chunks == 1
    units = 1 if stacked else bb
    segs = bb if stacked else 1
    rows = segs * chunk
    lo = CONV_PAD - (CONV_W - 1)

    @pl.when(ti == 0)
    def _():
        xq_ref[:, lo:CONV_PAD, :] = csq_ref[...]
        xk_ref[:, lo:CONV_PAD, :] = csk_ref[...]
        xv_ref[:, lo:CONV_PAD, :] = csv_ref[...]
        sn_ref[...] = s0_ref[...]

    def conv_silu(x_ref, xs_ref, cw_ref):
        y = _causal_conv(x_ref, xs_ref, cw_ref)
        return (y * jax.nn.sigmoid(y)).reshape(bb * tb, w)

    qc = conv_silu(q_ref, xq_ref, cwq_ref)
    kc = conv_silu(k_ref, xk_ref, cwk_ref)
    vn_ref[...] = conv_silu(v_ref, xv_ref, cwv_ref)
    for hd in range(nheads):
        sl = slice(hd * LANES, (hd + 1) * LANES)
        qh = qc[:, sl]
        kh = kc[:, sl]
        qn_ref[:, sl] = qh * (lax.rsqrt(jnp.sum(qh * qh, axis=-1, keepdims=True) + NORM_EPS) * (LANES ** -0.5))
        kn_ref[:, sl] = kh * lax.rsqrt(jnp.sum(kh * kh, axis=-1, keepdims=True) + NORM_EPS)
    ps = ps_ref[...].reshape(bb * tb, LANES)
    beta_ref[...] = jax.nn.sigmoid(ps)
    g_ref[...] = -ea_ref[...] * _softplus(ps + dtb_ref[...])

    ii = lax.broadcasted_iota(jnp.int32, (rows, rows), 0)
    jj = lax.broadcasted_iota(jnp.int32, (rows, rows), 1)
    if segs == 1:
        le = jj <= ii
        lt = jj < ii
    else:
        same = (ii // chunk) == (jj // chunk)
        le = jnp.logical_and(same, jj <= ii)
        lt = jnp.logical_and(same, jj < ii)
    le_b = jnp.where(le, 1.0, 0.0).astype(BF16)
    assert chunk & (chunk - 1) == 0 and chunk >= 2
    levels = []
    b = 1
    while b < chunk:
        sh = b.bit_length()
        join = jnp.logical_and((ii >> sh) == (jj >> sh), (ii >> (sh - 1)) != (jj >> (sh - 1)))
        levels.append(jnp.logical_and(join, lt))
        b *= 2
    lanes = [slice(hd * LANES, (hd + 1) * LANES) for hd in range(nheads)]
    probs = [(u, hd) for u in range(units) for hd in range(nheads)]

    def mm(a, b):
        return jnp.dot(a.astype(BF16), b.astype(BF16), preferred_element_type=F32)

    def do_chunk(c, carry):
        rsl, gc, gct, eg, beta = [], [], [], [], []
        for u in range(units):
            rs = pl.ds(pl.multiple_of(u * tb + c * rows, rows), rows)
            g = g_ref[rs, :]
            g_hi = g.astype(BF16)
            g_lo = (g - g_hi.astype(F32)).astype(BF16)
            gcu = (jnp.dot(le_b, g_hi, preferred_element_type=F32)
                   + jnp.dot(le_b, g_lo, preferred_element_type=F32))
            gcp = gcu if rows == LANES else jnp.concatenate([gcu, jnp.zeros((LANES - rows, LANES), F32)], axis=0)
            rsl.append(rs)
            gc.append(gcu)
            gct.append(gcp.T)
            eg.append(jnp.exp(gcu))
            beta.append(beta_ref[rs, :])
        q = [qn_ref[rsl[u], lanes[hd]] for u, hd in probs]
        k = [kn_ref[rsl[u], lanes[hd]] for u, hd in probs]
        v = [vn_ref[rsl[u], lanes[hd]] for u, hd in probs]
        bcol = [beta[u][:, hd:hd + 1] for u, hd in probs]
        gcol = [gc[u][:, nheads + hd:nheads + hd + 1] for u, hd in probs]
        egc = [eg[u][:, nheads + hd:nheads + hd + 1] for u, hd in probs]
        np_ = range(len(probs))
        decay = [jnp.where(le, jnp.exp(jnp.where(le, gcol[i] - gct[u][nheads + hd:nheads + hd + 1, :rows], 0.0)), 0.0)
                 for i, (u, hd) in enumerate(probs)]
        kb = [k[i] * bcol[i] for i in np_]
        k16 = [k[i].astype(BF16) for i in np_]
        kk = [lax.dot_general(kb[i].astype(BF16), k16[i], NT_DIMS, preferred_element_type=F32) for i in np_]
        qk = [lax.dot_general(q[i].astype(BF16), k16[i], NT_DIMS, preferred_element_type=F32) for i in np_]
        lmat = [jnp.where(lt, kk[i] * decay[i], 0.0) for i in np_]
        attn = [jnp.where(le, qk[i] * decay[i], 0.0).astype(BF16) for i in np_]
        nmat = [jnp.where(levels[0], -lmat[i], 0.0) for i in np_]
        for lvl in levels[1:]:
            off = [jnp.where(lvl, lmat[i], 0.0) for i in np_]
            m = [off[i] + mm(nmat[i], off[i]) for i in np_]
            nmat = [nmat[i] - m[i] - mm(m[i], nmat[i]) for i in np_]
        rhs = [jnp.concatenate([v[i] * bcol[i], kb[i] * egc[i]], axis=1) for i in np_]
        sol = [rhs[i] + mm(nmat[i], rhs[i]) for i in np_]
        qe = [q[i] * egc[i] for i in np_]
        seq = lambda u, b: b if stacked else u
        s_old = [[sn_ref[seq(u, b), hd] for b in range(segs)] for u, hd in probs]
        both = [[jnp.dot(jnp.concatenate([sol[i][b * chunk:(b + 1) * chunk, LANES:],
                                          qe[i][b * chunk:(b + 1) * chunk]], axis=0).astype(BF16),
                         s_old[i][b].astype(BF16), preferred_element_type=F32) for b in range(segs)] for i in np_]
        cat = lambda parts: parts[0] if segs == 1 else jnp.concatenate(parts, axis=0)
        v16 = [(sol[i][:, :LANES] - cat([both[i][b][:chunk] for b in range(segs)])).astype(BF16) for i in np_]
        o = [cat([both[i][b][chunk:] for b in range(segs)]) + jnp.dot(attn[i], v16[i], preferred_element_type=F32)
             for i in np_]
        for i, (u, hd) in enumerate(probs):
            for b in range(segs):
                seg = slice(b * chunk, (b + 1) * chunk)
                g_last = gcol[i][(b + 1) * chunk - 1:(b + 1) * chunk]
                kd = (k[i][seg] * jnp.exp(g_last - gcol[i][seg])).astype(BF16)
                sn_ref[seq(u, b), hd] = s_old[i][b] * jnp.exp(g_last) + lax.dot_general(
                    kd, v16[i][seg], TN_DIMS, preferred_element_type=F32)
        for i, (u, hd) in enumerate(probs):
            sl = lanes[hd]
            on = o[i] * lax.rsqrt(jnp.mean(o[i] * o[i], axis=-1, keepdims=True) + NORM_EPS) * nw_ref[...]
            if stacked:
                zz = z_ref[:, :, sl].reshape(rows, LANES)
                o_ref[:, :, sl] = (on * (zz * jax.nn.sigmoid(zz))).reshape(bb, chunk, LANES)
            else:
                tsl = pl.ds(pl.multiple_of(c * chunk, chunk), chunk)
                zz = z_ref[u, tsl, sl]
                o_ref[u, tsl, sl] = on * (zz * jax.nn.sigmoid(zz))
        return carry

    if n_chunks == 1:
        do_chunk(0, 0)
    else:
        lax.fori_loop(0, n_chunks, do_chunk, 0)


def _delta(pm3, ps3, conv_state, s0, cw, a_log, dt_bias, norm_w, *, bb, tb, chunk):
    b, t, _ = pm3.shape
    nheads = a_log.shape[-1]
    w = nheads * LANES
    assert s0.shape == (b, nheads, LANES, LANES) and nheads <= 8
    assert b % bb == 0 and t % tb == 0 and tb % chunk == 0 and chunk % 8 == 0
    assert tb > chunk or bb * chunk <= LANES
    pad = lambda v: jnp.zeros((1, LANES), F32).at[0, nheads:2 * nheads].set(v)
    kern = functools.partial(_delta_kernel, chunk=chunk, nheads=nheads)
    col = lambda c: pl.BlockSpec((bb, tb, w), lambda i, j, c=c: (i, j, c))
    cst = lambda c: pl.BlockSpec((bb, CONV_W - 1, w), lambda i, j, c=c: (i, 0, c))
    cwt = lambda c: pl.BlockSpec((CONV_W, w), lambda i, j, c=c: (0, c))
    vec = pl.BlockSpec((1, LANES), lambda i, j: (0, 0))
    sblk = pl.BlockSpec((bb, nheads, LANES, LANES), lambda i, j: (i, 0, 0, 0))
    out, sn = pl.pallas_call(
        kern,
        out_shape=(jax.ShapeDtypeStruct((b, t, w), F32), jax.ShapeDtypeStruct(s0.shape, F32)),
        grid=(b // bb, t // tb),
        in_specs=[col(2), col(3), col(4), col(5),
                  pl.BlockSpec((bb, tb, LANES), lambda i, j: (i, j, 0)),
                  cst(0), cst(1), cst(2), sblk, cwt(0), cwt(1), cwt(2), vec, vec, vec],
        out_specs=(pl.BlockSpec((bb, tb, w), lambda i, j: (i, j, 0)), sblk),
        scratch_shapes=[pltpu.VMEM((bb, tb + CONV_PAD, w), F32)] * 3
                       + [pltpu.VMEM((bb * tb, w), F32)] * 3
                       + [pltpu.VMEM((bb * tb, LANES), F32)] * 2,
        compiler_params=_params("arbitrary", "arbitrary"),
        name="delta",
    )(pm3, pm3, pm3, pm3, ps3, conv_state, conv_state, conv_state, s0, cw, cw, cw,
      pad(jnp.exp(a_log.astype(F32))), pad(dt_bias.astype(F32)), norm_w.reshape(1, LANES))
    return out, sn


def _outproj_kernel(ol_ref, od_ref, x_ref, g1_ref, sc_ref, sh_ref, nw_ref, wt_ref, wb_ref, x1_ref, h2_ref):
    tm = h2_ref.shape[0]
    a = ol_ref[...].reshape(tm, ol_ref.shape[-1]).astype(BF16)
    b = od_ref[...].reshape(tm, od_ref.shape[-1]).astype(BF16)
    mix = (jnp.dot(a, wt_ref[...], preferred_element_type=F32)
           + jnp.dot(b, wb_ref[...], preferred_element_type=F32))
    x1 = x_ref[...] + g1_ref[...] * mix.reshape(x_ref.shape)
    x1_ref[...] = x1
    y = x1 * lax.rsqrt(jnp.mean(x1 * x1, axis=-1, keepdims=True) + NORM_EPS) * nw_ref[...]
    h2_ref[...] = (y * (1.0 + sc_ref[...]) + sh_ref[...]).reshape(h2_ref.shape).astype(BF16)


def _outproj(ol, od, x, g1, sc, sh, nw, wt, wb):
    b, t, d = x.shape
    n = b * t
    lw, dw = ol.shape[-1], od.shape[-1]
    tm = _tile(n, 512, 16)
    gb, r, nt = _row_groups(b, t, tm)
    tok = lambda c: pl.BlockSpec((gb, r, c), lambda i: (i // nt, i % nt, 0))
    per = pl.BlockSpec((gb, 1, d), lambda i: (i // nt, 0, 0))
    return pl.pallas_call(
        _outproj_kernel,
        out_shape=(jax.ShapeDtypeStruct((b, t, d), F32), jax.ShapeDtypeStruct((n, d), BF16)),
        grid=(n // tm,),
        in_specs=[tok(lw), tok(dw), tok(d), per, per, per,
                  pl.BlockSpec((1, 1, d), lambda i: (0, 0, 0)),
                  pl.BlockSpec((lw, d), lambda i: (0, 0)),
                  pl.BlockSpec((dw, d), lambda i: (0, 0))],
        out_specs=(tok(d), pl.BlockSpec((tm, d), lambda i: (i, 0))),
        compiler_params=_params("arbitrary"),
        name="outproj",
    )(ol, od, x, g1, sc, sh, nw, wt, wb)


def _sorting_network(n):
    pairs = []
    p = 1
    while p < n:
        k = p
        while k >= 1:
            for j in range(k % p, n - k, 2 * k):
                for i in range(min(k, n - j - k)):
                    if (i + j) // (2 * p) == (i + j + k) // (2 * p):
                        pairs.append((i + j, i + j + k))
            k //= 2
        p *= 2
    return pairs


def _merge_top(lists, singles, count):
    lists = list(lists)
    singles = list(singles)
    vals = []
    for r in range(count):
        head = lists[0]
        for x in singles:
            head = jnp.maximum(head, x)
        m = jnp.max(head, axis=0, keepdims=True)
        vals.append(m)
        left = count - r - 1
        if left == 0:
            break
        eq = lists[0] == m
        keep = min(len(lists), left)
        lists = [jnp.where(eq, lists[v + 1] if v + 1 < len(lists) else -jnp.inf, lists[v]) for v in range(keep)]
        singles = [jnp.where(x == m, -jnp.inf, x) for x in singles]
    return vals


def _top_values(s, count):
    tiles = [s[8 * v:8 * (v + 1)] for v in range(s.shape[0] // 8)]
    for i, j in _sorting_network(len(tiles)):
        tiles[i], tiles[j] = jnp.maximum(tiles[i], tiles[j]), jnp.minimum(tiles[i], tiles[j])
    return _merge_top(tiles, [], count)


def _router_kernel(h_ref, wq_ref, sk_ref, s2_ref, thr_ref, r_ref, q_ref, *, nheads):
    nk = sk_ref.shape[1]
    half = sk_ref.shape[2]
    q_ref[...] = lax.dot_general(wq_ref[...], h_ref[...], NT_DIMS, preferred_element_type=F32)
    p = h_ref.shape[0]
    sub = lax.broadcasted_iota(jnp.int32, (8, p), 0)

    def head(hd, carry):
        base = pl.multiple_of(hd * (2 * half), 2 * half)
        q1 = q_ref[pl.ds(base, half), :].astype(BF16)
        q2 = q_ref[pl.ds(base + half, half), :].astype(BF16)
        s1 = jnp.dot(sk_ref[2 * hd], q1, preferred_element_type=F32)
        s2 = jnp.dot(sk_ref[2 * hd + 1], q2, preferred_element_type=F32)
        nv = PEER_TOPK + 1
        t1 = _top_values(s1, nv)
        t2 = _top_values(s2, nv)
        t2_lo = jnp.concatenate(t2[:8], axis=0)
        t2_hi = jnp.concatenate(t2[8:16], axis=0)
        lists = [t1[0] + t2_lo]
        for a in range(1, nv):
            nb = nv // (a + 1)
            c = t1[a] + t2_lo
            lists.append(c if nb >= 8 else jnp.where(sub < nb, c, -jnp.inf))
        singles = [t1[0] + t2_hi, jnp.where(sub < 1, t1[0] + t2[16], -jnp.inf)]
        cand = jnp.concatenate(lists + singles, axis=0)
        best = _merge_top(lists, singles, nv)
        tau = best[PEER_TOPK - 1]
        mid = 0.5 * (tau + best[PEER_TOPK])
        top = t1[0] + t2[0]
        z = jnp.sum(jnp.where(cand >= tau, jnp.exp(cand - top), 0.0), axis=0, keepdims=True)
        s2_ref[hd] = s2 * LOG2E
        thr_ref[hd] = (mid - s1) * LOG2E
        r_ref[hd] = (s1 - (top + jnp.log(z))) * LOG2E
        return carry

    lax.fori_loop(0, nheads, head, 0)


def _router(h2, wq_t, sk, *, tp):
    n, d = h2.shape
    nheads = sk.shape[0] // 2
    nk = sk.shape[1]
    hq = wq_t.shape[0]
    out = jax.ShapeDtypeStruct((nheads, nk, n), F32)
    oblk = pl.BlockSpec((nheads, nk, tp), lambda i: (0, 0, i))
    return pl.pallas_call(
        functools.partial(_router_kernel, nheads=nheads),
        out_shape=(out, out, out),
        grid=(n // tp,),
        in_specs=[pl.BlockSpec((tp, d), lambda i: (i, 0)),
                  pl.BlockSpec((hq, d), lambda i: (0, 0)),
                  pl.BlockSpec(sk.shape, lambda i: (0, 0, 0))],
        out_specs=(oblk, oblk, oblk),
        scratch_shapes=[pltpu.VMEM((hq, tp), F32)],
        compiler_params=_params("arbitrary"),
        name="router",
    )(h2, wq_t, sk)


def _dense_kernel(h_ref, u_ref, vt_ref, s2_ref, thr_ref, r_ref, x1_ref, g2_ref, fw_ref,
                  y_ref, acc_ref, st_ref, coef_ref, *, nheads, rows_per_step, n_chunks):
    s = pl.program_id(1)
    nk = s2_ref.shape[1]
    tp = h_ref.shape[0]

    @pl.when(s == 0)
    def _():
        acc_ref[...] = jnp.zeros_like(acc_ref)
        st_ref[...] = jnp.zeros_like(st_ref)

    cur = s % 2
    prev = 1 - cur
    chunk = jnp.clip(s - 1, 0, n_chunks - 1)
    for ii in range(rows_per_step):
        i1 = chunk * rows_per_step + ii
        rows = slice(ii * nk, (ii + 1) * nk)
        thr_rows = [thr_ref[hd, pl.ds(i1, 1), :] for hd in range(nheads)]
        r_rows = [r_ref[hd, pl.ds(i1, 1), :] for hd in range(nheads)]
        for pb in range(tp // LANES):
            cols = slice(pb * LANES, (pb + 1) * LANES)
            gate = None
            for hd in range(nheads):
                s2 = s2_ref[hd, :, cols]
                term = jnp.where(s2 >= thr_rows[hd][:, cols], jnp.exp2(s2 + r_rows[hd][:, cols]), 0.0)
                gate = term if gate is None else gate + term
            coef_ref[rows, cols] = (_gelu(st_ref[prev, rows, cols]) * gate).astype(BF16)
    acc_ref[...] += jnp.dot(vt_ref[0], coef_ref[...], preferred_element_type=F32)
    st_ref[cur] = lax.dot_general(u_ref[...], h_ref[...], NT_DIMS, preferred_element_type=F32)

    @pl.when(s == pl.num_programs(1) - 1)
    def _():
        peer = acc_ref[...].T.reshape(x1_ref.shape)
        x2 = x1_ref[...] + g2_ref[...] * peer
        y_ref[...] = x2 * lax.rsqrt(jnp.mean(x2 * x2, axis=-1, keepdims=True) + NORM_EPS) * fw_ref[...]


def _dense(h2, u16, vt16, s2t, thr, rl, x1, g2, fw, *, tp):
    b, t, d = x1.shape
    n = b * t
    nheads, nk, _ = s2t.shape
    n_chunks, _, ce = vt16.shape
    rows_per_step = ce // nk
    assert u16.shape[0] == n_chunks * ce
    gb, r, nt = _row_groups(b, t, tp)
    rblk = pl.BlockSpec((nheads, nk, tp), lambda i, s: (0, 0, i))
    tok = pl.BlockSpec((gb, r, d), lambda i, s: (i // nt, i % nt, 0))
    return pl.pallas_call(
        functools.partial(_dense_kernel, nheads=nheads, rows_per_step=rows_per_step, n_chunks=n_chunks),
        out_shape=jax.ShapeDtypeStruct((b, t, d), F32),
        grid=(n // tp, n_chunks + 1),
        in_specs=[pl.BlockSpec((tp, d), lambda i, s: (i, 0)),
                  pl.BlockSpec((ce, d), lambda i, s: (jnp.minimum(s, n_chunks - 1), 0)),
                  pl.BlockSpec((1, d, ce), lambda i, s: (jnp.clip(s - 1, 0, n_chunks - 1), 0, 0)),
                  rblk, rblk, rblk, tok,
                  pl.BlockSpec((gb, 1, d), lambda i, s: (i // nt, 0, 0)),
                  pl.BlockSpec((1, 1, d), lambda i, s: (0, 0, 0))],
        out_specs=tok,
        scratch_shapes=[pltpu.VMEM((d, tp), F32), pltpu.VMEM((2, ce, tp), F32), pltpu.VMEM((ce, tp), BF16)],
        compiler_params=_params("arbitrary", "arbitrary"),
        name="dense",
    )(h2, u16, vt16, s2t, thr, rl, x1, g2, fw)


def _group(x, mod, lru_conv, lru_h, dn_conv, dn_s, reset_first, p, fw, *, lru_tiles, dn_tiles, tp):
    b, t, d = x.shape
    sh1, sc1, g1, sh2, sc2, g2 = [m.reshape(b, 1, d) for m in jnp.split(mod, 6, axis=-1)]
    pm, ps = _inproj(x, sc1, sh1, p['norm_mix_w'], p['w_main'], p['w_small'])
    pm3 = pm.reshape(b, t, -1)
    ps3 = ps.reshape(b, t, LANES)
    lw = p['lru_lambda'].shape[-1]
    out_lru, new_h = _lru(pm3, lru_conv, lru_h, p['lru_conv_w'], p['lru_conv_b'], p['lru_wg'], p['lru_ba'],
                          p['lru_bx'], p['lru_lambda'], reset_first=reset_first, bb=lru_tiles[0], tc=lru_tiles[1])
    out_dn, new_s = _delta(pm3, ps3, dn_conv, dn_s, p['dn_conv_w'], p['dn_A_log'], p['dn_dt_bias'], p['dn_norm_w'],
                           bb=dn_tiles[0], tb=dn_tiles[1], chunk=dn_tiles[2])
    keep = CONV_W - 1
    new_lru_conv = pm3[:, t - keep:, :lw]
    new_dn_conv = pm3[:, t - keep:, 2 * lw:2 * lw + dn_conv.shape[-1]]
    x1, h2 = _outproj(out_lru, out_dn, x, g1, sc2, sh2, p['norm_ffn_w'], p['w_out_top'], p['w_out_bot'])
    s2t, thr, r = _router(h2, p['wq_t'], p['subkeys'], tp=tp)
    y = _dense(h2, p['peer_u'], p['peer_vt'], s2t, thr, r, x1, g2, fw, tp=tp)
    return y, (new_lru_conv, new_h, new_dn_conv, new_s)


def kernel(x_prompt, x_sample, state_lru_conv, state_lru_h, state_dn_conv, state_dn_S, c_prompt, c_sample,
           w_ada, b_ada, norm_mix_w, norm_ffn_w, w_in, lru_conv_w, lru_conv_b, lru_wa, lru_ba, lru_wx, lru_bx,
           lru_lambda, dn_conv_w, dn_A_log, dn_dt_bias, dn_norm_w, w_out, peer_wq, peer_subkeys, peer_u, peer_v,
           final_norm_w):
    depth = w_ada.shape[0]
    assert depth == 1, "the final norm is fused into the last layer's expert kernel"
    bp, seq, d = x_prompt.shape
    bs, dseq, _ = x_sample.shape
    lw = lru_lambda.shape[-1]
    dn_heads = dn_A_log.shape[-1]
    dn_w = dn_heads * LANES
    conv_ch = dn_conv_w.shape[-1]
    assert lru_wa.shape[-1] == LANES and lw == dn_w and conv_ch == 3 * dn_w
    assert peer_subkeys.shape[-1] == LANES and peer_subkeys.shape[-2] == LANES
    main = 2 * lw + conv_ch + dn_w
    l = 0
    w_small = jnp.zeros((d, LANES), F32).at[:, :2 * dn_heads].set(w_in[l][:, main:]).astype(BF16)
    nheads = peer_subkeys.shape[1]
    p = {
        'norm_mix_w': norm_mix_w[l].reshape(1, 1, d), 'norm_ffn_w': norm_ffn_w[l].reshape(1, 1, d),
        'w_main': w_in[l][:, :main].astype(BF16), 'w_small': w_small,
        'lru_conv_w': lru_conv_w[l], 'lru_conv_b': lru_conv_b[l],
        'lru_wg': jnp.concatenate([lru_wa[l], lru_wx[l]], axis=-1).astype(BF16),
        'lru_ba': lru_ba[l], 'lru_bx': lru_bx[l], 'lru_lambda': lru_lambda[l],
        'dn_conv_w': dn_conv_w[l], 'dn_A_log': dn_A_log[l], 'dn_dt_bias': dn_dt_bias[l], 'dn_norm_w': dn_norm_w[l],
        'w_out_top': w_out[l][:lw].astype(BF16), 'w_out_bot': w_out[l][lw:].astype(BF16),
        'wq_t': peer_wq[l].T.astype(BF16),
        'subkeys': peer_subkeys[l].reshape(nheads * 2, LANES, LANES).astype(BF16),
        'peer_u': peer_u[l].astype(BF16),
        'peer_vt': peer_v[l].reshape(-1, PEER_CHUNK, d).transpose(0, 2, 1).astype(BF16),
    }
    fw = final_norm_w.reshape(1, 1, d)
    mod = _ada(jnp.concatenate([c_prompt, c_sample], axis=0), w_ada[l], b_ada[l])
    zeros = lambda *s: jnp.zeros(s, F32)
    yp, sp = _group(x_prompt, mod[:bp], zeros(bp, CONV_W - 1, lw), zeros(bp, lw), zeros(bp, CONV_W - 1, conv_ch),
                    zeros(bp, dn_heads, LANES, LANES), True, p, fw,
                    lru_tiles=(bp, _tile(seq, 128, 8)), dn_tiles=(_tile(bp, 4, 1), _tile(seq, 128, DN_CHUNK), min(DN_CHUNK, seq)),
                    tp=_tile(bp * seq, 512, LANES))
    ys, ss = _group(x_sample, mod[bp:], state_lru_conv[l], state_lru_h[l], state_dn_conv[l], state_dn_S[l],
                    False, p, fw,
                    lru_tiles=(_tile(bs, 32, 1), dseq), dn_tiles=(_tile(bs, DN_CHUNK // dseq, 1), dseq, dseq),
                    tp=_tile(bs * dseq, 512, LANES))
    stack = lambda v: v[None]
    return (yp, ys, stack(sp[0]), stack(sp[1]), stack(sp[2]), stack(sp[3]),
            stack(ss[0]), stack(ss[1]), stack(ss[2]), stack(ss[3]))
```

```python
import functools
import math

import jax
import jax.numpy as jnp
from jax import lax
from jax.experimental import pallas as pl
from jax.experimental.pallas import tpu as pltpu

F32 = jnp.float32
BF16 = jnp.bfloat16
NORM_EPS = 1e-6
LRU_C = 8.0
CONV_W = 4
CONV_PAD = 8
PEER_TOPK = 16
PEER_CHUNK = 512
DN_CHUNK = 64
LOG2E = math.log2(math.e)
LANES = 128
VMEM_LIMIT_BYTES = 60 * 1024 * 1024
NT_DIMS = (((1,), (1,)), ((), ()))
TN_DIMS = (((0,), (0,)), ((), ()))


def _params(*semantics):
    return pltpu.CompilerParams(dimension_semantics=semantics, vmem_limit_bytes=VMEM_LIMIT_BYTES)


def _tile(n, target, multiple):
    if n <= target:
        return n
    t = (target // multiple) * multiple
    while t >= multiple:
        if n % t == 0:
            return t
        t -= multiple
    raise ValueError(f"no tile for {n} <= {target} in multiples of {multiple}")


def _softplus(x):
    return jnp.maximum(x, 0.0) + jnp.log1p(jnp.exp(-jnp.abs(x)))


_GELU_C0 = -2.0 * math.sqrt(2.0 / math.pi) * LOG2E
_GELU_C1 = 0.044715 * _GELU_C0


def _gelu(x):
    return x / (1.0 + jnp.exp2(x * (_GELU_C0 + _GELU_C1 * (x * x))))


def _row_groups(batch, seq, tm):
    r = min(seq, tm)
    assert seq % r == 0 and tm % r == 0 and (batch * seq) % tm == 0
    return tm // r, r, seq // r


def _ada_kernel(c_ref, w_ref, b_ref, o_ref):
    c = c_ref[...]
    a = (c * jax.nn.sigmoid(c)).astype(BF16)
    o_ref[...] = jnp.dot(a, w_ref[...].astype(BF16), preferred_element_type=F32) + b_ref[...]


def _ada(c, w, b):
    bc, d = c.shape
    n = w.shape[1]
    tn = _tile(n, 1024, LANES)
    return pl.pallas_call(
        _ada_kernel,
        out_shape=jax.ShapeDtypeStruct((bc, n), F32),
        grid=(n // tn,),
        in_specs=[pl.BlockSpec((bc, d), lambda j: (0, 0)),
                  pl.BlockSpec((d, tn), lambda j: (0, j)),
                  pl.BlockSpec((1, tn), lambda j: (0, j))],
        out_specs=pl.BlockSpec((bc, tn), lambda j: (0, j)),
        compiler_params=_params("arbitrary"),
        name="ada",
    )(c, w, b.reshape(1, n))


def _inproj_kernel(x_ref, sc_ref, sh_ref, nw_ref, wm_ref, ws_ref, om_ref, os_ref, h_ref):
    @pl.when(pl.program_id(1) == 0)
    def _():
        x = x_ref[...]
        y = x * lax.rsqrt(jnp.mean(x * x, axis=-1, keepdims=True) + NORM_EPS) * nw_ref[...]
        h = y * (1.0 + sc_ref[...]) + sh_ref[...]
        hb = h.reshape(h_ref.shape).astype(BF16)
        h_ref[...] = hb
        os_ref[...] = jnp.dot(hb, ws_ref[...], preferred_element_type=F32)

    om_ref[...] = jnp.dot(h_ref[...], wm_ref[...], preferred_element_type=F32)


def _inproj(x, sc, sh, nw, wm, ws):
    b, t, d = x.shape
    n = b * t
    m = wm.shape[1]
    tm = _tile(n, 1024, 8)
    tn = _tile(m, 1536, LANES)
    gb, r, nt = _row_groups(b, t, tm)
    return pl.pallas_call(
        _inproj_kernel,
        out_shape=(jax.ShapeDtypeStruct((n, m), F32), jax.ShapeDtypeStruct((n, LANES), F32)),
        grid=(n // tm, m // tn),
        in_specs=[pl.BlockSpec((gb, r, d), lambda i, j: (i // nt, i % nt, 0)),
                  pl.BlockSpec((gb, 1, d), lambda i, j: (i // nt, 0, 0)),
                  pl.BlockSpec((gb, 1, d), lambda i, j: (i // nt, 0, 0)),
                  pl.BlockSpec((1, 1, d), lambda i, j: (0, 0, 0)),
                  pl.BlockSpec((d, tn), lambda i, j: (0, j)),
                  pl.BlockSpec((d, LANES), lambda i, j: (0, 0))],
        out_specs=(pl.BlockSpec((tm, tn), lambda i, j: (i, j)),
                   pl.BlockSpec((tm, LANES), lambda i, j: (i, 0))),
        scratch_shapes=[pltpu.VMEM((tm, d), BF16)],
        compiler_params=_params("arbitrary", "arbitrary"),
        name="inproj",
    )(x, sc, sh, nw, wm, ws)


def _causal_conv(x_ref, xs_ref, cw_ref):
    t = x_ref.shape[1]
    xs_ref[:, CONV_PAD:CONV_PAD + t, :] = x_ref[...]
    cw = cw_ref[...]
    y = None
    for k in range(CONV_W):
        lo = CONV_PAD - (CONV_W - 1) + k
        term = xs_ref[:, lo:lo + t, :] * cw[k:k + 1, :][None]
        y = term if y is None else y + term
    xs_ref[:, CONV_PAD - (CONV_W - 1):CONV_PAD, :] = xs_ref[:, t + CONV_PAD - (CONV_W - 1):t + CONV_PAD, :]
    return y


def _lru_kernel(xl_ref, yl_ref, cs_ref, h0_ref, cw_ref, cb_ref, wg_ref, ba_ref, bx_ref, lam_ref,
                o_ref, hn_ref, xs_ref, a_ref, u_ref, h_ref, *, reset_first):
    ti = pl.program_id(1)
    bb, tc, lw = xl_ref.shape
    nheads = lw // LANES

    @pl.when(ti == 0)
    def _():
        xs_ref[:, CONV_PAD - (CONV_W - 1):CONV_PAD, :] = cs_ref[...]
        h_ref[...] = h0_ref[...]

    xc = (_causal_conv(xl_ref, xs_ref, cw_ref) + cb_ref[...][None]).reshape(bb * tc, lw)
    sp = _softplus(-lam_ref[...])
    t_glob = lax.broadcasted_iota(jnp.int32, (bb, tc, LANES), 1) + ti * tc
    for hd in range(nheads):
        sl = slice(hd * LANES, (hd + 1) * LANES)
        xh = xc[:, sl]
        gates = jnp.dot(xh.astype(BF16), wg_ref[hd], preferred_element_type=F32)
        r = jax.nn.sigmoid(gates[:, :LANES] + ba_ref[:, sl])
        i = jax.nn.sigmoid(gates[:, LANES:] + bx_ref[:, sl])
        log_a = -LRU_C * r * sp[:, sl]
        a = jnp.exp(log_a)
        mult = jnp.sqrt(jnp.maximum(-jnp.tanh(log_a) * (a * a + 1.0), 0.0))
        u = i * xh
        a3 = a.reshape(bb, tc, LANES)
        m3 = mult.reshape(bb, tc, LANES)
        if reset_first:
            m3 = jnp.where(t_glob == 0, 1.0, m3)
        a_ref[:, :, sl] = a3
        u_ref[:, :, sl] = m3 * u.reshape(bb, tc, LANES)

    def step(t, h):
        h = a_ref[:, pl.ds(t, 1), :] * h + u_ref[:, pl.ds(t, 1), :]
        u_ref[:, pl.ds(t, 1), :] = h
        return h

    h = lax.fori_loop(0, tc, step, h_ref[...], unroll=8)
    h_ref[...] = h
    hn_ref[...] = h
    o_ref[...] = u_ref[...] * _gelu(yl_ref[...])


def _lru(pm3, conv_state, h0, cw, cb, wg, ba, bx, lam, *, reset_first, bb, tc):
    b, t, _ = pm3.shape
    lw = lam.shape[-1]
    assert b % bb == 0 and t % tc == 0 and tc % 8 == 0
    row = lambda v: v.reshape(1, lw)
    kern = functools.partial(_lru_kernel, reset_first=reset_first)
    out, hn = pl.pallas_call(
        kern,
        out_shape=(jax.ShapeDtypeStruct((b, t, lw), F32), jax.ShapeDtypeStruct((b, 1, lw), F32)),
        grid=(b // bb, t // tc),
        in_specs=[pl.BlockSpec((bb, tc, lw), lambda i, j: (i, j, 0)),
                  pl.BlockSpec((bb, tc, lw), lambda i, j: (i, j, 1)),
                  pl.BlockSpec((bb, CONV_W - 1, lw), lambda i, j: (i, 0, 0)),
                  pl.BlockSpec((bb, 1, lw), lambda i, j: (i, 0, 0)),
                  pl.BlockSpec((CONV_W, lw), lambda i, j: (0, 0)),
                  pl.BlockSpec((1, lw), lambda i, j: (0, 0)),
                  pl.BlockSpec(wg.shape, lambda i, j: (0, 0, 0)),
                  pl.BlockSpec((1, lw), lambda i, j: (0, 0)),
                  pl.BlockSpec((1, lw), lambda i, j: (0, 0)),
                  pl.BlockSpec((1, lw), lambda i, j: (0, 0))],
        out_specs=(pl.BlockSpec((bb, tc, lw), lambda i, j: (i, j, 0)),
                   pl.BlockSpec((bb, 1, lw), lambda i, j: (i, 0, 0))),
        scratch_shapes=[pltpu.VMEM((bb, tc + CONV_PAD, lw), F32),
                        pltpu.VMEM((bb, tc, lw), F32),
                        pltpu.VMEM((bb, tc, lw), F32),
                        pltpu.VMEM((bb, 1, lw), F32)],
        compiler_params=_params("arbitrary", "arbitrary"),
        name="lru",
    )(pm3, pm3, conv_state, h0.reshape(b, 1, lw), cw, row(cb), wg, row(ba), row(bx), row(lam))
    return out, hn.reshape(b, lw)


def _delta_kernel(q_ref, k_ref, v_ref, z_ref, ps_ref, csq_ref, csk_ref, csv_ref, s0_ref,
                  cwq_ref, cwk_ref, cwv_ref, ea_ref, dtb_ref, nw_ref,
                  o_ref, sn_ref,
                  xq_ref, xk_ref, xv_ref, qn_ref, kn_ref, vn_ref, beta_ref, g_ref, *, chunk, nheads):
    ti = pl.program_id(1)
    bb, tb, w = q_ref.shape
    n_chunks = tb // chunk
    stacked = n_chunks == 1
    units = 1 if stacked else bb
    segs = bb if stacked else 1
    rows = segs * chunk
    lo = CONV_PAD - (CONV_W - 1)

    @pl.when(ti == 0)
    def _():
        xq_ref[:, lo:CONV_PAD, :] = csq_ref[...]
        xk_ref[:, lo:CONV_PAD, :] = csk_ref[...]
        xv_ref[:, lo:CONV_PAD, :] = csv_ref[...]
        sn_ref[...] = s0_ref[...]

    def conv_silu(x_ref, xs_ref, cw_ref):
        y = _causal_conv(x_ref, xs_ref, cw_ref)
        return (y * jax.nn.sigmoid(y)).reshape(bb * tb, w)

    qc = conv_silu(q_ref, xq_ref, cwq_ref)
    kc = conv_silu(k_ref, xk_ref, cwk_ref)
    vn_ref[...] = conv_silu(v_ref, xv_ref, cwv_ref)
    for hd in range(nheads):
        sl = slice(hd * LANES, (hd + 1) * LANES)
        qh = qc[:, sl]
        kh = kc[:, sl]
        qn_ref[:, sl] = qh * (lax.rsqrt(jnp.sum(qh * qh, axis=-1, keepdims=True) + NORM_EPS) * (LANES ** -0.5))
        kn_ref[:, sl] = kh * lax.rsqrt(jnp.sum(kh * kh, axis=-1, keepdims=True) + NORM_EPS)
    ps = ps_ref[...].reshape(bb * tb, LANES)
    beta_ref[...] = jax.nn.sigmoid(ps)
    g_ref[...] = -ea_ref[...] * _softplus(ps + dtb_ref[...])

    ii = lax.broadcasted_iota(jnp.int32, (rows, rows), 0)
    jj = lax.broadcasted_iota(jnp.int32, (rows, rows), 1)
    if segs == 1:
        le = jj <= ii
        lt = jj < ii
    else:
        same = (ii // chunk) == (jj // chunk)
        le = jnp.logical_and(same, jj <= ii)
        lt = jnp.logical_and(same, jj < ii)
    le_b = jnp.where(le, 1.0, 0.0).astype(BF16)
    assert chunk & (chunk - 1) == 0 and chunk >= 2
    levels = []
    b = 1
    while b < chunk:
        sh = b.bit_length()
        join = jnp.logical_and((ii >> sh) == (jj >> sh), (ii >> (sh - 1)) != (jj >> (sh - 1)))
        levels.append(jnp.logical_and(join, lt))
        b *= 2
    lanes = [slice(hd * LANES, (hd + 1) * LANES) for hd in range(nheads)]
    probs = [(u, hd) for u in range(units) for hd in range(nheads)]

    def mm(a, b):
        return jnp.dot(a.astype(BF16), b.astype(BF16), preferred_element_type=F32)

    def do_chunk(c, carry):
        rsl, gc, gct, eg, beta = [], [], [], [], []
        for u in range(units):
            rs = pl.ds(pl.multiple_of(u * tb + c * rows, rows), rows)
            g = g_ref[rs, :]
            g_hi = g.astype(BF16)
            g_lo = (g - g_hi.astype(F32)).astype(BF16)
            gcu = (jnp.dot(le_b, g_hi, preferred_element_type=F32)
                   + jnp.dot(le_b, g_lo, preferred_element_type=F32))
            gcp = gcu if rows == LANES else jnp.concatenate([gcu, jnp.zeros((LANES - rows, LANES), F32)], axis=0)
            rsl.append(rs)
            gc.append(gcu)
            gct.append(gcp.T)
            eg.append(jnp.exp(gcu))
            beta.append(beta_ref[rs, :])
        q = [qn_ref[rsl[u], lanes[hd]] for u, hd in probs]
        k = [kn_ref[rsl[u], lanes[hd]] for u, hd in probs]
        v = [vn_ref[rsl[u], lanes[hd]] for u, hd in probs]
        bcol = [beta[u][:, hd:hd + 1] for u, hd in probs]
        gcol = [gc[u][:, nheads + hd:nheads + hd + 1] for u, hd in probs]
        egc = [eg[u][:, nheads + hd:nheads + hd + 1] for u, hd in probs]
        np_ = range(len(probs))
        decay = [jnp.where(le, jnp.exp(jnp.where(le, gcol[i] - gct[u][nheads + hd:nheads + hd + 1, :rows], 0.0)), 0.0)
                 for i, (u, hd) in enumerate(probs)]
        kb = [k[i] * bcol[i] for i in np_]
        k16 = [k[i].astype(BF16) for i in np_]
        kk = [lax.dot_general(kb[i].astype(BF16), k16[i], NT_DIMS, preferred_element_type=F32) for i in np_]
        qk = [lax.dot_general(q[i].astype(BF16), k16[i], NT_DIMS, preferred_element_type=F32) for i in np_]
        lmat = [jnp.where(lt, kk[i] * decay[i], 0.0) for i in np_]
        attn = [jnp.where(le, qk[i] * decay[i], 0.0).astype(BF16) for i in np_]
        nmat = [jnp.where(levels[0], -lmat[i], 0.0) for i in np_]
        for lvl in levels[1:]:
            off = [jnp.where(lvl, lmat[i], 0.0) for i in np_]
            m = [off[i] + mm(nmat[i], off[i]) for i in np_]
            nmat = [nmat[i] - m[i] - mm(m[i], nmat[i]) for i in np_]
        rhs = [jnp.concatenate([v[i] * bcol[i], kb[i] * egc[i]], axis=1) for i in np_]
        sol = [rhs[i] + mm(nmat[i], rhs[i]) for i in np_]
        qe = [q[i] * egc[i] for i in np_]
        seq = lambda u, b: b if stacked else u
        s_old = [[sn_ref[seq(u, b), hd] for b in range(segs)] for u, hd in probs]
        both = [[jnp.dot(jnp.concatenate([sol[i][b * chunk:(b + 1) * chunk, LANES:],
                                          qe[i][b * chunk:(b + 1) * chunk]], axis=0).astype(BF16),
                         s_old[i][b].astype(BF16), preferred_element_type=F32) for b in range(segs)] for i in np_]
        cat = lambda parts: parts[0] if segs == 1 else jnp.concatenate(parts, axis=0)
        v16 = [(sol[i][:, :LANES] - cat([both[i][b][:chunk] for b in range(segs)])).astype(BF16) for i in np_]
        o = [cat([both[i][b][chunk:] for b in range(segs)]) + jnp.dot(attn[i], v16[i], preferred_element_type=F32)
             for i in np_]
        for i, (u, hd) in enumerate(probs):
            for b in range(segs):
                seg = slice(b * chunk, (b + 1) * chunk)
                g_last = gcol[i][(b + 1) * chunk - 1:(b + 1) * chunk]
                kd = (k[i][seg] * jnp.exp(g_last - gcol[i][seg])).astype(BF16)
                sn_ref[seq(u, b), hd] = s_old[i][b] * jnp.exp(g_last) + lax.dot_general(
                    kd, v16[i][seg], TN_DIMS, preferred_element_type=F32)
        for i, (u, hd) in enumerate(probs):
            sl = lanes[hd]
            on = o[i] * lax.rsqrt(jnp.mean(o[i] * o[i], axis=-1, keepdims=True) + NORM_EPS) * nw_ref[...]
            if stacked:
                zz = z_ref[:, :, sl].reshape(rows, LANES)
                o_ref[:, :, sl] = (on * (zz * jax.nn.sigmoid(zz))).reshape(bb, chunk, LANES)
            else:
                tsl = pl.ds(pl.multiple_of(c * chunk, chunk), chunk)
                zz = z_ref[u, tsl, sl]
                o_ref[u, tsl, sl] = on * (zz * jax.nn.sigmoid(zz))
        return carry

    if n_chunks == 1:
        do_chunk(0, 0)
    else:
        lax.fori_loop(0, n_chunks, do_chunk, 0)


def _delta(pm3, ps3, conv_state, s0, cw, a_log, dt_bias, norm_w, *, bb, tb, chunk):
    b, t, _ = pm3.shape
    nheads = a_log.shape[-1]
    w = nheads * LANES
    assert s0.shape == (b, nheads, LANES, LANES) and nheads <= 8
    assert b % bb == 0 and t % tb == 0 and tb % chunk == 0 and chunk % 8 == 0
    assert tb > chunk or bb * chunk <= LANES
    pad = lambda v: jnp.zeros((1, LANES), F32).at[0, nheads:2 * nheads].set(v)
    kern = functools.partial(_delta_kernel, chunk=chunk, nheads=nheads)
    col = lambda c: pl.BlockSpec((bb, tb, w), lambda i, j, c=c: (i, j, c))
    cst = lambda c: pl.BlockSpec((bb, CONV_W - 1, w), lambda i, j, c=c: (i, 0, c))
    cwt = lambda c: pl.BlockSpec((CONV_W, w), lambda i, j, c=c: (0, c))
    vec = pl.BlockSpec((1, LANES), lambda i, j: (0, 0))
    sblk = pl.BlockSpec((bb, nheads, LANES, LANES), lambda i, j: (i, 0, 0, 0))
    out, sn = pl.pallas_call(
        kern,
        out_shape=(jax.ShapeDtypeStruct((b, t, w), F32), jax.ShapeDtypeStruct(s0.shape, F32)),
        grid=(b // bb, t // tb),
        in_specs=[col(2), col(3), col(4), col(5),
                  pl.BlockSpec((bb, tb, LANES), lambda i, j: (i, j, 0)),
                  cst(0), cst(1), cst(2), sblk, cwt(0), cwt(1), cwt(2), vec, vec, vec],
        out_specs=(pl.BlockSpec((bb, tb, w), lambda i, j: (i, j, 0)), sblk),
        scratch_shapes=[pltpu.VMEM((bb, tb + CONV_PAD, w), F32)] * 3
                       + [pltpu.VMEM((bb * tb, w), F32)] * 3
                       + [pltpu.VMEM((bb * tb, LANES), F32)] * 2,
        compiler_params=_params("arbitrary", "arbitrary"),
        name="delta",
    )(pm3, pm3, pm3, pm3, ps3, conv_state, conv_state, conv_state, s0, cw, cw, cw,
      pad(jnp.exp(a_log.astype(F32))), pad(dt_bias.astype(F32)), norm_w.reshape(1, LANES))
    return out, sn


def _outproj_kernel(ol_ref, od_ref, x_ref, g1_ref, sc_ref, sh_ref, nw_ref, wt_ref, wb_ref, x1_ref, h2_ref):
    tm = h2_ref.shape[0]
    a = ol_ref[...].reshape(tm, ol_ref.shape[-1]).astype(BF16)
    b = od_ref[...].reshape(tm, od_ref.shape[-1]).astype(BF16)
    mix = (jnp.dot(a, wt_ref[...], preferred_element_type=F32)
           + jnp.dot(b, wb_ref[...], preferred_element_type=F32))
    x1 = x_ref[...] + g1_ref[...] * mix.reshape(x_ref.shape)
    x1_ref[...] = x1
    y = x1 * lax.rsqrt(jnp.mean(x1 * x1, axis=-1, keepdims=True) + NORM_EPS) * nw_ref[...]
    h2_ref[...] = (y * (1.0 + sc_ref[...]) + sh_ref[...]).reshape(h2_ref.shape).astype(BF16)


def _outproj(ol, od, x, g1, sc, sh, nw, wt, wb):
    b, t, d = x.shape
    n = b * t
    lw, dw = ol.shape[-1], od.shape[-1]
    tm = _tile(n, 512, 16)
    gb, r, nt = _row_groups(b, t, tm)
    tok = lambda c: pl.BlockSpec((gb, r, c), lambda i: (i // nt, i % nt, 0))
    per = pl.BlockSpec((gb, 1, d), lambda i: (i // nt, 0, 0))
    return pl.pallas_call(
        _outproj_kernel,
        out_shape=(jax.ShapeDtypeStruct((b, t, d), F32), jax.ShapeDtypeStruct((n, d), BF16)),
        grid=(n // tm,),
        in_specs=[tok(lw), tok(dw), tok(d), per, per, per,
                  pl.BlockSpec((1, 1, d), lambda i: (0, 0, 0)),
                  pl.BlockSpec((lw, d), lambda i: (0, 0)),
                  pl.BlockSpec((dw, d), lambda i: (0, 0))],
        out_specs=(tok(d), pl.BlockSpec((tm, d), lambda i: (i, 0))),
        compiler_params=_params("arbitrary"),
        name="outproj",
    )(ol, od, x, g1, sc, sh, nw, wt, wb)


def _sorting_network(n):
    pairs = []
    p = 1
    while p < n:
        k = p
        while k >= 1:
            for j in range(k % p, n - k, 2 * k):
                for i in range(min(k, n - j - k)):
                    if (i + j) // (2 * p) == (i + j + k) // (2 * p):
                        pairs.append((i + j, i + j + k))
            k //= 2
        p *= 2
    return pairs


def _merge_top(lists, singles, count):
    lists = list(lists)
    singles = list(singles)
    vals = []
    for r in range(count):
        head = lists[0]
        for x in singles:
            head = jnp.maximum(head, x)
        m = jnp.max(head, axis=0, keepdims=True)
        vals.append(m)
        left = count - r - 1
        if left == 0:
            break
        eq = lists[0] == m
        keep = min(len(lists), left)
        lists = [jnp.where(eq, lists[v + 1] if v + 1 < len(lists) else -jnp.inf, lists[v]) for v in range(keep)]
        singles = [jnp.where(x == m, -jnp.inf, x) for x in singles]
    return vals


def _top_values(s, count):
    tiles = [s[8 * v:8 * (v + 1)] for v in range(s.shape[0] // 8)]
    for i, j in _sorting_network(len(tiles)):
        tiles[i], tiles[j] = jnp.maximum(tiles[i], tiles[j]), jnp.minimum(tiles[i], tiles[j])
    return _merge_top(tiles, [], count)


def _router_kernel(h_ref, wq_ref, sk_ref, s2_ref, thr_ref, r_ref, q_ref, *, nheads):
    nk = sk_ref.shape[1]
    half = sk_ref.shape[2]
    q_ref[...] = lax.dot_general(wq_ref[...], h_ref[...], NT_DIMS, preferred_element_type=F32)
    p = h_ref.shape[0]
    sub = lax.broadcasted_iota(jnp.int32, (8, p), 0)

    def head(hd, carry):
        base = pl.multiple_of(hd * (2 * half), 2 * half)
        q1 = q_ref[pl.ds(base, half), :].astype(BF16)
        q2 = q_ref[pl.ds(base + half, half), :].astype(BF16)
        s1 = jnp.dot(sk_ref[2 * hd], q1, preferred_element_type=F32)
        s2 = jnp.dot(sk_ref[2 * hd + 1], q2, preferred_element_type=F32)
        nv = PEER_TOPK + 1
        t1 = _top_values(s1, nv)
        t2 = _top_values(s2, nv)
        t2_lo = jnp.concatenate(t2[:8], axis=0)
        t2_hi = jnp.concatenate(t2[8:16], axis=0)
        lists = [t1[0] + t2_lo]
        for a in range(1, nv):
            nb = nv // (a + 1)
            c = t1[a] + t2_lo
            lists.append(c if nb >= 8 else jnp.where(sub < nb, c, -jnp.inf))
        singles = [t1[0] + t2_hi, jnp.where(sub < 1, t1[0] + t2[16], -jnp.inf)]
        cand = jnp.concatenate(lists + singles, axis=0)
        best = _merge_top(lists, singles, nv)
        tau = best[PEER_TOPK - 1]
        mid = 0.5 * (tau + best[PEER_TOPK])
        top = t1[0] + t2[0]
        z = jnp.sum(jnp.where(cand >= tau, jnp.exp(cand - top), 0.0), axis=0, keepdims=True)
        s2_ref[hd] = s2 * LOG2E
        thr_ref[hd] = (mid - s1) * LOG2E
        r_ref[hd] = (s1 - (top + jnp.log(z))) * LOG2E
        return carry

    lax.fori_loop(0, nheads, head, 0)


def _router(h2, wq_t, sk, *, tp):
    n, d = h2.shape
    nheads = sk.shape[0] // 2
    nk = sk.shape[1]
    hq = wq_t.shape[0]
    out = jax.ShapeDtypeStruct((nheads, nk, n), F32)
    oblk = pl.BlockSpec((nheads, nk, tp), lambda i: (0, 0, i))
    return pl.pallas_call(
        functools.partial(_router_kernel, nheads=nheads),
        out_shape=(out, out, out),
        grid=(n // tp,),
        in_specs=[pl.BlockSpec((tp, d), lambda i: (i, 0)),
                  pl.BlockSpec((hq, d), lambda i: (0, 0)),
                  pl.BlockSpec(sk.shape, lambda i: (0, 0, 0))],
        out_specs=(oblk, oblk, oblk),
        scratch_shapes=[pltpu.VMEM((hq, tp), F32)],
        compiler_params=_params("arbitrary"),
        name="router",
    )(h2, wq_t, sk)


def _dense_kernel(h_ref, u_ref, vt_ref, s2_ref, thr_ref, r_ref, x1_ref, g2_ref, fw_ref,
                  y_ref, acc_ref, st_ref, coef_ref, *, nheads, rows_per_step, n_chunks):
    s = pl.program_id(1)
    nk = s2_ref.shape[1]
    tp = h_ref.shape[0]

    @pl.when(s == 0)
    def _():
        acc_ref[...] = jnp.zeros_like(acc_ref)
        st_ref[...] = jnp.zeros_like(st_ref)

    cur = s % 2
    prev = 1 - cur
    chunk = jnp.clip(s - 1, 0, n_chunks - 1)
    for ii in range(rows_per_step):
        i1 = chunk * rows_per_step + ii
        rows = slice(ii * nk, (ii + 1) * nk)
        thr_rows = [thr_ref[hd, pl.ds(i1, 1), :] for hd in range(nheads)]
        r_rows = [r_ref[hd, pl.ds(i1, 1), :] for hd in range(nheads)]
        for pb in range(tp // LANES):
            cols = slice(pb * LANES, (pb + 1) * LANES)
            gate = None
            for hd in range(nheads):
                s2 = s2_ref[hd, :, cols]
                term = jnp.where(s2 >= thr_rows[hd][:, cols], jnp.exp2(s2 + r_rows[hd][:, cols]), 0.0)
                gate = term if gate is None else gate + term
            coef_ref[rows, cols] = (_gelu(st_ref[prev, rows, cols]) * gate).astype(BF16)
    acc_ref[...] += jnp.dot(vt_ref[0], coef_ref[...], preferred_element_type=F32)
    st_ref[cur] = lax.dot_general(u_ref[...], h_ref[...], NT_DIMS, preferred_element_type=F32)

    @pl.when(s == pl.num_programs(1) - 1)
    def _():
        peer = acc_ref[...].T.reshape(x1_ref.shape)
        x2 = x1_ref[...] + g2_ref[...] * peer
        y_ref[...] = x2 * lax.rsqrt(jnp.mean(x2 * x2, axis=-1, keepdims=True) + NORM_EPS) * fw_ref[...]


def _dense(h2, u16, vt16, s2t, thr, rl, x1, g2, fw, *, tp):
    b, t, d = x1.shape
    n = b * t
    nheads, nk, _ = s2t.shape
    n_chunks, _, ce = vt16.shape
    rows_per_step = ce // nk
    assert u16.shape[0] == n_chunks * ce
    gb, r, nt = _row_groups(b, t, tp)
    rblk = pl.BlockSpec((nheads, nk, tp), lambda i, s: (0, 0, i))
    tok = pl.BlockSpec((gb, r, d), lambda i, s: (i // nt, i % nt, 0))
    return pl.pallas_call(
        functools.partial(_dense_kernel, nheads=nheads, rows_per_step=rows_per_step, n_chunks=n_chunks),
        out_shape=jax.ShapeDtypeStruct((b, t, d), F32),
        grid=(n // tp, n_chunks + 1),
        in_specs=[pl.BlockSpec((tp, d), lambda i, s: (i, 0)),
                  pl.BlockSpec((ce, d), lambda i, s: (jnp.minimum(s, n_chunks - 1), 0)),
                  pl.BlockSpec((1, d, ce), lambda i, s: (jnp.clip(s - 1, 0, n_chunks - 1), 0, 0)),
                  rblk, rblk, rblk, tok,
                  pl.BlockSpec((gb, 1, d), lambda i, s: (i // nt, 0, 0)),
                  pl.BlockSpec((1, 1, d), lambda i, s: (0, 0, 0))],
        out_specs=tok,
        scratch_shapes=[pltpu.VMEM((d, tp), F32), pltpu.VMEM((2, ce, tp), F32), pltpu.VMEM((ce, tp), BF16)],
        compiler_params=_params("arbitrary", "arbitrary"),
        name="dense",
    )(h2, u16, vt16, s2t, thr, rl, x1, g2, fw)


def _group(x, mod, lru_conv, lru_h, dn_conv, dn_s, reset_first, p, fw, *, lru_tiles, dn_tiles, tp):
    b, t, d = x.shape
    sh1, sc1, g1, sh2, sc2, g2 = [m.reshape(b, 1, d) for m in jnp.split(mod, 6, axis=-1)]
    pm, ps = _inproj(x, sc1, sh1, p['norm_mix_w'], p['w_main'], p['w_small'])
    pm3 = pm.reshape(b, t, -1)
    ps3 = ps.reshape(b, t, LANES)
    lw = p['lru_lambda'].shape[-1]
    out_lru, new_h = _lru(pm3, lru_conv, lru_h, p['lru_conv_w'], p['lru_conv_b'], p['lru_wg'], p['lru_ba'],
                          p['lru_bx'], p['lru_lambda'], reset_first=reset_first, bb=lru_tiles[0], tc=lru_tiles[1])
    out_dn, new_s = _delta(pm3, ps3, dn_conv, dn_s, p['dn_conv_w'], p['dn_A_log'], p['dn_dt_bias'], p['dn_norm_w'],
                           bb=dn_tiles[0], tb=dn_tiles[1], chunk=dn_tiles[2])
    keep = CONV_W - 1
    new_lru_conv = pm3[:, t - keep:, :lw]
    new_dn_conv = pm3[:, t - keep:, 2 * lw:2 * lw + dn_conv.shape[-1]]
    x1, h2 = _outproj(out_lru, out_dn, x, g1, sc2, sh2, p['norm_ffn_w'], p['w_out_top'], p['w_out_bot'])
    s2t, thr, r = _router(h2, p['wq_t'], p['subkeys'], tp=tp)
    y = _dense(h2, p['peer_u'], p['peer_vt'], s2t, thr, r, x1, g2, fw, tp=tp)
    return y, (new_lru_conv, new_h, new_dn_conv, new_s)


def kernel(x_prompt, x_sample, state_lru_conv, state_lru_h, state_dn_conv, state_dn_S, c_prompt, c_sample,
           w_ada, b_ada, norm_mix_w, norm_ffn_w, w_in, lru_conv_w, lru_conv_b, lru_wa, lru_ba, lru_wx, lru_bx,
           lru_lambda, dn_conv_w, dn_A_log, dn_dt_bias, dn_norm_w, w_out, peer_wq, peer_subkeys, peer_u, peer_v,
           final_norm_w):
    depth = w_ada.shape[0]
    assert depth == 1, "the final norm is fused into the last layer's expert kernel"
    bp, seq, d = x_prompt.shape
    bs, dseq, _ = x_sample.shape
    lw = lru_lambda.shape[-1]
    dn_heads = dn_A_log.shape[-1]
    dn_w = dn_heads * LANES
    conv_ch = dn_conv_w.shape[-1]
    assert lru_wa.shape[-1] == LANES and lw == dn_w and conv_ch == 3 * dn_w
    assert peer_subkeys.shape[-1] == LANES and peer_subkeys.shape[-2] == LANES
    main = 2 * lw + conv_ch + dn_w
    l = 0
    w_small = jnp.zeros((d, LANES), F32).at[:, :2 * dn_heads].set(w_in[l][:, main:]).astype(BF16)
    nheads = peer_subkeys.shape[1]
    p = {
        'norm_mix_w': norm_mix_w[l].reshape(1, 1, d), 'norm_ffn_w': norm_ffn_w[l].reshape(1, 1, d),
        'w_main': w_in[l][:, :main].astype(BF16), 'w_small': w_small,
        'lru_conv_w': lru_conv_w[l], 'lru_conv_b': lru_conv_b[l],
        'lru_wg': jnp.concatenate([lru_wa[l], lru_wx[l]], axis=-1).astype(BF16),
        'lru_ba': lru_ba[l], 'lru_bx': lru_bx[l], 'lru_lambda': lru_lambda[l],
        'dn_conv_w': dn_conv_w[l], 'dn_A_log': dn_A_log[l], 'dn_dt_bias': dn_dt_bias[l], 'dn_norm_w': dn_norm_w[l],
        'w_out_top': w_out[l][:lw].astype(BF16), 'w_out_bot': w_out[l][lw:].astype(BF16),
        'wq_t': peer_wq[l].T.astype(BF16),
        'subkeys': peer_subkeys[l].reshape(nheads * 2, LANES, LANES).astype(BF16),
        'peer_u': peer_u[l].astype(BF16),
        'peer_vt': peer_v[l].reshape(-1, PEER_CHUNK, d).transpose(0, 2, 1).astype(BF16),
    }
    fw = final_norm_w.reshape(1, 1, d)
    mod = _ada(jnp.concatenate([c_prompt, c_sample], axis=0), w_ada[l], b_ada[l])
    zeros = lambda *s: jnp.zeros(s, F32)
    yp, sp = _group(x_prompt, mod[:bp], zeros(bp, CONV_W - 1, lw), zeros(bp, lw), zeros(bp, CONV_W - 1, conv_ch),
                    zeros(bp, dn_heads, LANES, LANES), True, p, fw,
                    lru_tiles=(bp, _tile(seq, 128, 8)), dn_tiles=(_tile(bp, 4, 1), _tile(seq, 128, DN_CHUNK), min(DN_CHUNK, seq)),
                    tp=_tile(bp * seq, 512, LANES))
    ys, ss = _group(x_sample, mod[bp:], state_lru_conv[l], state_lru_h[l], state_dn_conv[l], state_dn_S[l],
                    False, p, fw,
                    lru_tiles=(_tile(bs, 32, 1), dseq), dn_tiles=(_tile(bs, DN_CHUNK // dseq, 1), dseq, dseq),
                    tp=_tile(bs * dseq, 512, LANES))
    stack = lambda v: v[None]
    return (yp, ys, stack(sp[0]), stack(sp[1]), stack(sp[2]), stack(sp[3]),
            stack(ss[0]), stack(ss[1]), stack(ss[2]), stack(ss[3]))
```
